```python
import math
import jax, jax.numpy as jnp
from jax import lax
import numpy as np

D_MODEL = 2048
BATCH = 2
SEQ = 4096
DEPTH = 1
DEC_BATCH = 128
DEC_SEQ = 1
PAST_LEN = 2048
PAGE_SIZE = 128

N_MEM = 256
ML_HEADS = 4
ML_DV = D_MODEL // ML_HEADS
ML_DQK = ML_DV // 2
ML_W = ML_HEADS * ML_DV
ML_QKW = ML_HEADS * ML_DQK
ML_CHUNK = 128
ML_I_BIAS = -2.0
ML_F_BIAS = 3.0
FOX_DH = 128
FOX_HEADS = D_MODEL // FOX_DH
FOX_W = FOX_HEADS * FOX_DH
FOX_F_BIAS = 2.0
Q_BLOCK = 128
MEM_HEADS = 4
MEM_DH = D_MODEL // MEM_HEADS
MEM_W = MEM_HEADS * MEM_DH
N_BRANCH = 3
EPS = 1e-6

IN_SPLITS = (ML_QKW, ML_QKW, ML_W, ML_HEADS, ML_HEADS, ML_W, ML_W,
             FOX_W, FOX_W, FOX_W, FOX_HEADS, FOX_W,
             MEM_W, MEM_W,
             N_BRANCH * D_MODEL)
IN_COLS = sum(IN_SPLITS)

kernel_name = "hybrid_mlstm_fox_memxattn_step"


def _rmsnorm(x, g):
    xf = x.astype(jnp.float32)
    y = xf * lax.rsqrt(jnp.mean(xf * xf, axis=-1, keepdims=True) + EPS)
    return (y * g.astype(jnp.float32)).astype(x.dtype)


def _split_cols(h):
    idx = np.cumsum(np.array(IN_SPLITS))[:-1].tolist()
    return jnp.split(h, idx, axis=-1)


def _mlstm_chunk(carry, inp):
    C, n, m = carry
    q, k, v, ig, lf = inp
    L = q.shape[2]
    b = jnp.cumsum(lf, axis=-1)
    a = b + m[..., None]
    causal = jnp.tril(jnp.ones((L, L), dtype=bool))
    dmat = jnp.where(causal, b[..., :, None] - b[..., None, :] + ig[..., None, :], -jnp.inf)
    m_t = jnp.maximum(a, jnp.max(dmat, axis=-1))
    w_inter = jnp.exp(a - m_t)
    w_intra = jnp.exp(dmat - m_t[..., None])
    s = jnp.einsum('bhtd,bhsd->bhts', q, k) * w_intra
    num = w_inter[..., None] * jnp.einsum('bhvd,bhtd->bhtv', C, q) + jnp.einsum('bhts,bhsv->bhtv', s, v)
    nq = w_inter * jnp.einsum('bhd,bhtd->bht', n, q) + jnp.sum(s, axis=-1)
    h = num / jnp.maximum(jnp.abs(nq), jnp.exp(-m_t))[..., None]
    m_new = m_t[..., -1]
    w_c = jnp.exp(a[..., -1] - m_new)
    w_s = w_intra[..., -1, :]
    C_new = w_c[..., None, None] * C + jnp.einsum('bhs,bhsv,bhsd->bhvd', w_s, v, k)
    n_new = w_c[..., None] * n + jnp.einsum('bhs,bhsd->bhd', w_s, k)
    return (C_new, n_new, m_new), h


def _mlstm(q, k, v, ig, lf, C0, n0, m0):
    B, T = q.shape[:2]
    L = math.gcd(T, ML_CHUNK)
    nc = T // L

    def to_chunks(a):
        a = a.reshape((B, nc, L) + a.shape[2:])
        return jnp.moveaxis(jnp.moveaxis(a, 3, 2), 1, 0)

    xs = (to_chunks(q), to_chunks(k), to_chunks(v), to_chunks(ig), to_chunks(lf))
    carry, h = lax.scan(_mlstm_chunk, (C0, n0, m0), xs)
    h = jnp.swapaxes(jnp.moveaxis(h, 0, 1), 2, 3).reshape(B, T, ML_HEADS, ML_DV)
    return h, carry


def _fox_attention(q, k, v, logf, q_offset):
    B, Tq = q.shape[:2]
    Tk = k.shape[1]
    F = jnp.swapaxes(jnp.cumsum(logf, axis=1), 1, 2)
    QB = math.gcd(Tq, Q_BLOCK)
    nb = Tq // QB
    k_pos = jnp.arange(Tk)
    scale = FOX_DH ** -0.5

    def block(i):
        start = i * QB
        qb = lax.dynamic_slice_in_dim(q, start, QB, axis=1)
        q_pos = q_offset + start + jnp.arange(QB)
        Fq = lax.dynamic_slice_in_dim(F, q_offset + start, QB, axis=2)
        s = jnp.einsum('bqhd,bkhd->bhqk', qb, k, preferred_element_type=jnp.float32) * scale
        s = s + Fq[..., :, None] - F[..., None, :]
        s = jnp.where(k_pos[None, :] <= q_pos[:, None], s, -jnp.inf)
        p = jax.nn.softmax(s, axis=-1)
        return jnp.einsum('bhqk,bkhd->bqhd', p.astype(v.dtype), v)

    out = lax.map(block, jnp.arange(nb))
    return jnp.moveaxis(out, 0, 1).reshape(B, Tq, FOX_HEADS, FOX_DH)


def _mem_attention(q, mk, mv):
    s = jnp.einsum('bthd,bmhd->bhtm', q, mk, preferred_element_type=jnp.float32) * (MEM_DH ** -0.5)
    p = jax.nn.softmax(s, axis=-1)
    return jnp.einsum('bhtm,bmhd->bthd', p.astype(mv.dtype), mv)


def _mem_kv(mem, norm_mem, w_mem_kv, mem_k_norm):
    B, N, _ = mem.shape
    kv = _rmsnorm(mem, norm_mem) @ w_mem_kv
    mk, mv = jnp.split(kv, 2, axis=-1)
    mk = _rmsnorm(mk.reshape(B, N, MEM_HEADS, MEM_DH), mem_k_norm)
    mv = mv.reshape(B, N, MEM_HEADS, MEM_DH)
    return mk, mv


def _gather_pages(pool, layer, page_table):
    rows = pool[layer, page_table]
    db, npg, ps = rows.shape[:3]
    return rows.reshape((db, npg * ps) + rows.shape[3:])


def _layer(x, mem_k, mem_v, past, C0, n0, m0, p):
    (norm_in, w_in, b_mlstm_i, b_mlstm_f, b_fox_f, mlstm_out_norm, fox_q_norm, fox_k_norm,
     mem_q_norm, w_br_mlstm, w_br_fox, w_br_mem, w_out) = p
    B, T, _ = x.shape
    f32 = jnp.float32
    h = _rmsnorm(x, norm_in) @ w_in
    (mq, mk, mv, mi, mf, mo, mz, fq, fk, fv, ff, fz, cq, cz, g) = _split_cols(h)

    mq = mq.reshape(B, T, ML_HEADS, ML_DQK).astype(f32) * (ML_DQK ** -0.5)
    mk = mk.reshape(B, T, ML_HEADS, ML_DQK).astype(f32)
    mv = mv.reshape(B, T, ML_HEADS, ML_DV).astype(f32)
    ig = mi.astype(f32) + b_mlstm_i.astype(f32)
    lf = jax.nn.log_sigmoid(mf.astype(f32) + b_mlstm_f.astype(f32))
    hm, (C1, n1, m1) = _mlstm(mq, mk, mv, ig, lf, C0.astype(f32), n0.astype(f32), m0.astype(f32))
    hm = _rmsnorm(hm, mlstm_out_norm.reshape(ML_HEADS, ML_DV)).reshape(B, T, ML_W).astype(x.dtype)
    hm = jax.nn.sigmoid(mo) * hm

    fq = _rmsnorm(fq.reshape(B, T, FOX_HEADS, FOX_DH), fox_q_norm)
    fk = _rmsnorm(fk.reshape(B, T, FOX_HEADS, FOX_DH), fox_k_norm)
    fv = fv.reshape(B, T, FOX_HEADS, FOX_DH)
    flf = jax.nn.log_sigmoid(ff.astype(f32) + b_fox_f.astype(f32))
    if past is None:
        keys, vals, lfs, q_off = fk, fv, flf, 0
    else:
        pk, pv, plf = past
        keys = jnp.concatenate([pk.astype(fk.dtype), fk], axis=1)
        vals = jnp.concatenate([pv.astype(fv.dtype), fv], axis=1)
        lfs = jnp.concatenate([plf.astype(f32), flf], axis=1)
        q_off = pk.shape[1]
    hf = _fox_attention(fq, keys, vals, lfs, q_off).reshape(B, T, FOX_W)

    cq = _rmsnorm(cq.reshape(B, T, MEM_HEADS, MEM_DH), mem_q_norm)
    hc = _mem_attention(cq, mem_k.astype(cq.dtype), mem_v.astype(cq.dtype)).reshape(B, T, MEM_W)

    br_m = (jax.nn.silu(mz) * hm) @ w_br_mlstm
    br_f = (jax.nn.silu(fz) * hf) @ w_br_fox
    br_c = (jax.nn.silu(cz) * hc) @ w_br_mem
    gm, gf, gc = jnp.split(jax.nn.sigmoid(g), N_BRANCH, axis=-1)
    y = x + (gm * br_m + gf * br_f + gc * br_c) @ w_out
    return y, fk, fv, flf, C1, n1, m1


def setup_inputs(seed: int = 0) -> dict:
    key = jax.random.key(seed)
    ks = jax.random.split(key, 32)
    nrm = jax.random.normal
    n_pages = PAST_LEN // PAGE_SIZE
    n_phys = (DEC_BATCH * n_pages * 5) // 4
    page_table = jax.random.permutation(ks[6], n_phys)[: DEC_BATCH * n_pages].reshape(DEC_BATCH, n_pages).astype(jnp.int32)
    return {
        "x_prompt": nrm(ks[0], (BATCH, SEQ, D_MODEL), jnp.float32),
        "x_sample": nrm(ks[1], (DEC_BATCH, DEC_SEQ, D_MODEL), jnp.float32),
        "mem_prompt": nrm(ks[2], (BATCH, N_MEM, D_MODEL), jnp.float32),
        "cache_fox_k": nrm(ks[3], (DEPTH, n_phys, PAGE_SIZE, FOX_HEADS, FOX_DH), jnp.float32),
        "cache_fox_v": nrm(ks[4], (DEPTH, n_phys, PAGE_SIZE, FOX_HEADS, FOX_DH), jnp.float32),
        "cache_fox_logf": jax.nn.log_sigmoid(FOX_F_BIAS + nrm(ks[5], (DEPTH, n_phys, PAGE_SIZE, FOX_HEADS), jnp.float32)),
        "page_table": page_table,
        "cache_mem_k": nrm(ks[7], (DEPTH, DEC_BATCH, N_MEM, MEM_HEADS, MEM_DH), jnp.float32),
        "cache_mem_v": nrm(ks[8], (DEPTH, DEC_BATCH, N_MEM, MEM_HEADS, MEM_DH), jnp.float32),
        "state_mlstm_C": 0.05 * nrm(ks[9], (DEPTH, DEC_BATCH, ML_HEADS, ML_DV, ML_DQK), jnp.float32),
        "state_mlstm_n": 0.05 * nrm(ks[10], (DEPTH, DEC_BATCH, ML_HEADS, ML_DQK), jnp.float32),
        "state_mlstm_m": jax.random.uniform(ks[11], (DEPTH, DEC_BATCH, ML_HEADS), jnp.float32, -2.0, 2.0),
        "norm_in": 1.0 + 0.02 * nrm(ks[12], (DEPTH, D_MODEL), jnp.float32),
        "norm_mem": 1.0 + 0.02 * nrm(ks[13], (DEPTH, D_MODEL), jnp.float32),
        "w_in": nrm(ks[14], (DEPTH, D_MODEL, IN_COLS), jnp.float32) * D_MODEL ** -0.5,
        "w_mem_kv": nrm(ks[15], (DEPTH, D_MODEL, 2 * MEM_W), jnp.float32) * D_MODEL ** -0.5,
        "b_mlstm_i": ML_I_BIAS + 0.1 * nrm(ks[16], (DEPTH, ML_HEADS), jnp.float32),
        "b_mlstm_f": ML_F_BIAS + 0.5 * nrm(ks[17], (DEPTH, ML_HEADS), jnp.float32),
        "b_fox_f": FOX_F_BIAS + 0.5 * nrm(ks[18], (DEPTH, FOX_HEADS), jnp.float32),
        "mlstm_out_norm": 1.0 + 0.02 * nrm(ks[19], (DEPTH, ML_W), jnp.float32),
        "fox_q_norm": 1.0 + 0.02 * nrm(ks[20], (DEPTH, FOX_DH), jnp.float32),
        "fox_k_norm": 1.0 + 0.02 * nrm(ks[21], (DEPTH, FOX_DH), jnp.float32),
        "mem_q_norm": 1.0 + 0.02 * nrm(ks[22], (DEPTH, MEM_DH), jnp.float32),
        "mem_k_norm": 1.0 + 0.02 * nrm(ks[23], (DEPTH, MEM_DH), jnp.float32),
        "w_br_mlstm": nrm(ks[24], (DEPTH, ML_W, D_MODEL), jnp.float32) * ML_W ** -0.5,
        "w_br_fox": nrm(ks[25], (DEPTH, FOX_W, D_MODEL), jnp.float32) * FOX_W ** -0.5,
        "w_br_mem": nrm(ks[26], (DEPTH, MEM_W, D_MODEL), jnp.float32) * MEM_W ** -0.5,
        "w_out": nrm(ks[27], (DEPTH, D_MODEL, D_MODEL), jnp.float32) * D_MODEL ** -0.5,
    }


def reference(x_prompt, x_sample, mem_prompt, cache_fox_k, cache_fox_v, cache_fox_logf, page_table,
              cache_mem_k, cache_mem_v, state_mlstm_C, state_mlstm_n, state_mlstm_m,
              norm_in, norm_mem, w_in, w_mem_kv, b_mlstm_i, b_mlstm_f, b_fox_f, mlstm_out_norm,
              fox_q_norm, fox_k_norm, mem_q_norm, mem_k_norm, w_br_mlstm, w_br_fox, w_br_mem, w_out):
    bp = x_prompt.shape[0]
    y_p, y_s = x_prompt, x_sample
    pk_l, pv_l, plf_l, pmk_l, pmv_l, pC_l, pn_l, pm_l = [], [], [], [], [], [], [], []
    sk_l, sv_l, slf_l, sC_l, sn_l, sm_l = [], [], [], [], [], []
    for l in range(DEPTH):
        p = (norm_in[l], w_in[l], b_mlstm_i[l], b_mlstm_f[l], b_fox_f[l], mlstm_out_norm[l],
             fox_q_norm[l], fox_k_norm[l], mem_q_norm[l], w_br_mlstm[l], w_br_fox[l], w_br_mem[l], w_out[l])
        mk_p, mv_p = _mem_kv(mem_prompt, norm_mem[l], w_mem_kv[l], mem_k_norm[l])
        C0 = jnp.zeros((bp, ML_HEADS, ML_DV, ML_DQK), jnp.float32)
        n0 = jnp.zeros((bp, ML_HEADS, ML_DQK), jnp.float32)
        m0 = jnp.zeros((bp, ML_HEADS), jnp.float32)
        y_p, pk, pv, plf, pC, pn, pm = _layer(y_p, mk_p, mv_p, None, C0, n0, m0, p)
        past = (_gather_pages(cache_fox_k, l, page_table),
                _gather_pages(cache_fox_v, l, page_table),
                _gather_pages(cache_fox_logf, l, page_table))
        y_s, sk, sv, slf, sC, sn, sm = _layer(y_s, cache_mem_k[l], cache_mem_v[l], past,
                                              state_mlstm_C[l], state_mlstm_n[l], state_mlstm_m[l], p)
        pk_l.append(pk); pv_l.append(pv); plf_l.append(plf); pmk_l.append(mk_p); pmv_l.append(mv_p)
        pC_l.append(pC); pn_l.append(pn); pm_l.append(pm)
        sk_l.append(sk); sv_l.append(sv); slf_l.append(slf); sC_l.append(sC); sn_l.append(sn); sm_l.append(sm)
    return (y_p, y_s,
            jnp.stack(pk_l), jnp.stack(pv_l), jnp.stack(plf_l), jnp.stack(pmk_l), jnp.stack(pmv_l),
            jnp.stack(pC_l), jnp.stack(pn_l), jnp.stack(pm_l),
            jnp.stack(sk_l), jnp.stack(sv_l), jnp.stack(slf_l), jnp.stack(sC_l), jnp.stack(sn_l), jnp.stack(sm_l))
```

```python
import functools
import math

import jax
import jax.numpy as jnp
from jax import lax
from jax.experimental import pallas as pl
from jax.experimental.pallas import tpu as pltpu

F32 = jnp.float32
BF16 = jnp.bfloat16

ML_HEADS = 4
FOX_HEADS = 16
MEM_HEADS = 4
N_BRANCH = 3
ML_CHUNK = 128
EPS = 1e-6
GATE_LANES = 128
VMEM_LIMIT = 56 * 1024 * 1024

NT_DIMS = (((1,), (1,)), ((), ()))
TN_DIMS = (((0,), (0,)), ((), ()))


def _params(*sem):
    return pltpu.CompilerParams(dimension_semantics=sem, vmem_limit_bytes=VMEM_LIMIT)


def _sigmoid(z):
    return 1.0 / (1.0 + jnp.exp(-z))


def _log_sigmoid(z):
    return jnp.minimum(z, 0.0) - jnp.log1p(jnp.exp(-jnp.abs(z)))


def _split3(x):
    x1 = x.astype(BF16)
    r1 = x - x1.astype(F32)
    x2 = r1.astype(BF16)
    r2 = r1 - x2.astype(F32)
    return x1, x2, r2.astype(BF16)


def _dot_sel_left(sel, x):
    x1, x2, x3 = _split3(x)
    d = lambda p: jnp.dot(sel, p, preferred_element_type=F32)
    return d(x1) + d(x2) + d(x3)


def _dot_sel_right(x, sel):
    x1, x2, x3 = _split3(x)
    d = lambda p: jnp.dot(p, sel, preferred_element_type=F32)
    return d(x1) + d(x2) + d(x3)


def _rms_rows(x, gain):
    return x * lax.rsqrt(jnp.mean(x * x, axis=-1, keepdims=True) + EPS) * gain


def _norm_proj_kernel(x_ref, g_ref, w_ref, *rest, head_norm):
    if head_norm:
        hg_ref, o_ref, xn_ref = rest
    else:
        o_ref, xn_ref = rest

    @pl.when(pl.program_id(1) == 0)
    def _():
        xn_ref[...] = _rms_rows(x_ref[...], g_ref[...]).astype(BF16)

    acc = jnp.dot(xn_ref[...], w_ref[...], preferred_element_type=F32)
    if head_norm:
        acc = _rms_rows(acc, hg_ref[...])
    o_ref[...] = acc.astype(o_ref.dtype)


def _norm_proj(x, gain, w, *, tm, tn, head_gain=None, out_dtype=F32):
    m, k = x.shape
    n = w.shape[1]
    tm = min(tm, m)
    assert m % tm == 0 and n % tn == 0
    in_specs = [
        pl.BlockSpec((tm, k), lambda i, j: (i, 0)),
        pl.BlockSpec((1, k), lambda i, j: (0, 0)),
        pl.BlockSpec((k, tn), lambda i, j: (0, j)),
    ]
    args = [x, gain.reshape(1, k), w]
    if head_gain is not None:
        assert head_gain.shape == (tn,)
        in_specs.append(pl.BlockSpec((1, tn), lambda i, j: (0, 0)))
        args.append(head_gain.reshape(1, tn))
    return pl.pallas_call(
        functools.partial(_norm_proj_kernel, head_norm=head_gain is not None),
        grid=(m // tm, n // tn),
        in_specs=in_specs,
        out_specs=pl.BlockSpec((tm, tn), lambda i, j: (i, j)),
        out_shape=jax.ShapeDtypeStruct((m, n), out_dtype),
        scratch_shapes=[pltpu.VMEM((tm, k), BF16)],
        compiler_params=_params("parallel", "arbitrary"),
        name="norm_proj",
    )(*args)


def _gates_kernel(x_ref, g_ref, w_ref, b_ref, o_ref):
    xn = _rms_rows(x_ref[...], g_ref[...])
    x1, x2, _ = _split3(xn)
    w1, w2, _ = _split3(w_ref[...])
    d = lambda a, b: jnp.dot(a, b, preferred_element_type=F32)
    z = d(x1, w1) + d(x1, w2) + d(x2, w1) + b_ref[...]
    lane = lax.broadcasted_iota(jnp.int32, z.shape, 1)
    o_ref[...] = jnp.where(lane < ML_HEADS, z, _log_sigmoid(z))


def _gates(x, gain, w_gate, bias, *, tm):
    m, k = x.shape
    tm = min(tm, m)
    return pl.pallas_call(
        _gates_kernel,
        grid=(m // tm,),
        in_specs=[
            pl.BlockSpec((tm, k), lambda i: (i, 0)),
            pl.BlockSpec((1, k), lambda i: (0, 0)),
            pl.BlockSpec((k, GATE_LANES), lambda i: (0, 0)),
            pl.BlockSpec((1, GATE_LANES), lambda i: (0, 0)),
        ],
        out_specs=pl.BlockSpec((tm, GATE_LANES), lambda i: (i, 0)),
        out_shape=jax.ShapeDtypeStruct((m, GATE_LANES), F32),
        compiler_params=_params("parallel"),
        name="gates",
    )(x, gain.reshape(1, k), w_gate, bias)


def _mlstm_chunk_kernel(q_ref, k_ref, v_ref, o_ref, z_ref, g_ref, gain_ref,
                        a_ref, c_out, n_out, m_out, c_sc, n_sc, m_sc, *, dqk):
    h = pl.program_id(1)
    c_idx = pl.program_id(2)
    L = q_ref.shape[0]

    @pl.when(c_idx == 0)
    def _():
        c_sc[...] = jnp.zeros_like(c_sc)
        n_sc[...] = jnp.zeros_like(n_sc)
        m_sc[...] = jnp.zeros_like(m_sc)

    row = lax.broadcasted_iota(jnp.int32, (L, L), 0)
    col = lax.broadcasted_iota(jnp.int32, (L, L), 1)
    causal = col <= row
    tri = jnp.where(causal, 1.0, 0.0).astype(BF16)

    g = g_ref[...]
    cum = _dot_sel_left(tri, g)
    g_t = g.T
    cum_t = cum.T
    lane = lax.broadcasted_iota(jnp.int32, (L, GATE_LANES), 1)
    sub = lax.broadcasted_iota(jnp.int32, (GATE_LANES, L), 0)
    pick_col = lambda x, idx: jnp.sum(jnp.where(lane == idx, x, 0.0), axis=1, keepdims=True)
    pick_row = lambda x, idx: jnp.sum(jnp.where(sub == idx, x, 0.0), axis=0, keepdims=True)
    ig_col = pick_col(g, h)
    b_col = pick_col(cum, ML_HEADS + h)
    ig_row = pick_row(g_t, h)
    b_row = pick_row(cum_t, ML_HEADS + h)

    m_prev = m_sc[...]
    a_col = b_col + m_prev
    dmat = jnp.where(causal, b_col - b_row + ig_row, -jnp.inf)
    m_t = jnp.maximum(a_col, jnp.max(dmat, axis=1, keepdims=True))
    w_inter = jnp.exp(a_col - m_t)
    w_intra = jnp.exp(dmat - m_t)

    qf = q_ref[...] * (dqk ** -0.5)
    kf = k_ref[...]
    q = qf.astype(BF16)
    k = kf.astype(BF16)
    v = v_ref[...].astype(BF16)
    c_prev = c_sc[...]
    n_prev = n_sc[...]

    s = lax.dot_general(q, k, NT_DIMS, preferred_element_type=F32) * w_intra
    inter = lax.dot_general(q, c_prev.astype(BF16), NT_DIMS, preferred_element_type=F32)
    num = w_inter * inter + jnp.dot(s.astype(BF16), v, preferred_element_type=F32)
    nq = w_inter * jnp.sum(qf * n_prev, axis=1, keepdims=True) + jnp.sum(s, axis=1, keepdims=True)
    hid = num / jnp.maximum(jnp.abs(nq), jnp.exp(-m_t))

    b_last = b_col[L - 1:L, :]
    m_new = m_t[L - 1:L, :]
    w_c = jnp.exp(b_last + m_prev - m_new)
    w_s = jnp.exp(b_last - b_col + ig_col - m_new)
    wv = (w_s * v_ref[...]).astype(BF16)
    c_new = w_c * c_prev + lax.dot_general(wv, k, TN_DIMS, preferred_element_type=F32)
    n_new = w_c * n_prev + jnp.sum(w_s * kf, axis=0, keepdims=True)
    c_sc[...] = c_new
    n_sc[...] = n_new
    m_sc[...] = m_new

    hn = _rms_rows(hid, gain_ref[...])
    zz = z_ref[...]
    a_ref[...] = (zz * _sigmoid(zz) * (_sigmoid(o_ref[...]) * hn)).astype(a_ref.dtype)

    @pl.when(c_idx == pl.num_programs(2) - 1)
    def _():
        c_out[...] = c_new
        n_out[...] = n_new
        m_out[...] = m_new


def _mlstm_prompt(h3, g3, out_gain, cols, *, dqk, dv):
    B, T, _ = h3.shape
    L = math.gcd(T, ML_CHUNK)
    nc = T // L
    qo, ko, vo, oo, zo = (cols[n] for n in ("mq", "mk", "mv", "mo", "mz"))
    blk = lambda w, off: pl.BlockSpec((None, L, w), lambda b, h, c, off=off, w=w: (b, c, off // w + h))
    kern = functools.partial(_mlstm_chunk_kernel, dqk=dqk)
    return pl.pallas_call(
        kern,
        grid=(B, ML_HEADS, nc),
        in_specs=[
            blk(dqk, qo), blk(dqk, ko), blk(dv, vo), blk(dv, oo), blk(dv, zo),
            pl.BlockSpec((None, L, GATE_LANES), lambda b, h, c: (b, c, 0)),
            pl.BlockSpec((1, dv), lambda b, h, c: (0, h)),
        ],
        out_specs=[
            pl.BlockSpec((None, L, dv), lambda b, h, c: (b, c, h)),
            pl.BlockSpec((None, None, dv, dqk), lambda b, h, c: (b, h, 0, 0)),
            pl.BlockSpec((None, None, 1, dqk), lambda b, h, c: (b, h, 0, 0)),
            pl.BlockSpec((None, None, 1, 1), lambda b, h, c: (b, h, 0, 0)),
        ],
        out_shape=[
            jax.ShapeDtypeStruct((B, T, ML_HEADS * dv), BF16),
            jax.ShapeDtypeStruct((B, ML_HEADS, dv, dqk), F32),
            jax.ShapeDtypeStruct((B, ML_HEADS, 1, dqk), F32),
            jax.ShapeDtypeStruct((B, ML_HEADS, 1, 1), F32),
        ],
        scratch_shapes=[pltpu.VMEM((dv, dqk), F32), pltpu.VMEM((1, dqk), F32), pltpu.VMEM((1, 1), F32)],
        compiler_params=_params("parallel", "parallel", "arbitrary"),
        name="mlstm_chunk",
    )(h3, h3, h3, h3, h3, g3, out_gain.reshape(1, ML_HEADS * dv))


def _mlstm_step_kernel(q_ref, k_ref, v_ref, o_ref, z_ref, g_ref, m0_ref, gain_ref, c0_ref, n0_ref,
                       a_ref, c_out, n_out, m_out, *, dqk, dv):
    g = g_ref[...]
    m0 = m0_ref[...]
    lane_h = lax.broadcasted_iota(jnp.int32, (1, ML_HEADS), 1)
    m_acc = jnp.zeros((1, ML_HEADS), F32)
    for h in range(ML_HEADS):
        ig = g[:, h:h + 1]
        lf = g[:, ML_HEADS + h:ML_HEADS + h + 1]
        a = lf + m0[:, h:h + 1]
        m_t = jnp.maximum(a, ig)
        w_inter = jnp.exp(a - m_t)
        w_intra = jnp.exp(ig - m_t)
        qf = q_ref[:, h * dqk:(h + 1) * dqk] * (dqk ** -0.5)
        kf = k_ref[:, h * dqk:(h + 1) * dqk]
        vf = v_ref[:, h * dv:(h + 1) * dv]
        c0 = c0_ref[h]
        n0 = n0_ref[h:h + 1, :]
        s = jnp.sum(qf * kf, axis=1, keepdims=True) * w_intra
        q8 = jnp.broadcast_to(qf, (8, dqk)).astype(BF16)
        cq = lax.dot_general(q8, c0.astype(BF16), NT_DIMS, preferred_element_type=F32)[0:1, :]
        num = w_inter * cq + s * vf
        nq = w_inter * jnp.sum(n0 * qf, axis=1, keepdims=True) + s
        hid = num / jnp.maximum(jnp.abs(nq), jnp.exp(-m_t))
        v_col = jnp.broadcast_to(vf, (128, dv)).T[:, 0:1]
        c_out[h] = w_inter * c0 + v_col * (w_intra * kf)
        n_out[h:h + 1, :] = w_inter * n0 + w_intra * kf
        m_acc = m_acc + jnp.where(lane_h == h, m_t, 0.0)
        hn = _rms_rows(hid, gain_ref[:, h * dv:(h + 1) * dv])
        zz = z_ref[:, h * dv:(h + 1) * dv]
        gate_o = _sigmoid(o_ref[:, h * dv:(h + 1) * dv])
        a_ref[:, h * dv:(h + 1) * dv] = (zz * _sigmoid(zz) * (gate_o * hn)).astype(a_ref.dtype)
    m_out[...] = m_acc


def _mlstm_sample(hs3, gs3, m0, out_gain, c0, n0, cols, *, dqk, dv):
    N = hs3.shape[0]
    W = ML_HEADS * dv
    WQ = ML_HEADS * dqk
    qo, ko, vo, oo, zo = (cols[n] for n in ("mq", "mk", "mv", "mo", "mz"))
    row = lambda w, off: pl.BlockSpec((None, 1, w), lambda b, off=off, w=w: (b, 0, off // w))
    kern = functools.partial(_mlstm_step_kernel, dqk=dqk, dv=dv)
    return pl.pallas_call(
        kern,
        grid=(N,),
        in_specs=[
            row(WQ, qo), row(WQ, ko), row(W, vo), row(W, oo), row(W, zo),
            pl.BlockSpec((None, 1, GATE_LANES), lambda b: (b, 0, 0)),
            pl.BlockSpec((None, 1, ML_HEADS), lambda b: (b, 0, 0)),
            pl.BlockSpec((1, W), lambda b: (0, 0)),
            pl.BlockSpec((None, ML_HEADS, dv, dqk), lambda b: (b, 0, 0, 0)),
            pl.BlockSpec((None, ML_HEADS, dqk), lambda b: (b, 0, 0)),
        ],
        out_specs=[
            pl.BlockSpec((None, 1, W), lambda b: (b, 0, 0)),
            pl.BlockSpec((None, ML_HEADS, dv, dqk), lambda b: (b, 0, 0, 0)),
            pl.BlockSpec((None, ML_HEADS, dqk), lambda b: (b, 0, 0)),
            pl.BlockSpec((None, 1, ML_HEADS), lambda b: (b, 0, 0)),
        ],
        out_shape=[
            jax.ShapeDtypeStruct((N, 1, W), BF16),
            jax.ShapeDtypeStruct((N, ML_HEADS, dv, dqk), F32),
            jax.ShapeDtypeStruct((N, ML_HEADS, dqk), F32),
            jax.ShapeDtypeStruct((N, 1, ML_HEADS), F32),
        ],
        compiler_params=_params("parallel"),
        name="mlstm_step",
    )(hs3, hs3, hs3, hs3, hs3, gs3, m0, out_gain.reshape(1, W), c0, n0)


def _fox_prep_kernel(q_ref, k_ref, qg_ref, kg_ref, qn_ref, kn_ref, *, dh):
    nh = q_ref.shape[1] // dh
    for h in range(nh):
        sl = slice(h * dh, (h + 1) * dh)
        qn_ref[:, sl] = _rms_rows(q_ref[:, sl], qg_ref[...]).astype(qn_ref.dtype)
        kn_ref[:, sl] = _rms_rows(k_ref[:, sl], kg_ref[...])


def _fox_prep(h3, q_gain, k_gain, cols, *, width, dh, tq):
    B, T, _ = h3.shape
    tq = min(tq, T)
    spec = lambda off: pl.BlockSpec((None, tq, width), lambda b, i, off=off: (b, i, off // width))
    out = pl.BlockSpec((None, tq, width), lambda b, i: (b, i, 0))
    gain = pl.BlockSpec((1, dh), lambda b, i: (0, 0))
    return pl.pallas_call(
        functools.partial(_fox_prep_kernel, dh=dh),
        grid=(B, T // tq),
        in_specs=[spec(cols["fq"]), spec(cols["fk"]), gain, gain],
        out_specs=[out, out],
        out_shape=[jax.ShapeDtypeStruct((B, T, width), BF16), jax.ShapeDtypeStruct((B, T, width), F32)],
        compiler_params=_params("parallel", "parallel"),
        name="fox_prep",
    )(h3, h3, q_gain.reshape(1, dh), k_gain.reshape(1, dh))


def _fox_cumsum_kernel(g_ref, fb_ref, ft_ref, carry_sc):
    L = g_ref.shape[0]

    @pl.when(pl.program_id(1) == 0)
    def _():
        carry_sc[...] = jnp.zeros_like(carry_sc)

    row = lax.broadcasted_iota(jnp.int32, (L, L), 0)
    col = lax.broadcasted_iota(jnp.int32, (L, L), 1)
    tri = jnp.where(col <= row, 1.0, 0.0).astype(BF16)
    cum = _dot_sel_left(tri, g_ref[...]) + carry_sc[...]
    carry_sc[...] = cum[L - 1:L, :]
    cum_t = cum.T
    off = 2 * ML_HEADS
    for h in range(FOX_HEADS):
        fb_ref[h] = jnp.broadcast_to(cum[:, off + h:off + h + 1], (L, GATE_LANES))
        ft_ref[h] = cum_t[off + h:off + h + 1, :]


def _fox_cumsum(g3):
    B, T, _ = g3.shape
    L = min(128, T)
    return pl.pallas_call(
        _fox_cumsum_kernel,
        grid=(B, T // L),
        in_specs=[pl.BlockSpec((None, L, GATE_LANES), lambda b, c: (b, c, 0))],
        out_specs=[
            pl.BlockSpec((None, FOX_HEADS, L, GATE_LANES), lambda b, c: (b, 0, c, 0)),
            pl.BlockSpec((None, FOX_HEADS, 1, L), lambda b, c: (b, 0, 0, c)),
        ],
        out_shape=[
            jax.ShapeDtypeStruct((B, FOX_HEADS, T, GATE_LANES), F32),
            jax.ShapeDtypeStruct((B, FOX_HEADS, 1, T), F32),
        ],
        scratch_shapes=[pltpu.VMEM((1, GATE_LANES), F32)],
        compiler_params=_params("parallel", "arbitrary"),
        name="fox_cumsum",
    )(g3)


def _fox_flash_kernel(q_ref, k_ref, v_ref, fq_ref, fk_ref, z_ref, o_ref, m_sc, l_sc, acc_sc, *, scale):
    qi = pl.program_id(2)
    ki = pl.program_id(3)
    tq = q_ref.shape[0]
    tk = k_ref.shape[0]

    @pl.when(ki == 0)
    def _():
        m_sc[...] = jnp.full_like(m_sc, -jnp.inf)
        l_sc[...] = jnp.zeros_like(l_sc)
        acc_sc[...] = jnp.zeros_like(acc_sc)

    @pl.when(ki <= qi)
    def _():
        s = lax.dot_general(q_ref[...], k_ref[...].astype(BF16), NT_DIMS, preferred_element_type=F32) * scale
        s = s + jnp.tile(fq_ref[...], (1, tk // GATE_LANES)) - fk_ref[...]
        row = lax.broadcasted_iota(jnp.int32, (tq, tk), 0) + qi * tq
        col = lax.broadcasted_iota(jnp.int32, (tq, tk), 1) + ki * tk
        s = jnp.where(col <= row, s, -jnp.inf)
        m_prev = m_sc[...]
        m_new = jnp.maximum(m_prev, jnp.max(s, axis=1, keepdims=True))
        alpha = jnp.exp(m_prev - m_new)
        p = jnp.exp(s - m_new)
        l_sc[...] = alpha * l_sc[...] + jnp.sum(p, axis=1, keepdims=True)
        acc_sc[...] = alpha * acc_sc[...] + jnp.dot(p.astype(BF16), v_ref[...].astype(BF16),
                                                    preferred_element_type=F32)
        m_sc[...] = m_new

    @pl.when(ki == qi)
    def _():
        zz = z_ref[...]
        o_ref[...] = (zz * _sigmoid(zz) * (acc_sc[...] / l_sc[...])).astype(o_ref.dtype)


def _fox_flash(qn, kn, h3, fb, ft, cols, *, dh, t):
    B, T, W = qn.shape
    t = min(t, T)
    nq = T // t
    vo, zo = cols["fv"] // dh, cols["fz"] // dh
    kv_idx = lambda b, h, i, j: (b, jnp.minimum(i, j), h)
    return pl.pallas_call(
        functools.partial(_fox_flash_kernel, scale=dh ** -0.5),
        grid=(B, FOX_HEADS, nq, nq),
        in_specs=[
            pl.BlockSpec((None, t, dh), lambda b, h, i, j: (b, i, h)),
            pl.BlockSpec((None, t, dh), kv_idx),
            pl.BlockSpec((None, t, dh), lambda b, h, i, j: (b, jnp.minimum(i, j), vo + h)),
            pl.BlockSpec((None, None, t, GATE_LANES), lambda b, h, i, j: (b, h, i, 0)),
            pl.BlockSpec((None, None, 1, t), lambda b, h, i, j: (b, h, 0, jnp.minimum(i, j))),
            pl.BlockSpec((None, t, dh), lambda b, h, i, j: (b, i, zo + h)),
        ],
        out_specs=pl.BlockSpec((None, t, dh), lambda b, h, i, j: (b, i, h)),
        out_shape=jax.ShapeDtypeStruct((B, T, W), BF16),
        scratch_shapes=[pltpu.VMEM((t, 1), F32), pltpu.VMEM((t, 1), F32), pltpu.VMEM((t, dh), F32)],
        compiler_params=_params("parallel", "parallel", "parallel", "arbitrary"),
        name="fox_flash",
    )(qn, kn, h3, fb, ft, h3)


def _lane_stride_roll(x, shift):
    return pltpu.roll(x, shift, axis=1)


def _logf_prep_kernel(lf_ref, suf_ref, tot_ref, *, nh):
    x = lf_ref[...]
    n = x.shape[1]
    lane = lax.broadcasted_iota(jnp.int32, x.shape, 1)
    suf = x
    tot = x
    step = nh
    while step < n:
        shifted = pltpu.roll(suf, n - step, axis=1)
        suf = suf + jnp.where(lane < n - step, shifted, 0.0)
        tot = tot + pltpu.roll(tot, n - step, axis=1)
        step *= 2
    suf_ref[...] = suf - x
    tot_ref[...] = tot


def _logf_prep(lf_flat, *, nh, rows):
    n_pages, n = lf_flat.shape
    rows = math.gcd(rows, n_pages)
    spec = pl.BlockSpec((rows, n), lambda i: (i, 0))
    return pl.pallas_call(
        functools.partial(_logf_prep_kernel, nh=nh),
        grid=(n_pages // rows,),
        in_specs=[spec],
        out_specs=[spec, spec],
        out_shape=[jax.ShapeDtypeStruct((n_pages, n), F32)] * 2,
        compiler_params=_params("parallel"),
        name="logf_prep",
    )(lf_flat)


def _fox_decode_kernel(pt_ref, q_ref, kn_ref, vn_ref, z_ref, gexp_ref, k_ref, v_ref, suf_ref, tot_ref,
                       o_ref, m_sc, l_sc, acc_sc, carry_sc, *, scale, nh):
    p = pl.program_id(1)
    q = q_ref[...]

    @pl.when(p == 0)
    def _():
        s_new = jnp.sum(q.astype(F32) * kn_ref[...], axis=1, keepdims=True) * scale
        m_sc[...] = s_new
        l_sc[...] = jnp.ones_like(l_sc)
        acc_sc[...] = vn_ref[...]
        carry_sc[...] = gexp_ref[...]

    kp = k_ref[...]
    vp = v_ref[...]
    rows = kp.shape[0] * kp.shape[1]
    k2 = kp.reshape(rows, kp.shape[2]).astype(BF16)
    v2 = vp.reshape(rows, vp.shape[2]).astype(BF16)
    s = lax.dot_general(q, k2, NT_DIMS, preferred_element_type=F32) * scale
    s = s + (suf_ref[...] + carry_sc[...])
    sub = lax.broadcasted_iota(jnp.int32, s.shape, 0)
    lane = lax.broadcasted_iota(jnp.int32, s.shape, 1)
    s = jnp.where((lane & (nh - 1)) == sub, s, -jnp.inf)
    m_prev = m_sc[...]
    m_new = jnp.maximum(m_prev, jnp.max(s, axis=1, keepdims=True))
    alpha = jnp.exp(m_prev - m_new)
    pr = jnp.exp(s - m_new)
    l_sc[...] = alpha * l_sc[...] + jnp.sum(pr, axis=1, keepdims=True)
    acc_sc[...] = alpha * acc_sc[...] + jnp.dot(pr.astype(BF16), v2, preferred_element_type=F32)
    m_sc[...] = m_new
    carry_sc[...] = carry_sc[...] + tot_ref[...]

    @pl.when(p == pl.num_programs(1) - 1)
    def _():
        zz = z_ref[...]
        o_ref[...] = zz * _sigmoid(zz) * (acc_sc[...] / l_sc[...])


def _fox_decode(page_table, q3, kn3, vn3, z3, gexp3, k_pool, v_pool, suf, tot, *, dh):
    N, nh, _ = q3.shape
    n_pages = page_table.shape[1]
    P = k_pool.shape[1]
    assert nh & (nh - 1) == 0
    per = lambda: pl.BlockSpec((None, nh, dh), lambda b, p, pt: (b, 0, 0))
    page = lambda b, p, pt: (pt[b, n_pages - 1 - p], 0, 0, 0)
    page3 = lambda b, p, pt: (pt[b, n_pages - 1 - p], 0, 0)
    grid_spec = pltpu.PrefetchScalarGridSpec(
        num_scalar_prefetch=1,
        grid=(N, n_pages),
        in_specs=[
            per(), per(), per(), per(),
            pl.BlockSpec((None, 1, P * nh), lambda b, p, pt: (b, 0, 0)),
            pl.BlockSpec((None, P, nh, dh), page),
            pl.BlockSpec((None, P, nh, dh), page),
            pl.BlockSpec((None, 1, P * nh), page3),
            pl.BlockSpec((None, 1, P * nh), page3),
        ],
        out_specs=pl.BlockSpec((None, nh, dh), lambda b, p, pt: (b, 0, 0)),
        scratch_shapes=[pltpu.VMEM((nh, 1), F32), pltpu.VMEM((nh, 1), F32), pltpu.VMEM((nh, dh), F32),
                        pltpu.VMEM((1, P * nh), F32)],
    )
    return pl.pallas_call(
        functools.partial(_fox_decode_kernel, scale=dh ** -0.5, nh=nh),
        grid_spec=grid_spec,
        out_shape=jax.ShapeDtypeStruct((N, nh, dh), F32),
        compiler_params=_params("parallel", "arbitrary"),
        name="fox_decode",
    )(page_table, q3, kn3, vn3, z3, gexp3, k_pool, v_pool, suf, tot)


def _mem_attn_kernel(q_ref, z_ref, k_ref, v_ref, qg_ref, a_ref, *, dh, scale):
    nh = q_ref.shape[1] // dh
    for h in range(nh):
        sl = slice(h * dh, (h + 1) * dh)
        q = _rms_rows(q_ref[:, sl], qg_ref[...]).astype(BF16)
        s = lax.dot_general(q, k_ref[:, sl].astype(BF16), NT_DIMS, preferred_element_type=F32) * scale
        e = jnp.exp(s - jnp.max(s, axis=1, keepdims=True))
        o = jnp.dot(e.astype(BF16), v_ref[:, sl].astype(BF16), preferred_element_type=F32)
        o = o / jnp.sum(e, axis=1, keepdims=True)
        zz = z_ref[:, sl]
        a_ref[:, sl] = (zz * _sigmoid(zz) * o).astype(a_ref.dtype)


def _mem_attn_prompt(h3, mk, mv, q_gain, cols, *, width, dh, tq):
    B, T, _ = h3.shape
    n_mem = mk.shape[1]
    tq = min(tq, T)
    spec = lambda off: pl.BlockSpec((None, tq, width), lambda b, i, off=off: (b, i, off // width))
    mem = pl.BlockSpec((None, n_mem, width), lambda b, i: (b, 0, 0))
    return pl.pallas_call(
        functools.partial(_mem_attn_kernel, dh=dh, scale=dh ** -0.5),
        grid=(B, T // tq),
        in_specs=[spec(cols["cq"]), spec(cols["cz"]), mem, mem, pl.BlockSpec((1, dh), lambda b, i: (0, 0))],
        out_specs=pl.BlockSpec((None, tq, width), lambda b, i: (b, i, 0)),
        out_shape=jax.ShapeDtypeStruct((B, T, width), BF16),
        compiler_params=_params("parallel", "parallel"),
        name="mem_attn",
    )(h3, h3, mk, mv, q_gain.reshape(1, dh))


def _mem_decode_kernel(q_ref, z_ref, k_ref, v_ref, qg_ref, o_ref, *, scale, nh):
    q = _rms_rows(q_ref[...], qg_ref[...]).astype(BF16)
    kp = k_ref[...]
    vp = v_ref[...]
    rows = kp.shape[0] * kp.shape[1]
    k2 = kp.reshape(rows, kp.shape[2]).astype(BF16)
    v2 = vp.reshape(rows, vp.shape[2]).astype(BF16)
    s = lax.dot_general(q, k2, NT_DIMS, preferred_element_type=F32) * scale
    sub = lax.broadcasted_iota(jnp.int32, s.shape, 0)
    lane = lax.broadcasted_iota(jnp.int32, s.shape, 1)
    s = jnp.where((lane & (nh - 1)) == sub, s, -jnp.inf)
    m = jnp.maximum(jnp.max(s, axis=1, keepdims=True), -1e30)
    e = jnp.exp(s - m)
    den = jnp.maximum(jnp.sum(e, axis=1, keepdims=True), 1e-30)
    o = jnp.dot(e.astype(BF16), v2, preferred_element_type=F32) / den
    zz = z_ref[...]
    o_ref[...] = zz * _sigmoid(zz) * o


def _mem_decode(q3, z3, k_cache, v_cache, q_gain, *, dh):
    N, n_mem, nh, _ = k_cache.shape
    assert nh & (nh - 1) == 0
    per = pl.BlockSpec((None, 8, dh), lambda b: (b, 0, 0))
    cache = pl.BlockSpec((None, n_mem, nh, dh), lambda b: (b, 0, 0, 0))
    return pl.pallas_call(
        functools.partial(_mem_decode_kernel, scale=dh ** -0.5, nh=nh),
        grid=(N,),
        in_specs=[per, per, cache, cache, pl.BlockSpec((1, dh), lambda b: (0, 0))],
        out_specs=per,
        out_shape=jax.ShapeDtypeStruct((N, 8, dh), F32),
        compiler_params=_params("parallel"),
        name="mem_decode",
    )(q3, z3, k_cache, v_cache, q_gain.reshape(1, dh))


def _merge_kernel(am_ref, af_ref, ac_ref, wm_ref, wf_ref, wc_ref, gm_ref, gf_ref, gc_ref, u_ref):
    d = lambda a, w: jnp.dot(a[...], w[...], preferred_element_type=F32)
    u = _sigmoid(gm_ref[...]) * d(am_ref, wm_ref)
    u = u + _sigmoid(gf_ref[...]) * d(af_ref, wf_ref)
    u = u + _sigmoid(gc_ref[...]) * d(ac_ref, wc_ref)
    u_ref[...] = u.astype(u_ref.dtype)


def _merge(am, af, ac, wm, wf, wc, h2, g_off, *, tm, tn):
    m, k = am.shape
    n = wm.shape[1]
    tm = min(tm, m)
    a_spec = pl.BlockSpec((tm, k), lambda i, j: (i, 0))
    w_spec = pl.BlockSpec((k, tn), lambda i, j: (0, j))
    g_spec = lambda b: pl.BlockSpec((tm, tn), lambda i, j, b=b: (i, (g_off + b * n) // tn + j))
    return pl.pallas_call(
        _merge_kernel,
        grid=(m // tm, n // tn),
        in_specs=[a_spec, a_spec, a_spec, w_spec, w_spec, w_spec, g_spec(0), g_spec(1), g_spec(2)],
        out_specs=pl.BlockSpec((tm, tn), lambda i, j: (i, j)),
        out_shape=jax.ShapeDtypeStruct((m, n), BF16),
        compiler_params=_params("parallel", "arbitrary"),
        name="merge",
    )(am, af, ac, wm, wf, wc, h2, h2, h2)


def _out_proj_kernel(u_ref, w_ref, x_ref, y_ref):
    y_ref[...] = x_ref[...] + jnp.dot(u_ref[...], w_ref[...], preferred_element_type=F32)


def _out_proj(u, w, x, *, tm, tn):
    m, k = u.shape
    n = w.shape[1]
    tm = min(tm, m)
    return pl.pallas_call(
        _out_proj_kernel,
        grid=(m // tm, n // tn),
        in_specs=[
            pl.BlockSpec((tm, k), lambda i, j: (i, 0)),
            pl.BlockSpec((k, tn), lambda i, j: (0, j)),
            pl.BlockSpec((tm, tn), lambda i, j: (i, j)),
        ],
        out_specs=pl.BlockSpec((tm, tn), lambda i, j: (i, j)),
        out_shape=jax.ShapeDtypeStruct((m, n), F32),
        compiler_params=_params("parallel", "arbitrary"),
        name="out_proj",
    )(u, w, x)


def _column_layout(d_model):
    ml_dv = d_model // ML_HEADS
    ml_qkw = ML_HEADS * (ml_dv // 2)
    names = ("mq", "mk", "mv", "mi", "mf", "mo", "mz", "fq", "fk", "fv", "ff", "fz", "cq", "cz", "g")
    widths = (ml_qkw, ml_qkw, d_model, ML_HEADS, ML_HEADS, d_model, d_model,
              d_model, d_model, d_model, FOX_HEADS, d_model, d_model, d_model, N_BRANCH * d_model)
    src, off = {}, 0
    for n, w in zip(names, widths):
        src[n] = (off, w)
        off += w
    big, boff = {}, 0
    for n in names:
        if n in ("mi", "mf", "ff"):
            continue
        big[n] = boff
        boff += src[n][1]
    return src, big, boff


def _layer(x_p, x_s, mem_p, fox_k_pool, fox_v_pool, fox_lf_pool, page_table, mem_k_s, mem_v_s,
           c0_s, n0_s, m0_s, norm_in, norm_mem, w_in, w_mem_kv, b_i, b_f, b_ff, ml_out_norm,
           fq_norm, fk_norm, cq_norm, ck_norm, w_br_m, w_br_f, w_br_c, w_out):
    B, T, D = x_p.shape
    N = x_s.shape[0]
    dv = D // ML_HEADS
    dqk = dv // 2
    fox_dh = D // FOX_HEADS
    mem_dh = D // MEM_HEADS
    src, cols, nbig = _column_layout(D)
    col = lambda n: w_in[:, src[n][0]:src[n][0] + src[n][1]]

    w_big = jnp.concatenate([col(n) for n in cols], axis=1).astype(BF16)
    n_gate = 2 * ML_HEADS + FOX_HEADS
    w_gate = jnp.concatenate([col("mi"), col("mf"), col("ff"),
                              jnp.zeros((D, GATE_LANES - n_gate), F32)], axis=1)
    b_gate = jnp.concatenate([b_i, b_f, b_ff, jnp.zeros((GATE_LANES - n_gate,), F32)]).reshape(1, GATE_LANES)
    w_mem = w_mem_kv.astype(BF16)
    wm, wf, wc, wo = (w.astype(BF16) for w in (w_br_m, w_br_f, w_br_c, w_out))

    xp2 = x_p.reshape(B * T, D)
    xs2 = x_s.reshape(N, D)

    h2 = _norm_proj(xp2, norm_in, w_big, tm=1024, tn=1024)
    g2 = _gates(xp2, norm_in, w_gate, b_gate, tm=1024)
    h3 = h2.reshape(B, T, nbig)
    g3 = g2.reshape(B, T, GATE_LANES)

    n_mem = mem_p.shape[1]
    mem2 = mem_p.reshape(B * n_mem, D)
    mk_p = _norm_proj(mem2, norm_mem, w_mem[:, :D], tm=512, tn=mem_dh, head_gain=ck_norm)
    mv_p = _norm_proj(mem2, norm_mem, w_mem[:, D:], tm=512, tn=mem_dh)

    a_m, c_p, n_p, m_p = _mlstm_prompt(h3, g3, ml_out_norm, cols, dqk=dqk, dv=dv)

    qn, kn = _fox_prep(h3, fq_norm, fk_norm, cols, width=D, dh=fox_dh, tq=512)
    fb, ft = _fox_cumsum(g3)
    a_f = _fox_flash(qn, kn, h3, fb, ft, cols, dh=fox_dh, t=1024)

    a_c = _mem_attn_prompt(h3, mk_p.reshape(B, n_mem, D), mv_p.reshape(B, n_mem, D), cq_norm, cols,
                           width=D, dh=mem_dh, tq=512)

    u_p = _merge(a_m.reshape(B * T, D), a_f.reshape(B * T, D), a_c.reshape(B * T, D),
                 wm, wf, wc, h2, cols["g"], tm=512, tn=512)
    y_p = _out_proj(u_p, wo, xp2, tm=1024, tn=1024).reshape(B, T, D)

    hs2 = _norm_proj(xs2, norm_in, w_big, tm=128, tn=1024)
    gs2 = _gates(xs2, norm_in, w_gate, b_gate, tm=128)
    hs3 = hs2.reshape(N, 1, nbig)
    gs3 = gs2.reshape(N, 1, GATE_LANES)

    a_ms, c_s, n_s, m_s = _mlstm_sample(hs3, gs3, m0_s.reshape(N, 1, ML_HEADS), ml_out_norm, c0_s, n0_s,
                                        cols, dqk=dqk, dv=dv)

    hs1 = hs2.reshape(1, N, nbig)
    qn_s, kn_s = _fox_prep(hs1, fq_norm, fk_norm, cols, width=D, dh=fox_dh, tq=128)
    seg = lambda n: hs2[:, cols[n]:cols[n] + D]
    fv_s = seg("fv")
    P = fox_k_pool.shape[1]
    n_phys = fox_k_pool.shape[0]
    lf_flat = fox_lf_pool.reshape(n_phys, P * FOX_HEADS)
    suf, tot = _logf_prep(lf_flat, nh=FOX_HEADS, rows=8)
    flf_s = gs2[:, 2 * ML_HEADS:n_gate]
    gexp = jnp.tile(flf_s, (1, P)).reshape(N, 1, P * FOX_HEADS)
    to_heads = lambda a: a.reshape(N, FOX_HEADS, fox_dh)
    o_f = _fox_decode(page_table, to_heads(qn_s.reshape(N, D)), to_heads(kn_s.reshape(N, D)), to_heads(fv_s),
                      to_heads(seg("fz")), gexp, fox_k_pool, fox_v_pool,
                      suf.reshape(n_phys, 1, P * FOX_HEADS), tot.reshape(n_phys, 1, P * FOX_HEADS), dh=fox_dh)
    a_fs = o_f.reshape(N, D).astype(BF16)

    pad8 = lambda a: jnp.pad(a.reshape(N, MEM_HEADS, mem_dh), ((0, 0), (0, 8 - MEM_HEADS), (0, 0)))
    o_c = _mem_decode(pad8(seg("cq")), pad8(seg("cz")), mem_k_s, mem_v_s, cq_norm, dh=mem_dh)
    a_cs = o_c[:, :MEM_HEADS].reshape(N, D).astype(BF16)

    u_s = _merge(a_ms.reshape(N, D), a_fs, a_cs, wm, wf, wc, hs2, cols["g"], tm=128, tn=512)
    y_s = _out_proj(u_s, wo, xs2, tm=128, tn=1024).reshape(N, 1, D)

    fv_p = h3[:, :, cols["fv"]:cols["fv"] + D]
    outs_p = (kn.reshape(B, T, FOX_HEADS, fox_dh), fv_p.reshape(B, T, FOX_HEADS, fox_dh),
              g3[:, :, 2 * ML_HEADS:n_gate],
              mk_p.reshape(B, n_mem, MEM_HEADS, mem_dh), mv_p.reshape(B, n_mem, MEM_HEADS, mem_dh),
              c_p, n_p.reshape(B, ML_HEADS, dqk), m_p.reshape(B, ML_HEADS))
    outs_s = (kn_s.reshape(N, 1, FOX_HEADS, fox_dh), fv_s.reshape(N, 1, FOX_HEADS, fox_dh),
              flf_s.reshape(N, 1, FOX_HEADS), c_s, n_s, m_s.reshape(N, ML_HEADS))
    return y_p, y_s, outs_p, outs_s


def kernel(x_prompt, x_sample, mem_prompt, cache_fox_k, cache_fox_v, cache_fox_logf, page_table, cache_mem_k, cache_mem_v, state_mlstm_C, state_mlstm_n, state_mlstm_m, norm_in, norm_mem, w_in, w_mem_kv, b_mlstm_i, b_mlstm_f, b_fox_f, mlstm_out_norm, fox_q_norm, fox_k_norm, mem_q_norm, mem_k_norm, w_br_mlstm, w_br_fox, w_br_mem, w_out):
    depth = w_in.shape[0]
    y_p, y_s = x_prompt, x_sample
    per_layer = []
    for l in range(depth):
        y_p, y_s, outs_p, outs_s = _layer(
            y_p, y_s, mem_prompt, cache_fox_k[l], cache_fox_v[l], cache_fox_logf[l], page_table,
            cache_mem_k[l], cache_mem_v[l], state_mlstm_C[l], state_mlstm_n[l], state_mlstm_m[l],
            norm_in[l], norm_mem[l], w_in[l], w_mem_kv[l], b_mlstm_i[l], b_mlstm_f[l], b_fox_f[l],
            mlstm_out_norm[l], fox_q_norm[l], fox_k_norm[l], mem_q_norm[l], mem_k_norm[l],
            w_br_mlstm[l], w_br_fox[l], w_br_mem[l], w_out[l])
        per_layer.append(outs_p + outs_s)
    stacked = tuple(jnp.stack([lay[i] for lay in per_layer]) for i in range(len(per_layer[0])))
    return (y_p, y_s) + stacked
```

```python
import functools
import math

import jax
import jax.numpy as jnp
from jax import lax
from jax.experimental import pallas as pl
from jax.experimental.pallas import tpu as pltpu

F32 = jnp.float32
BF16 = jnp.bfloat16

ML_HEADS = 4
FOX_HEADS = 16
MEM_HEADS = 4
N_BRANCH = 3
ML_CHUNK = 128
EPS = 1e-6
LOG2E = 1.4426950408889634
GATE_LANES = 128
VMEM_LIMIT = 56 * 1024 * 1024

NT_DIMS = (((1,), (1,)), ((), ()))
TN_DIMS = (((0,), (0,)), ((), ()))


def _params(*sem):
    return pltpu.CompilerParams(dimension_semantics=sem, vmem_limit_bytes=VMEM_LIMIT)


def _sigmoid(z):
    return 1.0 / (1.0 + jnp.exp(-z))


def _log_sigmoid(z):
    return jnp.minimum(z, 0.0) - jnp.log1p(jnp.exp(-jnp.abs(z)))


def _split3(x):
    x1 = x.astype(BF16)
    r1 = x - x1.astype(F32)
    x2 = r1.astype(BF16)
    r2 = r1 - x2.astype(F32)
    return x1, x2, r2.astype(BF16)


def _dot_sel_left(sel, x):
    x1, x2, x3 = _split3(x)
    d = lambda p: jnp.dot(sel, p, preferred_element_type=F32)
    return d(x1) + d(x2) + d(x3)


def _rms_rows(x, gain):
    return x * lax.rsqrt(jnp.mean(x * x, axis=-1, keepdims=True) + EPS) * gain


def _norm_proj_kernel(x_ref, g_ref, w_ref, *rest, head_norm):
    if head_norm:
        hg_ref, o_ref, xn_ref = rest
    else:
        o_ref, xn_ref = rest

    @pl.when(pl.program_id(1) == 0)
    def _():
        xn_ref[...] = _rms_rows(x_ref[...], g_ref[...]).astype(BF16)

    acc = jnp.dot(xn_ref[...], w_ref[...], preferred_element_type=F32)
    if head_norm:
        acc = _rms_rows(acc, hg_ref[...])
    o_ref[...] = acc.astype(o_ref.dtype)


def _norm_proj(x, gain, w, *, tm, tn, head_gain=None, out_dtype=F32):
    m, k = x.shape
    n = w.shape[1]
    tm = min(tm, m)
    assert m % tm == 0 and n % tn == 0
    in_specs = [
        pl.BlockSpec((tm, k), lambda i, j: (i, 0)),
        pl.BlockSpec((1, k), lambda i, j: (0, 0)),
        pl.BlockSpec((k, tn), lambda i, j: (0, j)),
    ]
    args = [x, gain.reshape(1, k), w]
    if head_gain is not None:
        assert head_gain.shape == (tn,)
        in_specs.append(pl.BlockSpec((1, tn), lambda i, j: (0, 0)))
        args.append(head_gain.reshape(1, tn))
    return pl.pallas_call(
        functools.partial(_norm_proj_kernel, head_norm=head_gain is not None),
        grid=(m // tm, n // tn),
        in_specs=in_specs,
        out_specs=pl.BlockSpec((tm, tn), lambda i, j: (i, j)),
        out_shape=jax.ShapeDtypeStruct((m, n), out_dtype),
        scratch_shapes=[pltpu.VMEM((tm, k), BF16)],
        compiler_params=_params("parallel", "arbitrary"),
        name="norm_proj",
    )(*args)


def _gates_kernel(x_ref, g_ref, w_ref, b_ref, o_ref):
    xn = _rms_rows(x_ref[...], g_ref[...])
    x1, x2, _ = _split3(xn)
    w1, w2, _ = _split3(w_ref[...])
    d = lambda a, b: jnp.dot(a, b, preferred_element_type=F32)
    z = d(x1, w1) + d(x1, w2) + d(x2, w1) + b_ref[...]
    lane = lax.broadcasted_iota(jnp.int32, z.shape, 1)
    o_ref[...] = jnp.where(lane < ML_HEADS, z, _log_sigmoid(z))


def _gates(x, gain, w_gate, bias, *, tm):
    m, k = x.shape
    tm = min(tm, m)
    return pl.pallas_call(
        _gates_kernel,
        grid=(m // tm,),
        in_specs=[
            pl.BlockSpec((tm, k), lambda i: (i, 0)),
            pl.BlockSpec((1, k), lambda i: (0, 0)),
            pl.BlockSpec((k, GATE_LANES), lambda i: (0, 0)),
            pl.BlockSpec((1, GATE_LANES), lambda i: (0, 0)),
        ],
        out_specs=pl.BlockSpec((tm, GATE_LANES), lambda i: (i, 0)),
        out_shape=jax.ShapeDtypeStruct((m, GATE_LANES), F32),
        compiler_params=_params("parallel"),
        name="gates",
    )(x, gain.reshape(1, k), w_gate, bias)


def _mlstm_chunk_kernel(q_ref, k_ref, v_ref, o_ref, z_ref, g_ref, gain_ref,
                        a_ref, c_out, n_out, m_out, c_sc, n_sc, m_sc, *, dqk):
    h = pl.program_id(1)
    c_idx = pl.program_id(2)
    L = q_ref.shape[0]

    @pl.when(c_idx == 0)
    def _():
        c_sc[...] = jnp.zeros_like(c_sc)
        n_sc[...] = jnp.zeros_like(n_sc)
        m_sc[...] = jnp.zeros_like(m_sc)

    row = lax.broadcasted_iota(jnp.int32, (L, L), 0)
    col = lax.broadcasted_iota(jnp.int32, (L, L), 1)
    causal = col <= row
    tri = jnp.where(causal, 1.0, 0.0).astype(BF16)

    g = g_ref[...]
    cum = _dot_sel_left(tri, g)
    g_t = g.T
    cum_t = cum.T
    lane = lax.broadcasted_iota(jnp.int32, (L, GATE_LANES), 1)
    sub = lax.broadcasted_iota(jnp.int32, (GATE_LANES, L), 0)
    pick_col = lambda x, idx: jnp.sum(jnp.where(lane == idx, x, 0.0), axis=1, keepdims=True)
    pick_row = lambda x, idx: jnp.sum(jnp.where(sub == idx, x, 0.0), axis=0, keepdims=True)
    ig_col = pick_col(g, h)
    b_col = pick_col(cum, ML_HEADS + h)
    ig_row = pick_row(g_t, h)
    b_row = pick_row(cum_t, ML_HEADS + h)

    m_prev = m_sc[...]
    a_col = b_col + m_prev
    dmat = jnp.where(causal, b_col - b_row + ig_row, -jnp.inf)
    m_t = jnp.maximum(a_col, jnp.max(dmat, axis=1, keepdims=True))
    w_inter = jnp.exp(a_col - m_t)
    w_intra = jnp.exp(dmat - m_t)

    qf = q_ref[...] * (dqk ** -0.5)
    kf = k_ref[...]
    q = qf.astype(BF16)
    k = kf.astype(BF16)
    v = v_ref[...].astype(BF16)
    c_prev = c_sc[...]
    n_prev = n_sc[...]

    s = lax.dot_general(q, k, NT_DIMS, preferred_element_type=F32) * w_intra
    inter = lax.dot_general(q, c_prev.astype(BF16), NT_DIMS, preferred_element_type=F32)
    num = w_inter * inter + jnp.dot(s.astype(BF16), v, preferred_element_type=F32)
    nq = w_inter * jnp.sum(qf * n_prev, axis=1, keepdims=True) + jnp.sum(s, axis=1, keepdims=True)
    hid = num / jnp.maximum(jnp.abs(nq), jnp.exp(-m_t))

    b_last = b_col[L - 1:L, :]
    m_new = m_t[L - 1:L, :]
    w_c = jnp.exp(b_last + m_prev - m_new)
    w_s = jnp.exp(b_last - b_col + ig_col - m_new)
    wv = (w_s * v_ref[...]).astype(BF16)
    c_new = w_c * c_prev + lax.dot_general(wv, k, TN_DIMS, preferred_element_type=F32)
    n_new = w_c * n_prev + jnp.sum(w_s * kf, axis=0, keepdims=True)
    c_sc[...] = c_new
    n_sc[...] = n_new
    m_sc[...] = m_new

    hn = _rms_rows(hid, gain_ref[...])
    zz = z_ref[...]
    a_ref[...] = (zz * _sigmoid(zz) * (_sigmoid(o_ref[...]) * hn)).astype(a_ref.dtype)

    @pl.when(c_idx == pl.num_programs(2) - 1)
    def _():
        c_out[...] = c_new
        n_out[...] = n_new
        m_out[...] = m_new


def _mlstm_prompt(h3, g3, out_gain, cols, *, dqk, dv):
    B, T, _ = h3.shape
    L = math.gcd(T, ML_CHUNK)
    nc = T // L
    qo, ko, vo, oo, zo = (cols[n] for n in ("mq", "mk", "mv", "mo", "mz"))
    blk = lambda w, off: pl.BlockSpec((None, L, w), lambda b, h, c, off=off, w=w: (b, c, off // w + h))
    kern = functools.partial(_mlstm_chunk_kernel, dqk=dqk)
    return pl.pallas_call(
        kern,
        grid=(B, ML_HEADS, nc),
        in_specs=[
            blk(dqk, qo), blk(dqk, ko), blk(dv, vo), blk(dv, oo), blk(dv, zo),
            pl.BlockSpec((None, L, GATE_LANES), lambda b, h, c: (b, c, 0)),
            pl.BlockSpec((1, dv), lambda b, h, c: (0, h)),
        ],
        out_specs=[
            pl.BlockSpec((None, L, dv), lambda b, h, c: (b, c, h)),
            pl.BlockSpec((None, None, dv, dqk), lambda b, h, c: (b, h, 0, 0)),
            pl.BlockSpec((None, None, 1, dqk), lambda b, h, c: (b, h, 0, 0)),
            pl.BlockSpec((None, None, 1, 1), lambda b, h, c: (b, h, 0, 0)),
        ],
        out_shape=[
            jax.ShapeDtypeStruct((B, T, ML_HEADS * dv), BF16),
            jax.ShapeDtypeStruct((B, ML_HEADS, dv, dqk), F32),
            jax.ShapeDtypeStruct((B, ML_HEADS, 1, dqk), F32),
            jax.ShapeDtypeStruct((B, ML_HEADS, 1, 1), F32),
        ],
        scratch_shapes=[pltpu.VMEM((dv, dqk), F32), pltpu.VMEM((1, dqk), F32), pltpu.VMEM((1, 1), F32)],
        compiler_params=_params("parallel", "parallel", "arbitrary"),
        name="mlstm_chunk",
    )(h3, h3, h3, h3, h3, g3, out_gain.reshape(1, ML_HEADS * dv))


def _mlstm_step_kernel(q_ref, k_ref, v_ref, o_ref, z_ref, g_ref, m0_ref, gain_ref, c0_ref, n0_ref,
                       a_ref, c_out, n_out, m_out, *, dqk, dv):
    g = g_ref[...]
    m0 = m0_ref[...]
    lane_h = lax.broadcasted_iota(jnp.int32, (1, ML_HEADS), 1)
    m_acc = jnp.zeros((1, ML_HEADS), F32)
    for h in range(ML_HEADS):
        ig = g[:, h:h + 1]
        lf = g[:, ML_HEADS + h:ML_HEADS + h + 1]
        a = lf + m0[:, h:h + 1]
        m_t = jnp.maximum(a, ig)
        w_inter = jnp.exp(a - m_t)
        w_intra = jnp.exp(ig - m_t)
        qf = q_ref[:, h * dqk:(h + 1) * dqk] * (dqk ** -0.5)
        kf = k_ref[:, h * dqk:(h + 1) * dqk]
        vf = v_ref[:, h * dv:(h + 1) * dv]
        c0 = c0_ref[h]
        n0 = n0_ref[h:h + 1, :]
        s = jnp.sum(qf * kf, axis=1, keepdims=True) * w_intra
        q8 = jnp.broadcast_to(qf, (8, dqk)).astype(BF16)
        cq = lax.dot_general(q8, c0.astype(BF16), NT_DIMS, preferred_element_type=F32)[0:1, :]
        num = w_inter * cq + s * vf
        nq = w_inter * jnp.sum(n0 * qf, axis=1, keepdims=True) + s
        hid = num / jnp.maximum(jnp.abs(nq), jnp.exp(-m_t))
        first = lax.broadcasted_iota(jnp.int32, (16, 1), 0) == 0
        v16 = jnp.where(first, vf, 0.0).astype(BF16)
        k16 = jnp.where(first, w_intra * kf, 0.0).astype(BF16)
        c_out[h] = w_inter * c0 + lax.dot_general(v16, k16, TN_DIMS, preferred_element_type=F32)
        n_out[h:h + 1, :] = w_inter * n0 + w_intra * kf
        m_acc = m_acc + jnp.where(lane_h == h, m_t, 0.0)
        hn = _rms_rows(hid, gain_ref[:, h * dv:(h + 1) * dv])
        zz = z_ref[:, h * dv:(h + 1) * dv]
        gate_o = _sigmoid(o_ref[:, h * dv:(h + 1) * dv])
        a_ref[:, h * dv:(h + 1) * dv] = (zz * _sigmoid(zz) * (gate_o * hn)).astype(a_ref.dtype)
    m_out[...] = m_acc


def _mlstm_sample(hs3, gs3, m0, out_gain, c0, n0, cols, *, dqk, dv):
    N = hs3.shape[0]
    W = ML_HEADS * dv
    WQ = ML_HEADS * dqk
    qo, ko, vo, oo, zo = (cols[n] for n in ("mq", "mk", "mv", "mo", "mz"))
    row = lambda w, off: pl.BlockSpec((None, 1, w), lambda b, off=off, w=w: (b, 0, off // w))
    kern = functools.partial(_mlstm_step_kernel, dqk=dqk, dv=dv)
    return pl.pallas_call(
        kern,
        grid=(N,),
        in_specs=[
            row(WQ, qo), row(WQ, ko), row(W, vo), row(W, oo), row(W, zo),
            pl.BlockSpec((None, 1, GATE_LANES), lambda b: (b, 0, 0)),
            pl.BlockSpec((None, 1, ML_HEADS), lambda b: (b, 0, 0)),
            pl.BlockSpec((1, W), lambda b: (0, 0)),
            pl.BlockSpec((None, ML_HEADS, dv, dqk), lambda b: (b, 0, 0, 0)),
            pl.BlockSpec((None, ML_HEADS, dqk), lambda b: (b, 0, 0)),
        ],
        out_specs=[
            pl.BlockSpec((None, 1, W), lambda b: (b, 0, 0)),
            pl.BlockSpec((None, ML_HEADS, dv, dqk), lambda b: (b, 0, 0, 0)),
            pl.BlockSpec((None, ML_HEADS, dqk), lambda b: (b, 0, 0)),
            pl.BlockSpec((None, 1, ML_HEADS), lambda b: (b, 0, 0)),
        ],
        out_shape=[
            jax.ShapeDtypeStruct((N, 1, W), BF16),
            jax.ShapeDtypeStruct((N, ML_HEADS, dv, dqk), F32),
            jax.ShapeDtypeStruct((N, ML_HEADS, dqk), F32),
            jax.ShapeDtypeStruct((N, 1, ML_HEADS), F32),
        ],
        compiler_params=_params("parallel"),
        name="mlstm_step",
    )(hs3, hs3, hs3, hs3, hs3, gs3, m0, out_gain.reshape(1, W), c0, n0)


N_AUG = 128


def _qk_norm_kernel(q_ref, k_ref, qg_ref, kg_ref, qn_ref, kn_ref, *, dh):
    nh = q_ref.shape[1] // dh
    for h in range(nh):
        sl = slice(h * dh, (h + 1) * dh)
        qn_ref[:, sl] = _rms_rows(q_ref[:, sl], qg_ref[...]).astype(qn_ref.dtype)
        kn_ref[:, sl] = _rms_rows(k_ref[:, sl], kg_ref[...])


def _qk_norm(h3, q_gain, k_gain, cols, *, width, dh, tq):
    B, T, _ = h3.shape
    tq = min(tq, T)
    spec = lambda off: pl.BlockSpec((None, tq, width), lambda b, i, off=off: (b, i, off // width))
    out = pl.BlockSpec((None, tq, width), lambda b, i: (b, i, 0))
    gain = pl.BlockSpec((1, dh), lambda b, i: (0, 0))
    return pl.pallas_call(
        functools.partial(_qk_norm_kernel, dh=dh),
        grid=(B, T // tq),
        in_specs=[spec(cols["fq"]), spec(cols["fk"]), gain, gain],
        out_specs=[out, out],
        out_shape=[jax.ShapeDtypeStruct((B, T, width), BF16), jax.ShapeDtypeStruct((B, T, width), F32)],
        compiler_params=_params("parallel", "parallel"),
        name="qk_norm",
    )(h3, h3, q_gain.reshape(1, dh), k_gain.reshape(1, dh))


def _fox_prep_kernel(q_ref, k_ref, v_ref, g_ref, qg_ref, kg_ref,
                     qa_ref, ka_ref, vb_ref, kn_ref, vf_ref, carry_sc, *, dh, qscale):
    tp = q_ref.shape[0]
    nh = q_ref.shape[1] // dh

    @pl.when(pl.program_id(1) == 0)
    def _():
        carry_sc[...] = jnp.zeros_like(carry_sc)

    row = lax.broadcasted_iota(jnp.int32, (tp, tp), 0)
    col = lax.broadcasted_iota(jnp.int32, (tp, tp), 1)
    tri = jnp.where(col <= row, 1.0, 0.0).astype(BF16)
    cum = _dot_sel_left(tri, g_ref[...]) + carry_sc[...]
    carry_sc[...] = cum[tp - 1:tp, :]

    v = v_ref[...]
    vf_ref[...] = v
    vb_ref[...] = v.astype(BF16)
    lane = lax.broadcasted_iota(jnp.int32, (tp, N_AUG), 1)
    off = 2 * ML_HEADS
    for h in range(nh):
        sl = slice(h * dh, (h + 1) * dh)
        kn = _rms_rows(k_ref[:, sl], kg_ref[...])
        kn_ref[:, sl] = kn
        qn = _rms_rows(q_ref[:, sl], qg_ref[...]) * qscale
        f = cum[:, off + h:off + h + 1] * LOG2E
        f1 = f.astype(BF16).astype(F32)
        r1 = f - f1
        f2 = r1.astype(BF16).astype(F32)
        f3 = r1 - f2
        eq = jnp.where(lane < 3, 1.0, jnp.where(lane == 3, f1, jnp.where(lane == 4, f2,
                       jnp.where(lane == 5, f3, 0.0))))
        ek = jnp.where(lane == 0, -f1, jnp.where(lane == 1, -f2, jnp.where(lane == 2, -f3,
                       jnp.where(lane < 6, 1.0, 0.0))))
        base = h * (dh + N_AUG)
        qa_ref[:, base:base + dh] = qn.astype(BF16)
        qa_ref[:, base + dh:base + dh + N_AUG] = eq.astype(BF16)
        ka_ref[:, base:base + dh] = kn.astype(BF16)
        ka_ref[:, base + dh:base + dh + N_AUG] = ek.astype(BF16)


def _fox_prep(h3, g3, q_gain, k_gain, cols, *, width, dh, tp):
    B, T, _ = h3.shape
    tp = min(tp, T)
    nh = width // dh
    wa = nh * (dh + N_AUG)
    spec = lambda off: pl.BlockSpec((None, tp, width), lambda b, i, off=off: (b, i, off // width))
    out = lambda w: pl.BlockSpec((None, tp, w), lambda b, i: (b, i, 0))
    gain = pl.BlockSpec((1, dh), lambda b, i: (0, 0))
    return pl.pallas_call(
        functools.partial(_fox_prep_kernel, dh=dh, qscale=dh ** -0.5 * LOG2E),
        grid=(B, T // tp),
        in_specs=[spec(cols["fq"]), spec(cols["fk"]), spec(cols["fv"]),
                  pl.BlockSpec((None, tp, GATE_LANES), lambda b, i: (b, i, 0)), gain, gain],
        out_specs=[out(wa), out(wa), out(width), out(width), out(width)],
        out_shape=[jax.ShapeDtypeStruct((B, T, wa), BF16), jax.ShapeDtypeStruct((B, T, wa), BF16),
                   jax.ShapeDtypeStruct((B, T, width), BF16),
                   jax.ShapeDtypeStruct((B, T, width), F32), jax.ShapeDtypeStruct((B, T, width), F32)],
        scratch_shapes=[pltpu.VMEM((1, GATE_LANES), F32)],
        compiler_params=_params("parallel", "arbitrary"),
        name="fox_prep",
    )(h3, h3, h3, g3, q_gain.reshape(1, dh), k_gain.reshape(1, dh))


def _lane_tiles_reduce(x, op):
    acc = x[:, :128]
    for j in range(1, x.shape[1] // 128):
        acc = op(acc, x[:, j * 128:(j + 1) * 128])
    return acc


def _fox_flash_kernel(q_ref, k_ref, v_ref, z_ref, o_ref, m_sc, l_sc, acc_sc, mx_sc, s_sc, p_sc, *, sub):
    qi = pl.program_id(2)
    ki = pl.program_id(3)
    t = q_ref.shape[0]
    ns = t // sub

    @pl.when(ki == 0)
    def _():
        m_sc[...] = jnp.full_like(m_sc, -jnp.inf)
        l_sc[...] = jnp.zeros_like(l_sc)
        acc_sc[...] = jnp.zeros_like(acc_sc)

    def block(diag):
        row_l = lax.broadcasted_iota(jnp.int32, (sub, sub), 0)
        col_l = lax.broadcasted_iota(jnp.int32, (sub, sub), 1)
        n_chunks = lambda i: i + 1 if diag else ns
        for i in range(ns):
            rows = slice(i * sub, (i + 1) * sub)
            q_i = q_ref[rows, :]
            mx = None
            for c in range(n_chunks(i)):
                cs = slice(c * sub, (c + 1) * sub)
                s = lax.dot_general(q_i, k_ref[cs, :], NT_DIMS, preferred_element_type=F32)
                if diag and c == i:
                    s = jnp.where(col_l <= row_l, s, -jnp.inf)
                s_sc[rows, cs] = s
                cm = _lane_tiles_reduce(s, jnp.maximum)
                mx = cm if mx is None else jnp.maximum(mx, cm)
            mx_sc[rows, :] = mx
        for i in range(ns):
            rows = slice(i * sub, (i + 1) * sub)
            m_prev = m_sc[rows, :]
            m_new = jnp.maximum(m_prev, jnp.max(mx_sc[rows, :], axis=1, keepdims=True))
            alpha = jnp.exp2(m_prev - m_new)
            ls = None
            for c in range(n_chunks(i)):
                cs = slice(c * sub, (c + 1) * sub)
                p = jnp.exp2(s_sc[rows, cs] - m_new)
                p_sc[rows, cs] = p.astype(BF16)
                lp = _lane_tiles_reduce(p, jnp.add)
                ls = lp if ls is None else ls + lp
            w = n_chunks(i) * sub
            l_sc[rows, :] = alpha * l_sc[rows, :] + jnp.sum(ls, axis=1, keepdims=True)
            acc_sc[rows, :] = alpha * acc_sc[rows, :] + jnp.dot(p_sc[rows, :w], v_ref[:w, :],
                                                                preferred_element_type=F32)
            m_sc[rows, :] = m_new

    @pl.when(ki < qi)
    def _():
        block(False)

    @pl.when(ki == qi)
    def _():
        block(True)
        zz = z_ref[...]
        o_ref[...] = (zz * _sigmoid(zz) * (acc_sc[...] / l_sc[...])).astype(o_ref.dtype)


def _fox_flash(qa, ka, vb, h3, cols, *, dh, t, sub):
    B, T, W = vb.shape
    t = min(t, T)
    sub = min(sub, t)
    nq = T // t
    da = dh + N_AUG
    zo = cols["fz"] // dh
    return pl.pallas_call(
        functools.partial(_fox_flash_kernel, sub=sub),
        grid=(B, W // dh, nq, nq),
        in_specs=[
            pl.BlockSpec((None, t, da), lambda b, h, i, j: (b, i, h)),
            pl.BlockSpec((None, t, da), lambda b, h, i, j: (b, jnp.minimum(i, j), h)),
            pl.BlockSpec((None, t, dh), lambda b, h, i, j: (b, jnp.minimum(i, j), h)),
            pl.BlockSpec((None, t, dh), lambda b, h, i, j: (b, i, zo + h)),
        ],
        out_specs=pl.BlockSpec((None, t, dh), lambda b, h, i, j: (b, i, h)),
        out_shape=jax.ShapeDtypeStruct((B, T, W), BF16),
        scratch_shapes=[pltpu.VMEM((t, 1), F32), pltpu.VMEM((t, 1), F32), pltpu.VMEM((t, dh), F32),
                        pltpu.VMEM((t, 128), F32), pltpu.VMEM((t, t), F32), pltpu.VMEM((t, t), BF16)],
        compiler_params=_params("parallel", "parallel", "parallel", "arbitrary"),
        name="fox_flash",
    )(qa, ka, vb, h3)


def _decay_selector(P, nh):
    t_src = jnp.arange(P)[:, None]
    t_dst = jnp.arange(P * nh)[None, :] // nh
    later = (t_src > t_dst).astype(BF16)
    return jnp.concatenate([later, jnp.ones((P, 128), BF16)], axis=1)


def _fox_decode_kernel(pt_ref, q_ref, kn_ref, vn_ref, z_ref, lfn_ref, u_ref, *rest, scale, nh, G):
    k_refs, v_refs, lf_refs = rest[:G], rest[G:2 * G], rest[2 * G:3 * G]
    o_ref, m_sc, l_sc, acc_sc, carry_sc = rest[3 * G:]
    p = pl.program_id(1)
    q = q_ref[...]

    @pl.when(p == 0)
    def _():
        s_new = jnp.sum(q.astype(F32) * kn_ref[...], axis=1, keepdims=True) * scale
        m_sc[...] = s_new
        l_sc[...] = jnp.ones_like(l_sc)
        acc_sc[...] = vn_ref[...]
        carry_sc[...] = lfn_ref[...]

    P = k_refs[0].shape[0]
    n = P * nh
    sub = lax.broadcasted_iota(jnp.int32, (nh, n), 0)
    lane = lax.broadcasted_iota(jnp.int32, (nh, n), 1)
    diag = (lane & (nh - 1)) == sub
    carry = carry_sc[...]
    scores = []
    for g in range(G):
        k2 = k_refs[g][...].reshape(n, k_refs[g].shape[2]).astype(BF16)
        s = lax.dot_general(q, k2, NT_DIMS, preferred_element_type=F32) * scale
        l1, l2, l3 = _split3(lf_refs[g][...].T)
        zz = jnp.dot(jnp.concatenate([l1, l2, l3], axis=0), u_ref[...], preferred_element_type=F32)
        zs = zz[0:nh] + zz[nh:2 * nh] + zz[2 * nh:3 * nh]
        s = s + zs[:, :n] + jnp.tile(carry, (1, n // 128))
        carry = carry + zs[:, n:]
        scores.append(jnp.where(diag, s, -jnp.inf))
    carry_sc[...] = carry

    m_prev = m_sc[...]
    m_new = m_prev
    for s in scores:
        m_new = jnp.maximum(m_new, jnp.max(s, axis=1, keepdims=True))
    alpha = jnp.exp(m_prev - m_new)
    l_new = alpha * l_sc[...]
    acc = alpha * acc_sc[...]
    for g in range(G):
        pr = jnp.exp(scores[g] - m_new)
        v2 = v_refs[g][...].reshape(n, v_refs[g].shape[2]).astype(BF16)
        l_new = l_new + jnp.sum(pr, axis=1, keepdims=True)
        acc = acc + jnp.dot(pr.astype(BF16), v2, preferred_element_type=F32)
    l_sc[...] = l_new
    acc_sc[...] = acc
    m_sc[...] = m_new

    @pl.when(p == pl.num_programs(1) - 1)
    def _():
        zz = z_ref[...]
        o_ref[...] = zz * _sigmoid(zz) * (acc / l_new)


def _fox_decode(page_table, q3, kn3, vn3, z3, lfn3, k_pool, v_pool, lf_pool, *, dh, G):
    N, nh, _ = q3.shape
    n_pages = page_table.shape[1]
    P = k_pool.shape[1]
    G = math.gcd(G, n_pages)
    assert nh & (nh - 1) == 0
    u = _decay_selector(P, nh)
    per = lambda: pl.BlockSpec((None, nh, dh), lambda b, p, pt: (b, 0, 0))
    page = lambda g: (lambda b, p, pt, g=g: (pt[b, n_pages - 1 - (p * G + g)], 0, 0, 0))
    page3 = lambda g: (lambda b, p, pt, g=g: (pt[b, n_pages - 1 - (p * G + g)], 0, 0))
    grid_spec = pltpu.PrefetchScalarGridSpec(
        num_scalar_prefetch=1,
        grid=(N, n_pages // G),
        in_specs=[per(), per(), per(), per(),
                  pl.BlockSpec((None, nh, 128), lambda b, p, pt: (b, 0, 0)),
                  pl.BlockSpec(u.shape, lambda b, p, pt: (0, 0))]
                 + [pl.BlockSpec((None, P, nh, dh), page(g)) for g in range(G)]
                 + [pl.BlockSpec((None, P, nh, dh), page(g)) for g in range(G)]
                 + [pl.BlockSpec((None, P, nh), page3(g)) for g in range(G)],
        out_specs=pl.BlockSpec((None, nh, dh), lambda b, p, pt: (b, 0, 0)),
        scratch_shapes=[pltpu.VMEM((nh, 1), F32), pltpu.VMEM((nh, 1), F32), pltpu.VMEM((nh, dh), F32),
                        pltpu.VMEM((nh, 128), F32)],
    )
    return pl.pallas_call(
        functools.partial(_fox_decode_kernel, scale=dh ** -0.5, nh=nh, G=G),
        grid_spec=grid_spec,
        out_shape=jax.ShapeDtypeStruct((N, nh, dh), F32),
        compiler_params=_params("parallel", "arbitrary"),
        name="fox_decode",
    )(page_table, q3, kn3, vn3, z3, lfn3, u, *([k_pool] * G), *([v_pool] * G), *([lf_pool] * G))


def _mem_attn_kernel(q_ref, z_ref, k_ref, v_ref, qg_ref, a_ref, *, dh, scale):
    nh = q_ref.shape[1] // dh
    for h in range(nh):
        sl = slice(h * dh, (h + 1) * dh)
        q = _rms_rows(q_ref[:, sl], qg_ref[...]).astype(BF16)
        s = lax.dot_general(q, k_ref[:, sl].astype(BF16), NT_DIMS, preferred_element_type=F32) * scale
        e = jnp.exp(s - jnp.max(s, axis=1, keepdims=True))
        o = jnp.dot(e.astype(BF16), v_ref[:, sl].astype(BF16), preferred_element_type=F32)
        o = o / jnp.sum(e, axis=1, keepdims=True)
        zz = z_ref[:, sl]
        a_ref[:, sl] = (zz * _sigmoid(zz) * o).astype(a_ref.dtype)


def _mem_attn_prompt(h3, mk, mv, q_gain, cols, *, width, dh, tq):
    B, T, _ = h3.shape
    n_mem = mk.shape[1]
    tq = min(tq, T)
    spec = lambda off: pl.BlockSpec((None, tq, width), lambda b, i, off=off: (b, i, off // width))
    mem = pl.BlockSpec((None, n_mem, width), lambda b, i: (b, 0, 0))
    return pl.pallas_call(
        functools.partial(_mem_attn_kernel, dh=dh, scale=dh ** -0.5),
        grid=(B, T // tq),
        in_specs=[spec(cols["cq"]), spec(cols["cz"]), mem, mem, pl.BlockSpec((1, dh), lambda b, i: (0, 0))],
        out_specs=pl.BlockSpec((None, tq, width), lambda b, i: (b, i, 0)),
        out_shape=jax.ShapeDtypeStruct((B, T, width), BF16),
        compiler_params=_params("parallel", "parallel"),
        name="mem_attn",
    )(h3, h3, mk, mv, q_gain.reshape(1, dh))


def _mem_decode_kernel(q_ref, z_ref, k_ref, v_ref, qg_ref, o_ref, *, scale, nh):
    q = _rms_rows(q_ref[...], qg_ref[...]).astype(BF16)
    kp = k_ref[...]
    vp = v_ref[...]
    rows = kp.shape[0] * kp.shape[1]
    k2 = kp.reshape(rows, kp.shape[2]).astype(BF16)
    v2 = vp.reshape(rows, vp.shape[2]).astype(BF16)
    s = lax.dot_general(q, k2, NT_DIMS, preferred_element_type=F32) * scale
    sub = lax.broadcasted_iota(jnp.int32, s.shape, 0)
    lane = lax.broadcasted_iota(jnp.int32, s.shape, 1)
    s = jnp.where((lane & (nh - 1)) == sub, s, -jnp.inf)
    m = jnp.maximum(jnp.max(s, axis=1, keepdims=True), -1e30)
    e = jnp.exp(s - m)
    den = jnp.maximum(jnp.sum(e, axis=1, keepdims=True), 1e-30)
    o = jnp.dot(e.astype(BF16), v2, preferred_element_type=F32) / den
    zz = z_ref[...]
    o_ref[...] = zz * _sigmoid(zz) * o


def _mem_decode(q3, z3, k_cache, v_cache, q_gain, *, dh):
    N, n_mem, nh, _ = k_cache.shape
    assert nh & (nh - 1) == 0
    per = pl.BlockSpec((None, 8, dh), lambda b: (b, 0, 0))
    cache = pl.BlockSpec((None, n_mem, nh, dh), lambda b: (b, 0, 0, 0))
    return pl.pallas_call(
        functools.partial(_mem_decode_kernel, scale=dh ** -0.5, nh=nh),
        grid=(N,),
        in_specs=[per, per, cache, cache, pl.BlockSpec((1, dh), lambda b: (0, 0))],
        out_specs=per,
        out_shape=jax.ShapeDtypeStruct((N, 8, dh), F32),
        compiler_params=_params("parallel"),
        name="mem_decode",
    )(q3, z3, k_cache, v_cache, q_gain.reshape(1, dh))


def _merge_kernel(am_ref, af_ref, ac_ref, wm_ref, wf_ref, wc_ref, gm_ref, gf_ref, gc_ref, u_ref):
    d = lambda a, w: jnp.dot(a[...], w[...], preferred_element_type=F32)
    u = _sigmoid(gm_ref[...]) * d(am_ref, wm_ref)
    u = u + _sigmoid(gf_ref[...]) * d(af_ref, wf_ref)
    u = u + _sigmoid(gc_ref[...]) * d(ac_ref, wc_ref)
    u_ref[...] = u.astype(u_ref.dtype)


def _merge(am, af, ac, wm, wf, wc, h2, g_off, *, tm, tn):
    m, k = am.shape
    n = wm.shape[1]
    tm = min(tm, m)
    a_spec = pl.BlockSpec((tm, k), lambda i, j: (i, 0))
    w_spec = pl.BlockSpec((k, tn), lambda i, j: (0, j))
    g_spec = lambda b: pl.BlockSpec((tm, tn), lambda i, j, b=b: (i, (g_off + b * n) // tn + j))
    return pl.pallas_call(
        _merge_kernel,
        grid=(m // tm, n // tn),
        in_specs=[a_spec, a_spec, a_spec, w_spec, w_spec, w_spec, g_spec(0), g_spec(1), g_spec(2)],
        out_specs=pl.BlockSpec((tm, tn), lambda i, j: (i, j)),
        out_shape=jax.ShapeDtypeStruct((m, n), BF16),
        compiler_params=_params("parallel", "arbitrary"),
        name="merge",
    )(am, af, ac, wm, wf, wc, h2, h2, h2)


def _out_proj_kernel(u_ref, w_ref, x_ref, y_ref):
    y_ref[...] = x_ref[...] + jnp.dot(u_ref[...], w_ref[...], preferred_element_type=F32)


def _out_proj(u, w, x, *, tm, tn):
    m, k = u.shape
    n = w.shape[1]
    tm = min(tm, m)
    return pl.pallas_call(
        _out_proj_kernel,
        grid=(m // tm, n // tn),
        in_specs=[
            pl.BlockSpec((tm, k), lambda i, j: (i, 0)),
            pl.BlockSpec((k, tn), lambda i, j: (0, j)),
            pl.BlockSpec((tm, tn), lambda i, j: (i, j)),
        ],
        out_specs=pl.BlockSpec((tm, tn), lambda i, j: (i, j)),
        out_shape=jax.ShapeDtypeStruct((m, n), F32),
        compiler_params=_params("parallel", "arbitrary"),
        name="out_proj",
    )(u, w, x)


def _column_layout(d_model):
    ml_dv = d_model // ML_HEADS
    ml_qkw = ML_HEADS * (ml_dv // 2)
    names = ("mq", "mk", "mv", "mi", "mf", "mo", "mz", "fq", "fk", "fv", "ff", "fz", "cq", "cz", "g")
    widths = (ml_qkw, ml_qkw, d_model, ML_HEADS, ML_HEADS, d_model, d_model,
              d_model, d_model, d_model, FOX_HEADS, d_model, d_model, d_model, N_BRANCH * d_model)
    src, off = {}, 0
    for n, w in zip(names, widths):
        src[n] = (off, w)
        off += w
    big, boff = {}, 0
    for n in names:
        if n in ("mi", "mf", "ff"):
            continue
        big[n] = boff
        boff += src[n][1]
    return src, big, boff


def _layer(x_p, x_s, mem_p, fox_k_pool, fox_v_pool, fox_lf_pool, page_table, mem_k_s, mem_v_s,
           c0_s, n0_s, m0_s, norm_in, norm_mem, w_in, w_mem_kv, b_i, b_f, b_ff, ml_out_norm,
           fq_norm, fk_norm, cq_norm, ck_norm, w_br_m, w_br_f, w_br_c, w_out):
    B, T, D = x_p.shape
    N = x_s.shape[0]
    dv = D // ML_HEADS
    dqk = dv // 2
    fox_dh = D // FOX_HEADS
    mem_dh = D // MEM_HEADS
    src, cols, nbig = _column_layout(D)
    col = lambda n: w_in[:, src[n][0]:src[n][0] + src[n][1]]

    w_big = jnp.concatenate([col(n) for n in cols], axis=1).astype(BF16)
    n_gate = 2 * ML_HEADS + FOX_HEADS
    w_gate = jnp.concatenate([col("mi"), col("mf"), col("ff"),
                              jnp.zeros((D, GATE_LANES - n_gate), F32)], axis=1)
    b_gate = jnp.concatenate([b_i, b_f, b_ff, jnp.zeros((GATE_LANES - n_gate,), F32)]).reshape(1, GATE_LANES)
    w_mem = w_mem_kv.astype(BF16)
    wm, wf, wc, wo = (w.astype(BF16) for w in (w_br_m, w_br_f, w_br_c, w_out))

    xp2 = x_p.reshape(B * T, D)
    xs2 = x_s.reshape(N, D)

    h2 = _norm_proj(xp2, norm_in, w_big, tm=1024, tn=1024)
    g2 = _gates(xp2, norm_in, w_gate, b_gate, tm=1024)
    h3 = h2.reshape(B, T, nbig)
    g3 = g2.reshape(B, T, GATE_LANES)

    n_mem = mem_p.shape[1]
    mem2 = mem_p.reshape(B * n_mem, D)
    mk_p = _norm_proj(mem2, norm_mem, w_mem[:, :D], tm=512, tn=mem_dh, head_gain=ck_norm)
    mv_p = _norm_proj(mem2, norm_mem, w_mem[:, D:], tm=512, tn=mem_dh)

    a_m, c_p, n_p, m_p = _mlstm_prompt(h3, g3, ml_out_norm, cols, dqk=dqk, dv=dv)

    qa, ka, vb, kn, fv_p = _fox_prep(h3, g3, fq_norm, fk_norm, cols, width=D, dh=fox_dh, tp=256)
    a_f = _fox_flash(qa, ka, vb, h3, cols, dh=fox_dh, t=1024, sub=256)

    a_c = _mem_attn_prompt(h3, mk_p.reshape(B, n_mem, D), mv_p.reshape(B, n_mem, D), cq_norm, cols,
                           width=D, dh=mem_dh, tq=512)

    u_p = _merge(a_m.reshape(B * T, D), a_f.reshape(B * T, D), a_c.reshape(B * T, D),
                 wm, wf, wc, h2, cols["g"], tm=1024, tn=256)
    y_p = _out_proj(u_p, wo, xp2, tm=1024, tn=1024).reshape(B, T, D)

    hs2 = _norm_proj(xs2, norm_in, w_big, tm=128, tn=1024)
    gs2 = _gates(xs2, norm_in, w_gate, b_gate, tm=128)
    hs3 = hs2.reshape(N, 1, nbig)
    gs3 = gs2.reshape(N, 1, GATE_LANES)

    a_ms, c_s, n_s, m_s = _mlstm_sample(hs3, gs3, m0_s.reshape(N, 1, ML_HEADS), ml_out_norm, c0_s, n0_s,
                                        cols, dqk=dqk, dv=dv)

    hs1 = hs2.reshape(1, N, nbig)
    qn_s, kn_s = _qk_norm(hs1, fq_norm, fk_norm, cols, width=D, dh=fox_dh, tq=128)
    seg = lambda n: hs2[:, cols[n]:cols[n] + D]
    fv_s = seg("fv")
    flf_s = gs2[:, 2 * ML_HEADS:n_gate]
    lfn = jnp.broadcast_to(flf_s[:, :, None], (N, FOX_HEADS, 128))
    to_heads = lambda a: a.reshape(N, FOX_HEADS, fox_dh)
    o_f = _fox_decode(page_table, to_heads(qn_s.reshape(N, D)), to_heads(kn_s.reshape(N, D)), to_heads(fv_s),
                      to_heads(seg("fz")), lfn, fox_k_pool, fox_v_pool, fox_lf_pool, dh=fox_dh, G=4)
    a_fs = o_f.reshape(N, D).astype(BF16)

    pad8 = lambda a: jnp.pad(a.reshape(N, MEM_HEADS, mem_dh), ((0, 0), (0, 8 - MEM_HEADS), (0, 0)))
    o_c = _mem_decode(pad8(seg("cq")), pad8(seg("cz")), mem_k_s, mem_v_s, cq_norm, dh=mem_dh)
    a_cs = o_c[:, :MEM_HEADS].reshape(N, D).astype(BF16)

    u_s = _merge(a_ms.reshape(N, D), a_fs, a_cs, wm, wf, wc, hs2, cols["g"], tm=128, tn=512)
    y_s = _out_proj(u_s, wo, xs2, tm=128, tn=1024).reshape(N, 1, D)

    outs_p = (kn.reshape(B, T, FOX_HEADS, fox_dh), fv_p.reshape(B, T, FOX_HEADS, fox_dh),
              g3[:, :, 2 * ML_HEADS:n_gate],
              mk_p.reshape(B, n_mem, MEM_HEADS, mem_dh), mv_p.reshape(B, n_mem, MEM_HEADS, mem_dh),
              c_p, n_p.reshape(B, ML_HEADS, dqk), m_p.reshape(B, ML_HEADS))
    outs_s = (kn_s.reshape(N, 1, FOX_HEADS, fox_dh), fv_s.reshape(N, 1, FOX_HEADS, fox_dh),
              flf_s.reshape(N, 1, FOX_HEADS), c_s, n_s, m_s.reshape(N, ML_HEADS))
    return y_p, y_s, outs_p, outs_s


def kernel(x_prompt, x_sample, mem_prompt, cache_fox_k, cache_fox_v, cache_fox_logf, page_table, cache_mem_k, cache_mem_v, state_mlstm_C, state_mlstm_n, state_mlstm_m, norm_in, norm_mem, w_in, w_mem_kv, b_mlstm_i, b_mlstm_f, b_fox_f, mlstm_out_norm, fox_q_norm, fox_k_norm, mem_q_norm, mem_k_norm, w_br_mlstm, w_br_fox, w_br_mem, w_out):
    depth = w_in.shape[0]
    y_p, y_s = x_prompt, x_sample
    per_layer = []
    for l in range(depth):
        y_p, y_s, outs_p, outs_s = _layer(
            y_p, y_s, mem_prompt, cache_fox_k[l], cache_fox_v[l], cache_fox_logf[l], page_table,
            cache_mem_k[l], cache_mem_v[l], state_mlstm_C[l], state_mlstm_n[l], state_mlstm_m[l],
            norm_in[l], norm_mem[l], w_in[l], w_mem_kv[l], b_mlstm_i[l], b_mlstm_f[l], b_fox_f[l],
            mlstm_out_norm[l], fox_q_norm[l], fox_k_norm[l], mem_q_norm[l], mem_k_norm[l],
            w_br_mlstm[l], w_br_fox[l], w_br_mem[l], w_out[l])
        per_layer.append(outs_p + outs_s)
    stacked = tuple(jnp.stack([lay[i] for lay in per_layer]) for i in range(len(per_layer[0])))
    return (y_p, y_s) + stacked
```

```python
import functools
import math

import jax
import jax.numpy as jnp
from jax import lax
from jax.experimental import pallas as pl
from jax.experimental.pallas import tpu as pltpu

F32 = jnp.float32
BF16 = jnp.bfloat16

ML_HEADS = 4
FOX_HEADS = 16
MEM_HEADS = 4
N_BRANCH = 3
ML_CHUNK = 128
EPS = 1e-6
LOG2E = 1.4426950408889634
GATE_LANES = 128
VMEM_LIMIT = 56 * 1024 * 1024

NT_DIMS = (((1,), (1,)), ((), ()))
TN_DIMS = (((0,), (0,)), ((), ()))


def _params(*sem):
    return pltpu.CompilerParams(dimension_semantics=sem, vmem_limit_bytes=VMEM_LIMIT)


def _sigmoid(z):
    return 1.0 / (1.0 + jnp.exp(-z))


def _log_sigmoid(z):
    return jnp.minimum(z, 0.0) - jnp.log1p(jnp.exp(-jnp.abs(z)))


def _split3(x):
    x1 = x.astype(BF16)
    r1 = x - x1.astype(F32)
    x2 = r1.astype(BF16)
    r2 = r1 - x2.astype(F32)
    return x1, x2, r2.astype(BF16)


def _dot_sel_left(sel, x):
    x1, x2, x3 = _split3(x)
    d = lambda p: jnp.dot(sel, p, preferred_element_type=F32)
    return d(x1) + d(x2) + d(x3)


def _rms_rows(x, gain):
    return x * lax.rsqrt(jnp.mean(x * x, axis=-1, keepdims=True) + EPS) * gain


def _norm_proj_kernel(x_ref, g_ref, w_ref, *rest, head_norm, transposed):
    if head_norm:
        hg_ref, o_ref, xn_ref = rest
    else:
        o_ref, xn_ref = rest

    @pl.when(pl.program_id(1) == 0)
    def _():
        xn_ref[...] = _rms_rows(x_ref[...], g_ref[...]).astype(BF16)

    w = w_ref[...].astype(BF16)
    if transposed:
        acc = lax.dot_general(xn_ref[...], w, NT_DIMS, preferred_element_type=F32)
    else:
        acc = jnp.dot(xn_ref[...], w, preferred_element_type=F32)
    if head_norm:
        acc = _rms_rows(acc, hg_ref[...])
    o_ref[...] = acc.astype(o_ref.dtype)


def _norm_proj(x, gain, w, *, tm, tn, n_out, w_index, transposed, head_gain=None, out_dtype=F32):
    m, k = x.shape
    n = n_out
    tm = min(tm, m)
    assert m % tm == 0 and n % tn == 0
    if transposed:
        w_spec = pl.BlockSpec((pl.Element(tn), pl.Element(k)), lambda i, j: (pl.multiple_of(w_index(j), 8), 0))
    else:
        w_spec = pl.BlockSpec((k, tn), lambda i, j: (0, w_index(j)))
    in_specs = [
        pl.BlockSpec((tm, k), lambda i, j: (i, 0)),
        pl.BlockSpec((1, k), lambda i, j: (0, 0)),
        w_spec,
    ]
    args = [x, gain.reshape(1, k), w]
    if head_gain is not None:
        assert head_gain.shape == (tn,)
        in_specs.append(pl.BlockSpec((1, tn), lambda i, j: (0, 0)))
        args.append(head_gain.reshape(1, tn))
    return pl.pallas_call(
        functools.partial(_norm_proj_kernel, head_norm=head_gain is not None, transposed=transposed),
        grid=(m // tm, n // tn),
        in_specs=in_specs,
        out_specs=pl.BlockSpec((tm, tn), lambda i, j: (i, j)),
        out_shape=jax.ShapeDtypeStruct((m, n), out_dtype),
        scratch_shapes=[pltpu.VMEM((tm, k), BF16)],
        compiler_params=_params("parallel", "arbitrary"),
        name="norm_proj",
    )(*args)


def _gates_kernel(x_ref, g_ref, w_ref, b_ref, o_ref):
    xn = _rms_rows(x_ref[...], g_ref[...])
    x1, x2, _ = _split3(xn)
    w1, w2, _ = _split3(w_ref[...])
    d = lambda a, b: lax.dot_general(a, b, NT_DIMS, preferred_element_type=F32)
    z = d(x1, w1) + d(x1, w2) + d(x2, w1) + b_ref[...]
    lane = lax.broadcasted_iota(jnp.int32, z.shape, 1)
    o_ref[...] = jnp.where(lane < ML_HEADS, z, _log_sigmoid(z))


def _gates(x, gain, w_gate, bias, *, tm):
    m, k = x.shape
    tm = min(tm, m)
    return pl.pallas_call(
        _gates_kernel,
        grid=(m // tm,),
        in_specs=[
            pl.BlockSpec((tm, k), lambda i: (i, 0)),
            pl.BlockSpec((1, k), lambda i: (0, 0)),
            pl.BlockSpec((GATE_LANES, k), lambda i: (0, 0)),
            pl.BlockSpec((1, GATE_LANES), lambda i: (0, 0)),
        ],
        out_specs=pl.BlockSpec((tm, GATE_LANES), lambda i: (i, 0)),
        out_shape=jax.ShapeDtypeStruct((m, GATE_LANES), F32),
        compiler_params=_params("parallel"),
        name="gates",
    )(x, gain.reshape(1, k), w_gate, bias)


def _mlstm_chunk_kernel(q_ref, k_ref, v_ref, o_ref, z_ref, g_ref, gain_ref,
                        a_ref, c_out, n_out, m_out, c_sc, n_sc, m_sc, *, dqk):
    h = pl.program_id(1)
    c_idx = pl.program_id(2)
    L = q_ref.shape[0]

    @pl.when(c_idx == 0)
    def _():
        c_sc[...] = jnp.zeros_like(c_sc)
        n_sc[...] = jnp.zeros_like(n_sc)
        m_sc[...] = jnp.zeros_like(m_sc)

    row = lax.broadcasted_iota(jnp.int32, (L, L), 0)
    col = lax.broadcasted_iota(jnp.int32, (L, L), 1)
    causal = col <= row
    tri = jnp.where(causal, 1.0, 0.0).astype(BF16)

    g = g_ref[...]
    cum = _dot_sel_left(tri, g)
    g_t = g.T
    cum_t = cum.T
    lane = lax.broadcasted_iota(jnp.int32, (L, GATE_LANES), 1)
    sub = lax.broadcasted_iota(jnp.int32, (GATE_LANES, L), 0)
    pick_col = lambda x, idx: jnp.sum(jnp.where(lane == idx, x, 0.0), axis=1, keepdims=True)
    pick_row = lambda x, idx: jnp.sum(jnp.where(sub == idx, x, 0.0), axis=0, keepdims=True)
    ig_col = pick_col(g, h)
    b_col = pick_col(cum, ML_HEADS + h)
    ig_row = pick_row(g_t, h)
    b_row = pick_row(cum_t, ML_HEADS + h)

    m_prev = m_sc[...]
    a_col = b_col + m_prev
    dmat = jnp.where(causal, b_col - b_row + ig_row, -jnp.inf)
    m_t = jnp.maximum(a_col, jnp.max(dmat, axis=1, keepdims=True))
    w_inter = jnp.exp(a_col - m_t)
    w_intra = jnp.exp(dmat - m_t)

    qf = q_ref[...] * (dqk ** -0.5)
    kf = k_ref[...]
    q = qf.astype(BF16)
    k = kf.astype(BF16)
    v = v_ref[...].astype(BF16)
    c_prev = c_sc[...]
    n_prev = n_sc[...]

    s = lax.dot_general(q, k, NT_DIMS, preferred_element_type=F32) * w_intra
    inter = lax.dot_general(q, c_prev.astype(BF16), NT_DIMS, preferred_element_type=F32)
    num = w_inter * inter + jnp.dot(s.astype(BF16), v, preferred_element_type=F32)
    nq = w_inter * jnp.sum(qf * n_prev, axis=1, keepdims=True) + jnp.sum(s, axis=1, keepdims=True)
    hid = num / jnp.maximum(jnp.abs(nq), jnp.exp(-m_t))

    b_last = b_col[L - 1:L, :]
    m_new = m_t[L - 1:L, :]
    w_c = jnp.exp(b_last + m_prev - m_new)
    w_s = jnp.exp(b_last - b_col + ig_col - m_new)
    wv = (w_s * v_ref[...]).astype(BF16)
    c_new = w_c * c_prev + lax.dot_general(wv, k, TN_DIMS, preferred_element_type=F32)
    n_new = w_c * n_prev + jnp.sum(w_s * kf, axis=0, keepdims=True)
    c_sc[...] = c_new
    n_sc[...] = n_new
    m_sc[...] = m_new

    hn = _rms_rows(hid, gain_ref[...])
    zz = z_ref[...]
    a_ref[...] = (zz * _sigmoid(zz) * (_sigmoid(o_ref[...]) * hn)).astype(a_ref.dtype)

    @pl.when(c_idx == pl.num_programs(2) - 1)
    def _():
        c_out[...] = c_new
        n_out[...] = n_new
        m_out[...] = m_new


def _mlstm_prompt(h3, g3, out_gain, cols, *, dqk, dv):
    B, T, _ = h3.shape
    L = math.gcd(T, ML_CHUNK)
    nc = T // L
    qo, ko, vo, oo, zo = (cols[n] for n in ("mq", "mk", "mv", "mo", "mz"))
    blk = lambda w, off: pl.BlockSpec((None, L, w), lambda b, h, c, off=off, w=w: (b, c, off // w + h))
    kern = functools.partial(_mlstm_chunk_kernel, dqk=dqk)
    return pl.pallas_call(
        kern,
        grid=(B, ML_HEADS, nc),
        in_specs=[
            blk(dqk, qo), blk(dqk, ko), blk(dv, vo), blk(dv, oo), blk(dv, zo),
            pl.BlockSpec((None, L, GATE_LANES), lambda b, h, c: (b, c, 0)),
            pl.BlockSpec((1, dv), lambda b, h, c: (0, h)),
        ],
        out_specs=[
            pl.BlockSpec((None, L, dv), lambda b, h, c: (b, c, h)),
            pl.BlockSpec((None, None, dv, dqk), lambda b, h, c: (b, h, 0, 0)),
            pl.BlockSpec((None, None, 1, dqk), lambda b, h, c: (b, h, 0, 0)),
            pl.BlockSpec((None, None, 1, 1), lambda b, h, c: (b, h, 0, 0)),
        ],
        out_shape=[
            jax.ShapeDtypeStruct((B, T, ML_HEADS * dv), BF16),
            jax.ShapeDtypeStruct((B, ML_HEADS, dv, dqk), F32),
            jax.ShapeDtypeStruct((B, ML_HEADS, 1, dqk), F32),
            jax.ShapeDtypeStruct((B, ML_HEADS, 1, 1), F32),
        ],
        scratch_shapes=[pltpu.VMEM((dv, dqk), F32), pltpu.VMEM((1, dqk), F32), pltpu.VMEM((1, 1), F32)],
        compiler_params=_params("parallel", "parallel", "arbitrary"),
        name="mlstm_chunk",
    )(h3, h3, h3, h3, h3, g3, out_gain.reshape(1, ML_HEADS * dv))


def _mlstm_step_kernel(q_ref, k_ref, v_ref, o_ref, z_ref, g_ref, m0_ref, gain_ref, c0_ref, n0_ref,
                       a_ref, c_out, n_out, m_out, *, dqk, dv):
    g = g_ref[...]
    m0 = m0_ref[...]
    lane_h = lax.broadcasted_iota(jnp.int32, (1, ML_HEADS), 1)
    m_acc = jnp.zeros((1, ML_HEADS), F32)
    for h in range(ML_HEADS):
        ig = g[:, h:h + 1]
        lf = g[:, ML_HEADS + h:ML_HEADS + h + 1]
        a = lf + m0[:, h:h + 1]
        m_t = jnp.maximum(a, ig)
        w_inter = jnp.exp(a - m_t)
        w_intra = jnp.exp(ig - m_t)
        qf = q_ref[:, h * dqk:(h + 1) * dqk] * (dqk ** -0.5)
        kf = k_ref[:, h * dqk:(h + 1) * dqk]
        vf = v_ref[:, h * dv:(h + 1) * dv]
        c0 = c0_ref[h]
        n0 = n0_ref[h:h + 1, :]
        s = jnp.sum(qf * kf, axis=1, keepdims=True) * w_intra
        q8 = jnp.broadcast_to(qf, (8, dqk)).astype(BF16)
        cq = lax.dot_general(q8, c0.astype(BF16), NT_DIMS, preferred_element_type=F32)[0:1, :]
        num = w_inter * cq + s * vf
        nq = w_inter * jnp.sum(n0 * qf, axis=1, keepdims=True) + s
        hid = num / jnp.maximum(jnp.abs(nq), jnp.exp(-m_t))
        first = lax.broadcasted_iota(jnp.int32, (16, 1), 0) == 0
        v16 = jnp.where(first, vf, 0.0).astype(BF16)
        k16 = jnp.where(first, w_intra * kf, 0.0).astype(BF16)
        c_out[h] = w_inter * c0 + lax.dot_general(v16, k16, TN_DIMS, preferred_element_type=F32)
        n_out[h:h + 1, :] = w_inter * n0 + w_intra * kf
        m_acc = m_acc + jnp.where(lane_h == h, m_t, 0.0)
        hn = _rms_rows(hid, gain_ref[:, h * dv:(h + 1) * dv])
        zz = z_ref[:, h * dv:(h + 1) * dv]
        gate_o = _sigmoid(o_ref[:, h * dv:(h + 1) * dv])
        a_ref[:, h * dv:(h + 1) * dv] = (zz * _sigmoid(zz) * (gate_o * hn)).astype(a_ref.dtype)
    m_out[...] = m_acc


def _mlstm_sample(hs3, gs3, m0, out_gain, c0, n0, cols, *, dqk, dv):
    N = hs3.shape[0]
    W = ML_HEADS * dv
    WQ = ML_HEADS * dqk
    qo, ko, vo, oo, zo = (cols[n] for n in ("mq", "mk", "mv", "mo", "mz"))
    row = lambda w, off: pl.BlockSpec((None, 1, w), lambda b, off=off, w=w: (b, 0, off // w))
    kern = functools.partial(_mlstm_step_kernel, dqk=dqk, dv=dv)
    return pl.pallas_call(
        kern,
        grid=(N,),
        in_specs=[
            row(WQ, qo), row(WQ, ko), row(W, vo), row(W, oo), row(W, zo),
            pl.BlockSpec((None, 1, GATE_LANES), lambda b: (b, 0, 0)),
            pl.BlockSpec((None, 1, ML_HEADS), lambda b: (b, 0, 0)),
            pl.BlockSpec((1, W), lambda b: (0, 0)),
            pl.BlockSpec((None, ML_HEADS, dv, dqk), lambda b: (b, 0, 0, 0)),
            pl.BlockSpec((None, ML_HEADS, dqk), lambda b: (b, 0, 0)),
        ],
        out_specs=[
            pl.BlockSpec((None, 1, W), lambda b: (b, 0, 0)),
            pl.BlockSpec((None, ML_HEADS, dv, dqk), lambda b: (b, 0, 0, 0)),
            pl.BlockSpec((None, ML_HEADS, dqk), lambda b: (b, 0, 0)),
            pl.BlockSpec((None, 1, ML_HEADS), lambda b: (b, 0, 0)),
        ],
        out_shape=[
            jax.ShapeDtypeStruct((N, 1, W), BF16),
            jax.ShapeDtypeStruct((N, ML_HEADS, dv, dqk), F32),
            jax.ShapeDtypeStruct((N, ML_HEADS, dqk), F32),
            jax.ShapeDtypeStruct((N, 1, ML_HEADS), F32),
        ],
        compiler_params=_params("parallel"),
        name="mlstm_step",
    )(hs3, hs3, hs3, hs3, hs3, gs3, m0, out_gain.reshape(1, W), c0, n0)


N_AUG = 128


def _qk_norm_kernel(q_ref, k_ref, qg_ref, kg_ref, qn_ref, kn_ref, *, dh):
    nh = q_ref.shape[1] // dh
    for h in range(nh):
        sl = slice(h * dh, (h + 1) * dh)
        qn_ref[:, sl] = _rms_rows(q_ref[:, sl], qg_ref[...]).astype(qn_ref.dtype)
        kn_ref[:, sl] = _rms_rows(k_ref[:, sl], kg_ref[...])


def _qk_norm(h3, q_gain, k_gain, cols, *, width, dh, tq):
    B, T, _ = h3.shape
    tq = min(tq, T)
    spec = lambda off: pl.BlockSpec((None, tq, width), lambda b, i, off=off: (b, i, off // width))
    out = pl.BlockSpec((None, tq, width), lambda b, i: (b, i, 0))
    gain = pl.BlockSpec((1, dh), lambda b, i: (0, 0))
    return pl.pallas_call(
        functools.partial(_qk_norm_kernel, dh=dh),
        grid=(B, T // tq),
        in_specs=[spec(cols["fq"]), spec(cols["fk"]), gain, gain],
        out_specs=[out, out],
        out_shape=[jax.ShapeDtypeStruct((B, T, width), BF16), jax.ShapeDtypeStruct((B, T, width), F32)],
        compiler_params=_params("parallel", "parallel"),
        name="qk_norm",
    )(h3, h3, q_gain.reshape(1, dh), k_gain.reshape(1, dh))


def _fox_prep_kernel(q_ref, k_ref, v_ref, g_ref, qg_ref, kg_ref,
                     qa_ref, ka_ref, vb_ref, kn_ref, vf_ref, carry_sc, *, dh, qscale):
    tp = q_ref.shape[0]
    nh = q_ref.shape[1] // dh

    @pl.when(pl.program_id(1) == 0)
    def _():
        carry_sc[...] = jnp.zeros_like(carry_sc)

    row = lax.broadcasted_iota(jnp.int32, (tp, tp), 0)
    col = lax.broadcasted_iota(jnp.int32, (tp, tp), 1)
    tri = jnp.where(col <= row, 1.0, 0.0).astype(BF16)
    cum = _dot_sel_left(tri, g_ref[...]) + carry_sc[...]
    carry_sc[...] = cum[tp - 1:tp, :]

    v = v_ref[...]
    vf_ref[...] = v
    vb_ref[...] = v.astype(BF16)
    lane = lax.broadcasted_iota(jnp.int32, (tp, N_AUG), 1)
    off = 2 * ML_HEADS
    for h in range(nh):
        sl = slice(h * dh, (h + 1) * dh)
        kn = _rms_rows(k_ref[:, sl], kg_ref[...])
        kn_ref[:, sl] = kn
        qn = _rms_rows(q_ref[:, sl], qg_ref[...]) * qscale
        f = cum[:, off + h:off + h + 1] * LOG2E
        f1 = f.astype(BF16).astype(F32)
        r1 = f - f1
        f2 = r1.astype(BF16).astype(F32)
        f3 = r1 - f2
        eq = jnp.where(lane < 3, 1.0, jnp.where(lane == 3, f1, jnp.where(lane == 4, f2,
                       jnp.where(lane == 5, f3, 0.0))))
        ek = jnp.where(lane == 0, -f1, jnp.where(lane == 1, -f2, jnp.where(lane == 2, -f3,
                       jnp.where(lane < 6, 1.0, 0.0))))
        base = h * (dh + N_AUG)
        qa_ref[:, base:base + dh] = qn.astype(BF16)
        qa_ref[:, base + dh:base + dh + N_AUG] = eq.astype(BF16)
        ka_ref[:, base:base + dh] = kn.astype(BF16)
        ka_ref[:, base + dh:base + dh + N_AUG] = ek.astype(BF16)


def _fox_prep(h3, g3, q_gain, k_gain, cols, *, width, dh, tp):
    B, T, _ = h3.shape
    tp = min(tp, T)
    nh = width // dh
    wa = nh * (dh + N_AUG)
    spec = lambda off: pl.BlockSpec((None, tp, width), lambda b, i, off=off: (b, i, off // width))
    out = lambda w: pl.BlockSpec((None, tp, w), lambda b, i: (b, i, 0))
    gain = pl.BlockSpec((1, dh), lambda b, i: (0, 0))
    return pl.pallas_call(
        functools.partial(_fox_prep_kernel, dh=dh, qscale=dh ** -0.5 * LOG2E),
        grid=(B, T // tp),
        in_specs=[spec(cols["fq"]), spec(cols["fk"]), spec(cols["fv"]),
                  pl.BlockSpec((None, tp, GATE_LANES), lambda b, i: (b, i, 0)), gain, gain],
        out_specs=[out(wa), out(wa), out(width), out(width), out(width)],
        out_shape=[jax.ShapeDtypeStruct((B, T, wa), BF16), jax.ShapeDtypeStruct((B, T, wa), BF16),
                   jax.ShapeDtypeStruct((B, T, width), BF16),
                   jax.ShapeDtypeStruct((B, T, width), F32), jax.ShapeDtypeStruct((B, T, width), F32)],
        scratch_shapes=[pltpu.VMEM((1, GATE_LANES), F32)],
        compiler_params=_params("parallel", "arbitrary"),
        name="fox_prep",
    )(h3, h3, h3, g3, q_gain.reshape(1, dh), k_gain.reshape(1, dh))


def _lane_tiles_reduce(x, op):
    acc = x[:, :128]
    for j in range(1, x.shape[1] // 128):
        acc = op(acc, x[:, j * 128:(j + 1) * 128])
    return acc


def _fox_flash_kernel(q_ref, k_ref, v_ref, z_ref, o_ref, m_sc, l_sc, acc_sc, mx_sc, s_sc, p_sc, *, sub):
    qi = pl.program_id(2)
    ki = pl.program_id(3)
    t = q_ref.shape[0]
    ns = t // sub

    @pl.when(ki == 0)
    def _():
        m_sc[...] = jnp.full_like(m_sc, -jnp.inf)
        l_sc[...] = jnp.zeros_like(l_sc)
        acc_sc[...] = jnp.zeros_like(acc_sc)

    def block(diag):
        row_l = lax.broadcasted_iota(jnp.int32, (sub, sub), 0)
        col_l = lax.broadcasted_iota(jnp.int32, (sub, sub), 1)
        n_chunks = lambda i: i + 1 if diag else ns
        for i in range(ns):
            rows = slice(i * sub, (i + 1) * sub)
            q_i = q_ref[rows, :]
            mx = None
            for c in range(n_chunks(i)):
                cs = slice(c * sub, (c + 1) * sub)
                s = lax.dot_general(q_i, k_ref[cs, :], NT_DIMS, preferred_element_type=F32)
                if diag and c == i:
                    s = jnp.where(col_l <= row_l, s, -jnp.inf)
                s_sc[rows, cs] = s
                cm = _lane_tiles_reduce(s, jnp.maximum)
                mx = cm if mx is None else jnp.maximum(mx, cm)
            mx_sc[rows, :] = mx
        for i in range(ns):
            rows = slice(i * sub, (i + 1) * sub)
            m_prev = m_sc[rows, :]
            m_new = jnp.maximum(m_prev, jnp.max(mx_sc[rows, :], axis=1, keepdims=True))
            alpha = jnp.exp2(m_prev - m_new)
            ls = None
            for c in range(n_chunks(i)):
                cs = slice(c * sub, (c + 1) * sub)
                p = jnp.exp2(s_sc[rows, cs] - m_new)
                p_sc[rows, cs] = p.astype(BF16)
                lp = _lane_tiles_reduce(p, jnp.add)
                ls = lp if ls is None else ls + lp
            w = n_chunks(i) * sub
            l_sc[rows, :] = alpha * l_sc[rows, :] + jnp.sum(ls, axis=1, keepdims=True)
            acc_sc[rows, :] = alpha * acc_sc[rows, :] + jnp.dot(p_sc[rows, :w], v_ref[:w, :],
                                                                preferred_element_type=F32)
            m_sc[rows, :] = m_new

    @pl.when(ki < qi)
    def _():
        block(False)

    @pl.when(ki == qi)
    def _():
        block(True)
        zz = z_ref[...]
        o_ref[...] = (zz * _sigmoid(zz) * (acc_sc[...] / l_sc[...])).astype(o_ref.dtype)


def _fox_flash(qa, ka, vb, h3, cols, *, dh, t, sub):
    B, T, W = vb.shape
    t = min(t, T)
    sub = min(sub, t)
    nq = T // t
    da = dh + N_AUG
    zo = cols["fz"] // dh
    return pl.pallas_call(
        functools.partial(_fox_flash_kernel, sub=sub),
        grid=(B, W // dh, nq, nq),
        in_specs=[
            pl.BlockSpec((None, t, da), lambda b, h, i, j: (b, i, h)),
            pl.BlockSpec((None, t, da), lambda b, h, i, j: (b, jnp.minimum(i, j), h)),
            pl.BlockSpec((None, t, dh), lambda b, h, i, j: (b, jnp.minimum(i, j), h)),
            pl.BlockSpec((None, t, dh), lambda b, h, i, j: (b, i, zo + h)),
        ],
        out_specs=pl.BlockSpec((None, t, dh), lambda b, h, i, j: (b, i, h)),
        out_shape=jax.ShapeDtypeStruct((B, T, W), BF16),
        scratch_shapes=[pltpu.VMEM((t, 1), F32), pltpu.VMEM((t, 1), F32), pltpu.VMEM((t, dh), F32),
                        pltpu.VMEM((t, 128), F32), pltpu.VMEM((t, t), F32), pltpu.VMEM((t, t), BF16)],
        compiler_params=_params("parallel", "parallel", "parallel", "arbitrary"),
        name="fox_flash",
    )(qa, ka, vb, h3)


def _decay_selector(P, nh):
    t_src = jnp.arange(P)[:, None]
    t_dst = jnp.arange(P * nh)[None, :] // nh
    later = (t_src > t_dst).astype(BF16)
    return jnp.concatenate([later, jnp.ones((P, 128), BF16)], axis=1)


def _fox_decode_kernel(pt_ref, q_ref, kn_ref, vn_ref, z_ref, lfn_ref, u_ref, *rest, scale, nh, G):
    k_refs, v_refs, lf_refs = rest[:G], rest[G:2 * G], rest[2 * G:3 * G]
    o_ref, m_sc, l_sc, acc_sc, carry_sc = rest[3 * G:]
    p = pl.program_id(1)
    q = q_ref[...]

    @pl.when(p == 0)
    def _():
        s_new = jnp.sum(q.astype(F32) * kn_ref[...], axis=1, keepdims=True) * scale
        m_sc[...] = s_new
        l_sc[...] = jnp.ones_like(l_sc)
        acc_sc[...] = vn_ref[...]
        carry_sc[...] = lfn_ref[...]

    P = k_refs[0].shape[0]
    n = P * nh
    sub = lax.broadcasted_iota(jnp.int32, (nh, n), 0)
    lane = lax.broadcasted_iota(jnp.int32, (nh, n), 1)
    diag = (lane & (nh - 1)) == sub
    carry = carry_sc[...]
    scores = []
    for g in range(G):
        k2 = k_refs[g][...].reshape(n, k_refs[g].shape[2]).astype(BF16)
        s = lax.dot_general(q, k2, NT_DIMS, preferred_element_type=F32) * scale
        l1, l2, l3 = _split3(lf_refs[g][...])
        zz = jnp.dot(jnp.concatenate([l1, l2, l3], axis=0), u_ref[...], preferred_element_type=F32)
        zs = zz[0:nh] + zz[nh:2 * nh] + zz[2 * nh:3 * nh]
        s = s + zs[:, :n] + jnp.tile(carry, (1, n // 128))
        carry = carry + zs[:, n:]
        scores.append(jnp.where(diag, s, -jnp.inf))
    carry_sc[...] = carry

    m_prev = m_sc[...]
    m_new = m_prev
    for s in scores:
        m_new = jnp.maximum(m_new, jnp.max(s, axis=1, keepdims=True))
    alpha = jnp.exp(m_prev - m_new)
    l_new = alpha * l_sc[...]
    acc = alpha * acc_sc[...]
    for g in range(G):
        pr = jnp.exp(scores[g] - m_new)
        v2 = v_refs[g][...].reshape(n, v_refs[g].shape[2]).astype(BF16)
        l_new = l_new + jnp.sum(pr, axis=1, keepdims=True)
        acc = acc + jnp.dot(pr.astype(BF16), v2, preferred_element_type=F32)
    l_sc[...] = l_new
    acc_sc[...] = acc
    m_sc[...] = m_new

    @pl.when(p == pl.num_programs(1) - 1)
    def _():
        zz = z_ref[...]
        o_ref[...] = zz * _sigmoid(zz) * (acc / l_new)


def _fox_decode(page_table, q3, kn3, vn3, z3, lfn3, k_pool, v_pool, lf_pool_t, *, dh, G):
    N, nh, _ = q3.shape
    n_pages = page_table.shape[1]
    P = k_pool.shape[1]
    G = math.gcd(G, n_pages)
    assert nh & (nh - 1) == 0
    u = _decay_selector(P, nh)
    per = lambda: pl.BlockSpec((None, nh, dh), lambda b, p, pt: (b, 0, 0))
    page = lambda g: (lambda b, p, pt, g=g: (pt[b, n_pages - 1 - (p * G + g)], 0, 0, 0))
    page3 = lambda g: (lambda b, p, pt, g=g: (pt[b, n_pages - 1 - (p * G + g)], 0, 0))
    grid_spec = pltpu.PrefetchScalarGridSpec(
        num_scalar_prefetch=1,
        grid=(N, n_pages // G),
        in_specs=[per(), per(), per(), per(),
                  pl.BlockSpec((None, nh, 128), lambda b, p, pt: (b, 0, 0)),
                  pl.BlockSpec(u.shape, lambda b, p, pt: (0, 0))]
                 + [pl.BlockSpec((None, P, nh, dh), page(g)) for g in range(G)]
                 + [pl.BlockSpec((None, P, nh, dh), page(g)) for g in range(G)]
                 + [pl.BlockSpec((None, nh, P), page3(g)) for g in range(G)],
        out_specs=pl.BlockSpec((None, nh, dh), lambda b, p, pt: (b, 0, 0)),
        scratch_shapes=[pltpu.VMEM((nh, 1), F32), pltpu.VMEM((nh, 1), F32), pltpu.VMEM((nh, dh), F32),
                        pltpu.VMEM((nh, 128), F32)],
    )
    return pl.pallas_call(
        functools.partial(_fox_decode_kernel, scale=dh ** -0.5, nh=nh, G=G),
        grid_spec=grid_spec,
        out_shape=jax.ShapeDtypeStruct((N, nh, dh), F32),
        compiler_params=_params("parallel", "arbitrary"),
        name="fox_decode",
    )(page_table, q3, kn3, vn3, z3, lfn3, u, *([k_pool] * G), *([v_pool] * G), *([lf_pool_t] * G))


def _mem_attn_kernel(q_ref, z_ref, k_ref, v_ref, qg_ref, a_ref, *, dh, scale):
    nh = q_ref.shape[1] // dh
    for h in range(nh):
        sl = slice(h * dh, (h + 1) * dh)
        q = _rms_rows(q_ref[:, sl], qg_ref[...]).astype(BF16)
        s = lax.dot_general(q, k_ref[:, sl].astype(BF16), NT_DIMS, preferred_element_type=F32) * scale
        e = jnp.exp(s - jnp.max(s, axis=1, keepdims=True))
        o = jnp.dot(e.astype(BF16), v_ref[:, sl].astype(BF16), preferred_element_type=F32)
        o = o / jnp.sum(e, axis=1, keepdims=True)
        zz = z_ref[:, sl]
        a_ref[:, sl] = (zz * _sigmoid(zz) * o).astype(a_ref.dtype)


def _mem_attn_prompt(h3, mk, mv, q_gain, cols, *, width, dh, tq):
    B, T, _ = h3.shape
    n_mem = mk.shape[1]
    tq = min(tq, T)
    spec = lambda off: pl.BlockSpec((None, tq, width), lambda b, i, off=off: (b, i, off // width))
    mem = pl.BlockSpec((None, n_mem, width), lambda b, i: (b, 0, 0))
    return pl.pallas_call(
        functools.partial(_mem_attn_kernel, dh=dh, scale=dh ** -0.5),
        grid=(B, T // tq),
        in_specs=[spec(cols["cq"]), spec(cols["cz"]), mem, mem, pl.BlockSpec((1, dh), lambda b, i: (0, 0))],
        out_specs=pl.BlockSpec((None, tq, width), lambda b, i: (b, i, 0)),
        out_shape=jax.ShapeDtypeStruct((B, T, width), BF16),
        compiler_params=_params("parallel", "parallel"),
        name="mem_attn",
    )(h3, h3, mk, mv, q_gain.reshape(1, dh))


def _mem_decode_kernel(q_ref, z_ref, k_ref, v_ref, qg_ref, o_ref, *, scale, nh):
    q = _rms_rows(q_ref[...], qg_ref[...]).astype(BF16)
    kp = k_ref[...]
    vp = v_ref[...]
    rows = kp.shape[0] * kp.shape[1]
    k2 = kp.reshape(rows, kp.shape[2]).astype(BF16)
    v2 = vp.reshape(rows, vp.shape[2]).astype(BF16)
    s = lax.dot_general(q, k2, NT_DIMS, preferred_element_type=F32) * scale
    sub = lax.broadcasted_iota(jnp.int32, s.shape, 0)
    lane = lax.broadcasted_iota(jnp.int32, s.shape, 1)
    s = jnp.where((lane & (nh - 1)) == sub, s, -jnp.inf)
    m = jnp.maximum(jnp.max(s, axis=1, keepdims=True), -1e30)
    e = jnp.exp(s - m)
    den = jnp.maximum(jnp.sum(e, axis=1, keepdims=True), 1e-30)
    o = jnp.dot(e.astype(BF16), v2, preferred_element_type=F32) / den
    zz = z_ref[...]
    o_ref[...] = zz * _sigmoid(zz) * o


def _mem_decode(q3, z3, k_cache, v_cache, q_gain, *, dh):
    N, n_mem, nh, _ = k_cache.shape
    assert nh & (nh - 1) == 0
    per = pl.BlockSpec((None, 8, dh), lambda b: (b, 0, 0))
    cache = pl.BlockSpec((None, n_mem, nh, dh), lambda b: (b, 0, 0, 0))
    return pl.pallas_call(
        functools.partial(_mem_decode_kernel, scale=dh ** -0.5, nh=nh),
        grid=(N,),
        in_specs=[per, per, cache, cache, pl.BlockSpec((1, dh), lambda b: (0, 0))],
        out_specs=per,
        out_shape=jax.ShapeDtypeStruct((N, 8, dh), F32),
        compiler_params=_params("parallel"),
        name="mem_decode",
    )(q3, z3, k_cache, v_cache, q_gain.reshape(1, dh))


def _merge_kernel(am_ref, af_ref, ac_ref, wm_ref, wf_ref, wc_ref, gm_ref, gf_ref, gc_ref, u_ref):
    d = lambda a, w: jnp.dot(a[...], w[...], preferred_element_type=F32)
    u = _sigmoid(gm_ref[...]) * d(am_ref, wm_ref)
    u = u + _sigmoid(gf_ref[...]) * d(af_ref, wf_ref)
    u = u + _sigmoid(gc_ref[...]) * d(ac_ref, wc_ref)
    u_ref[...] = u.astype(u_ref.dtype)


def _merge(am, af, ac, wm, wf, wc, h2, g_off, *, tm, tn):
    m, k = am.shape
    n = wm.shape[1]
    tm = min(tm, m)
    a_spec = pl.BlockSpec((tm, k), lambda i, j: (i, 0))
    w_spec = pl.BlockSpec((k, tn), lambda i, j: (0, j))
    g_spec = lambda b: pl.BlockSpec((tm, tn), lambda i, j, b=b: (i, (g_off + b * n) // tn + j))
    return pl.pallas_call(
        _merge_kernel,
        grid=(m // tm, n // tn),
        in_specs=[a_spec, a_spec, a_spec, w_spec, w_spec, w_spec, g_spec(0), g_spec(1), g_spec(2)],
        out_specs=pl.BlockSpec((tm, tn), lambda i, j: (i, j)),
        out_shape=jax.ShapeDtypeStruct((m, n), BF16),
        compiler_params=_params("parallel", "arbitrary"),
        name="merge",
    )(am, af, ac, wm, wf, wc, h2, h2, h2)


def _out_proj_kernel(u_ref, w_ref, x_ref, y_ref):
    y_ref[...] = x_ref[...] + jnp.dot(u_ref[...], w_ref[...], preferred_element_type=F32)


def _out_proj(u, w, x, *, tm, tn):
    m, k = u.shape
    n = w.shape[1]
    tm = min(tm, m)
    return pl.pallas_call(
        _out_proj_kernel,
        grid=(m // tm, n // tn),
        in_specs=[
            pl.BlockSpec((tm, k), lambda i, j: (i, 0)),
            pl.BlockSpec((k, tn), lambda i, j: (0, j)),
            pl.BlockSpec((tm, tn), lambda i, j: (i, j)),
        ],
        out_specs=pl.BlockSpec((tm, tn), lambda i, j: (i, j)),
        out_shape=jax.ShapeDtypeStruct((m, n), F32),
        compiler_params=_params("parallel", "arbitrary"),
        name="out_proj",
    )(u, w, x)


def _column_layout(d_model):
    ml_dv = d_model // ML_HEADS
    ml_qkw = ML_HEADS * (ml_dv // 2)
    names = ("mq", "mk", "mv", "mi", "mf", "mo", "mz", "fq", "fk", "fv", "ff", "fz", "cq", "cz", "g")
    widths = (ml_qkw, ml_qkw, d_model, ML_HEADS, ML_HEADS, d_model, d_model,
              d_model, d_model, d_model, FOX_HEADS, d_model, d_model, d_model, N_BRANCH * d_model)
    src, off = {}, 0
    for n, w in zip(names, widths):
        src[n] = (off, w)
        off += w
    big, boff = {}, 0
    for n in names:
        if n in ("mi", "mf", "ff"):
            continue
        big[n] = boff
        boff += src[n][1]
    return src, big, boff


def _layer(x_p, x_s, mem_p, fox_k_pool, fox_v_pool, fox_lf_pool, page_table, mem_k_s, mem_v_s,
           c0_s, n0_s, m0_s, norm_in, norm_mem, w_in, w_mem_kv, b_i, b_f, b_ff, ml_out_norm,
           fq_norm, fk_norm, cq_norm, ck_norm, w_br_m, w_br_f, w_br_c, w_out):
    B, T, D = x_p.shape
    N = x_s.shape[0]
    dv = D // ML_HEADS
    dqk = dv // 2
    fox_dh = D // FOX_HEADS
    mem_dh = D // MEM_HEADS
    src, cols, nbig = _column_layout(D)
    w_in_t = w_in.T
    gap1, gap2 = cols["mo"], cols["fz"]
    skip1, skip2 = 2 * ML_HEADS, 2 * ML_HEADS + FOX_HEADS

    def big_rows(tn):
        assert gap1 % tn == 0 and gap2 % tn == 0
        return lambda j: j * tn + jnp.where(j * tn >= gap2, skip2, jnp.where(j * tn >= gap1, skip1, 0))

    rows = lambda n: w_in_t[src[n][0]:src[n][0] + src[n][1]]
    n_gate = 2 * ML_HEADS + FOX_HEADS
    w_gate_t = jnp.concatenate([rows("mi"), rows("mf"), rows("ff"),
                                jnp.zeros((GATE_LANES - n_gate, D), F32)], axis=0)
    b_gate = jnp.concatenate([b_i, b_f, b_ff, jnp.zeros((GATE_LANES - n_gate,), F32)]).reshape(1, GATE_LANES)
    wm, wf, wc, wo = (w.astype(BF16) for w in (w_br_m, w_br_f, w_br_c, w_out))

    xp2 = x_p.reshape(B * T, D)
    xs2 = x_s.reshape(N, D)

    h2 = _norm_proj(xp2, norm_in, w_in_t, tm=1024, tn=1024, n_out=nbig, w_index=big_rows(1024), transposed=True)
    g2 = _gates(xp2, norm_in, w_gate_t, b_gate, tm=1024)
    h3 = h2.reshape(B, T, nbig)
    g3 = g2.reshape(B, T, GATE_LANES)

    n_mem = mem_p.shape[1]
    mem2 = mem_p.reshape(B * n_mem, D)
    heads_k = D // mem_dh
    mk_p = _norm_proj(mem2, norm_mem, w_mem_kv, tm=512, tn=mem_dh, n_out=D, w_index=lambda j: j,
                      transposed=False, head_gain=ck_norm)
    mv_p = _norm_proj(mem2, norm_mem, w_mem_kv, tm=512, tn=mem_dh, n_out=D, w_index=lambda j: heads_k + j,
                      transposed=False)

    a_m, c_p, n_p, m_p = _mlstm_prompt(h3, g3, ml_out_norm, cols, dqk=dqk, dv=dv)

    qa, ka, vb, kn, fv_p = _fox_prep(h3, g3, fq_norm, fk_norm, cols, width=D, dh=fox_dh, tp=256)
    a_f = _fox_flash(qa, ka, vb, h3, cols, dh=fox_dh, t=1024, sub=256)

    a_c = _mem_attn_prompt(h3, mk_p.reshape(B, n_mem, D), mv_p.reshape(B, n_mem, D), cq_norm, cols,
                           width=D, dh=mem_dh, tq=512)

    u_p = _merge(a_m.reshape(B * T, D), a_f.reshape(B * T, D), a_c.reshape(B * T, D),
                 wm, wf, wc, h2, cols["g"], tm=1024, tn=256)
    y_p = _out_proj(u_p, wo, xp2, tm=1024, tn=1024).reshape(B, T, D)

    hs2 = _norm_proj(xs2, norm_in, w_in_t, tm=128, tn=1024, n_out=nbig, w_index=big_rows(1024), transposed=True)
    gs2 = _gates(xs2, norm_in, w_gate_t, b_gate, tm=128)
    hs3 = hs2.reshape(N, 1, nbig)
    gs3 = gs2.reshape(N, 1, GATE_LANES)

    a_ms, c_s, n_s, m_s = _mlstm_sample(hs3, gs3, m0_s.reshape(N, 1, ML_HEADS), ml_out_norm, c0_s, n0_s,
                                        cols, dqk=dqk, dv=dv)

    hs1 = hs2.reshape(1, N, nbig)
    qn_s, kn_s = _qk_norm(hs1, fq_norm, fk_norm, cols, width=D, dh=fox_dh, tq=128)
    seg = lambda n: hs2[:, cols[n]:cols[n] + D]
    fv_s = seg("fv")
    flf_s = gs2[:, 2 * ML_HEADS:n_gate]
    lfn = jnp.broadcast_to(flf_s[:, :, None], (N, FOX_HEADS, 128))
    to_heads = lambda a: a.reshape(N, FOX_HEADS, fox_dh)
    o_f = _fox_decode(page_table, to_heads(qn_s.reshape(N, D)), to_heads(kn_s.reshape(N, D)), to_heads(fv_s),
                      to_heads(seg("fz")), lfn, fox_k_pool, fox_v_pool, jnp.swapaxes(fox_lf_pool, 1, 2), dh=fox_dh, G=4)
    a_fs = o_f.reshape(N, D).astype(BF16)

    pad8 = lambda a: jnp.pad(a.reshape(N, MEM_HEADS, mem_dh), ((0, 0), (0, 8 - MEM_HEADS), (0, 0)))
    o_c = _mem_decode(pad8(seg("cq")), pad8(seg("cz")), mem_k_s, mem_v_s, cq_norm, dh=mem_dh)
    a_cs = o_c[:, :MEM_HEADS].reshape(N, D).astype(BF16)

    u_s = _merge(a_ms.reshape(N, D), a_fs, a_cs, wm, wf, wc, hs2, cols["g"], tm=128, tn=512)
    y_s = _out_proj(u_s, wo, xs2, tm=128, tn=1024).reshape(N, 1, D)

    outs_p = (kn.reshape(B, T, FOX_HEADS, fox_dh), fv_p.reshape(B, T, FOX_HEADS, fox_dh),
              g3[:, :, 2 * ML_HEADS:n_gate],
              mk_p.reshape(B, n_mem, MEM_HEADS, mem_dh), mv_p.reshape(B, n_mem, MEM_HEADS, mem_dh),
              c_p, n_p.reshape(B, ML_HEADS, dqk), m_p.reshape(B, ML_HEADS))
    outs_s = (kn_s.reshape(N, 1, FOX_HEADS, fox_dh), fv_s.reshape(N, 1, FOX_HEADS, fox_dh),
              flf_s.reshape(N, 1, FOX_HEADS), c_s, n_s, m_s.reshape(N, ML_HEADS))
    return y_p, y_s, outs_p, outs_s


def kernel(x_prompt, x_sample, mem_prompt, cache_fox_k, cache_fox_v, cache_fox_logf, page_table, cache_mem_k, cache_mem_v, state_mlstm_C, state_mlstm_n, state_mlstm_m, norm_in, norm_mem, w_in, w_mem_kv, b_mlstm_i, b_mlstm_f, b_fox_f, mlstm_out_norm, fox_q_norm, fox_k_norm, mem_q_norm, mem_k_norm, w_br_mlstm, w_br_fox, w_br_mem, w_out):
    depth = w_in.shape[0]
    y_p, y_s = x_prompt, x_sample
    per_layer = []
    for l in range(depth):
        y_p, y_s, outs_p, outs_s = _layer(
            y_p, y_s, mem_prompt, cache_fox_k[l], cache_fox_v[l], cache_fox_logf[l], page_table,
            cache_mem_k[l], cache_mem_v[l], state_mlstm_C[l], state_mlstm_n[l], state_mlstm_m[l],
            norm_in[l], norm_mem[l], w_in[l], w_mem_kv[l], b_mlstm_i[l], b_mlstm_f[l], b_fox_f[l],
            mlstm_out_norm[l], fox_q_norm[l], fox_k_norm[l], mem_q_norm[l], mem_k_norm[l],
            w_br_mlstm[l], w_br_fox[l], w_br_mem[l], w_out[l])
        per_layer.append(outs_p + outs_s)
    stacked = tuple(jnp.stack([lay[i] for lay in per_layer]) for i in range(len(per_layer[0])))
    return (y_p, y_s) + stacked
```

```python
import functools
import math

import jax
import jax.numpy as jnp
from jax import lax
from jax.experimental import pallas as pl
from jax.experimental.pallas import tpu as pltpu

F32 = jnp.float32
BF16 = jnp.bfloat16

ML_HEADS = 4
FOX_HEADS = 16
MEM_HEADS = 4
N_BRANCH = 3
ML_CHUNK = 128
EPS = 1e-6
LOG2E = 1.4426950408889634
GATE_LANES = 128
VMEM_LIMIT = 56 * 1024 * 1024

NT_DIMS = (((1,), (1,)), ((), ()))
TN_DIMS = (((0,), (0,)), ((), ()))


def _params(*sem):
    return pltpu.CompilerParams(dimension_semantics=sem, vmem_limit_bytes=VMEM_LIMIT)


def _sigmoid(z):
    return 1.0 / (1.0 + jnp.exp(-z))


def _log_sigmoid(z):
    return jnp.minimum(z, 0.0) - jnp.log1p(jnp.exp(-jnp.abs(z)))


def _split3(x):
    x1 = x.astype(BF16)
    r1 = x - x1.astype(F32)
    x2 = r1.astype(BF16)
    r2 = r1 - x2.astype(F32)
    return x1, x2, r2.astype(BF16)


def _dot_sel_left(sel, x):
    x1, x2, x3 = _split3(x)
    d = lambda p: jnp.dot(sel, p, preferred_element_type=F32)
    return d(x1) + d(x2) + d(x3)


def _rms_rows(x, gain):
    return x * lax.rsqrt(jnp.mean(x * x, axis=-1, keepdims=True) + EPS) * gain


def _norm_proj_kernel(x_ref, g_ref, w_ref, *rest, head_norm, transposed):
    if head_norm:
        hg_ref, o_ref, xn_ref = rest
    else:
        o_ref, xn_ref = rest

    @pl.when(pl.program_id(1) == 0)
    def _():
        xn_ref[...] = _rms_rows(x_ref[...], g_ref[...]).astype(BF16)

    w = w_ref[...].astype(BF16)
    if transposed:
        acc = lax.dot_general(xn_ref[...], w, NT_DIMS, preferred_element_type=F32)
    else:
        acc = jnp.dot(xn_ref[...], w, preferred_element_type=F32)
    if head_norm:
        acc = _rms_rows(acc, hg_ref[...])
    o_ref[...] = acc.astype(o_ref.dtype)


def _norm_proj(x, gain, w, *, tm, tn, n_out, w_index, transposed, head_gain=None, out_dtype=F32):
    m, k = x.shape
    n = n_out
    tm = min(tm, m)
    assert m % tm == 0 and n % tn == 0
    if transposed:
        w_spec = pl.BlockSpec((pl.Element(tn), pl.Element(k)), lambda i, j: (pl.multiple_of(w_index(j), 8), 0))
    else:
        w_spec = pl.BlockSpec((k, tn), lambda i, j: (0, w_index(j)))
    in_specs = [
        pl.BlockSpec((tm, k), lambda i, j: (i, 0)),
        pl.BlockSpec((1, k), lambda i, j: (0, 0)),
        w_spec,
    ]
    args = [x, gain.reshape(1, k), w]
    if head_gain is not None:
        assert head_gain.shape == (tn,)
        in_specs.append(pl.BlockSpec((1, tn), lambda i, j: (0, 0)))
        args.append(head_gain.reshape(1, tn))
    return pl.pallas_call(
        functools.partial(_norm_proj_kernel, head_norm=head_gain is not None, transposed=transposed),
        grid=(m // tm, n // tn),
        in_specs=in_specs,
        out_specs=pl.BlockSpec((tm, tn), lambda i, j: (i, j)),
        out_shape=jax.ShapeDtypeStruct((m, n), out_dtype),
        scratch_shapes=[pltpu.VMEM((tm, k), BF16)],
        compiler_params=_params("parallel", "arbitrary"),
        name="norm_proj",
    )(*args)


def _gates_kernel(x_ref, g_ref, w_ref, b_ref, o_ref):
    xn = _rms_rows(x_ref[...], g_ref[...])
    x1, x2, _ = _split3(xn)
    w1, w2, _ = _split3(w_ref[...])
    d = lambda a, b: lax.dot_general(a, b, NT_DIMS, preferred_element_type=F32)
    z = d(x1, w1) + d(x1, w2) + d(x2, w1) + b_ref[...]
    lane = lax.broadcasted_iota(jnp.int32, z.shape, 1)
    o_ref[...] = jnp.where(lane < ML_HEADS, z, _log_sigmoid(z))


def _gates(x, gain, w_gate, bias, *, tm):
    m, k = x.shape
    tm = min(tm, m)
    return pl.pallas_call(
        _gates_kernel,
        grid=(m // tm,),
        in_specs=[
            pl.BlockSpec((tm, k), lambda i: (i, 0)),
            pl.BlockSpec((1, k), lambda i: (0, 0)),
            pl.BlockSpec((GATE_LANES, k), lambda i: (0, 0)),
            pl.BlockSpec((1, GATE_LANES), lambda i: (0, 0)),
        ],
        out_specs=pl.BlockSpec((tm, GATE_LANES), lambda i: (i, 0)),
        out_shape=jax.ShapeDtypeStruct((m, GATE_LANES), F32),
        compiler_params=_params("parallel"),
        name="gates",
    )(x, gain.reshape(1, k), w_gate, bias)


def _mlstm_chunk_kernel(q_ref, k_ref, v_ref, o_ref, z_ref, g_ref, gain_ref,
                        a_ref, c_out, n_out, m_out, c_sc, n_sc, m_sc, *, dqk, L):
    h = pl.program_id(1)
    step = pl.program_id(2)
    n_chunks = q_ref.shape[0] // L

    @pl.when(step == 0)
    def _():
        c_sc[...] = jnp.zeros_like(c_sc)
        n_sc[...] = jnp.zeros_like(n_sc)
        m_sc[...] = jnp.zeros_like(m_sc)

    row = lax.broadcasted_iota(jnp.int32, (L, L), 0)
    col = lax.broadcasted_iota(jnp.int32, (L, L), 1)
    causal = col <= row
    tri = jnp.where(causal, 1.0, 0.0).astype(BF16)
    lane = lax.broadcasted_iota(jnp.int32, (L, GATE_LANES), 1)
    sub = lax.broadcasted_iota(jnp.int32, (GATE_LANES, L), 0)
    pick_col = lambda x, idx: jnp.sum(jnp.where(lane == idx, x, 0.0), axis=1, keepdims=True)
    pick_row = lambda x, idx: jnp.sum(jnp.where(sub == idx, x, 0.0), axis=0, keepdims=True)

    def chunk(ci, carry):
        rows = pl.ds(pl.multiple_of(ci * L, L), L)
        g = g_ref[rows, :]
        cum = _dot_sel_left(tri, g)
        g_t = g.T
        cum_t = cum.T
        ig_col = pick_col(g, h)
        b_col = pick_col(cum, ML_HEADS + h)
        ig_row = pick_row(g_t, h)
        b_row = pick_row(cum_t, ML_HEADS + h)

        m_prev = m_sc[...]
        a_col = b_col + m_prev
        dmat = jnp.where(causal, b_col - b_row + ig_row, -jnp.inf)
        m_t = jnp.maximum(a_col, jnp.max(dmat, axis=1, keepdims=True))
        w_inter = jnp.exp(a_col - m_t)
        w_intra = jnp.exp(dmat - m_t)

        qf = q_ref[rows, :] * (dqk ** -0.5)
        kf = k_ref[rows, :]
        vf = v_ref[rows, :]
        q = qf.astype(BF16)
        k = kf.astype(BF16)
        v = vf.astype(BF16)
        c_prev = c_sc[...]
        n_prev = n_sc[...]

        s = lax.dot_general(q, k, NT_DIMS, preferred_element_type=F32) * w_intra
        inter = lax.dot_general(q, c_prev.astype(BF16), NT_DIMS, preferred_element_type=F32)
        num = w_inter * inter + jnp.dot(s.astype(BF16), v, preferred_element_type=F32)
        nq = w_inter * jnp.sum(qf * n_prev, axis=1, keepdims=True) + jnp.sum(s, axis=1, keepdims=True)
        hid = num / jnp.maximum(jnp.abs(nq), jnp.exp(-m_t))

        b_last = b_col[L - 1:L, :]
        m_new = m_t[L - 1:L, :]
        w_c = jnp.exp(b_last + m_prev - m_new)
        w_s = jnp.exp(b_last - b_col + ig_col - m_new)
        wv = (w_s * vf).astype(BF16)
        c_sc[...] = w_c * c_prev + lax.dot_general(wv, k, TN_DIMS, preferred_element_type=F32)
        n_sc[...] = w_c * n_prev + jnp.sum(w_s * kf, axis=0, keepdims=True)
        m_sc[...] = m_new

        hn = _rms_rows(hid, gain_ref[...])
        zz = z_ref[rows, :]
        a_ref[rows, :] = (zz * _sigmoid(zz) * (_sigmoid(o_ref[rows, :]) * hn)).astype(a_ref.dtype)
        return carry

    lax.fori_loop(0, n_chunks, chunk, 0)

    @pl.when(step == pl.num_programs(2) - 1)
    def _():
        c_out[...] = c_sc[...]
        n_out[...] = n_sc[...]
        m_out[...] = m_sc[...]


def _mlstm_prompt(h3, g3, out_gain, cols, *, dqk, dv, chunks_per_step):
    B, T, _ = h3.shape
    L = math.gcd(T, ML_CHUNK)
    rows = L * math.gcd(T // L, chunks_per_step)
    qo, ko, vo, oo, zo = (cols[n] for n in ("mq", "mk", "mv", "mo", "mz"))
    blk = lambda w, off: pl.BlockSpec((None, rows, w), lambda b, h, c, off=off, w=w: (b, c, off // w + h))
    kern = functools.partial(_mlstm_chunk_kernel, dqk=dqk, L=L)
    return pl.pallas_call(
        kern,
        grid=(B, ML_HEADS, T // rows),
        in_specs=[
            blk(dqk, qo), blk(dqk, ko), blk(dv, vo), blk(dv, oo), blk(dv, zo),
            pl.BlockSpec((None, rows, GATE_LANES), lambda b, h, c: (b, c, 0)),
            pl.BlockSpec((1, dv), lambda b, h, c: (0, h)),
        ],
        out_specs=[
            pl.BlockSpec((None, rows, dv), lambda b, h, c: (b, c, h)),
            pl.BlockSpec((None, None, dv, dqk), lambda b, h, c: (b, h, 0, 0)),
            pl.BlockSpec((None, None, 1, dqk), lambda b, h, c: (b, h, 0, 0)),
            pl.BlockSpec((None, None, 1, 1), lambda b, h, c: (b, h, 0, 0)),
        ],
        out_shape=[
            jax.ShapeDtypeStruct((B, T, ML_HEADS * dv), BF16),
            jax.ShapeDtypeStruct((B, ML_HEADS, dv, dqk), F32),
            jax.ShapeDtypeStruct((B, ML_HEADS, 1, dqk), F32),
            jax.ShapeDtypeStruct((B, ML_HEADS, 1, 1), F32),
        ],
        scratch_shapes=[pltpu.VMEM((dv, dqk), F32), pltpu.VMEM((1, dqk), F32), pltpu.VMEM((1, 1), F32)],
        compiler_params=_params("parallel", "parallel", "arbitrary"),
        name="mlstm_chunk",
    )(h3, h3, h3, h3, h3, g3, out_gain.reshape(1, ML_HEADS * dv))


def _mlstm_step_kernel(q_ref, k_ref, v_ref, o_ref, z_ref, g_ref, m0_ref, gain_ref, c0_ref, n0_ref,
                       a_ref, c_out, n_out, m_out, *, dqk, dv):
    g = g_ref[...]
    m0 = m0_ref[...]
    lane_h = lax.broadcasted_iota(jnp.int32, (1, ML_HEADS), 1)
    m_acc = jnp.zeros((1, ML_HEADS), F32)
    for h in range(ML_HEADS):
        ig = g[:, h:h + 1]
        lf = g[:, ML_HEADS + h:ML_HEADS + h + 1]
        a = lf + m0[:, h:h + 1]
        m_t = jnp.maximum(a, ig)
        w_inter = jnp.exp(a - m_t)
        w_intra = jnp.exp(ig - m_t)
        qf = q_ref[:, h * dqk:(h + 1) * dqk] * (dqk ** -0.5)
        kf = k_ref[:, h * dqk:(h + 1) * dqk]
        vf = v_ref[:, h * dv:(h + 1) * dv]
        c0 = c0_ref[h]
        n0 = n0_ref[h:h + 1, :]
        s = jnp.sum(qf * kf, axis=1, keepdims=True) * w_intra
        q8 = jnp.broadcast_to(qf, (8, dqk)).astype(BF16)
        cq = lax.dot_general(q8, c0.astype(BF16), NT_DIMS, preferred_element_type=F32)[0:1, :]
        num = w_inter * cq + s * vf
        nq = w_inter * jnp.sum(n0 * qf, axis=1, keepdims=True) + s
        hid = num / jnp.maximum(jnp.abs(nq), jnp.exp(-m_t))
        first = lax.broadcasted_iota(jnp.int32, (16, 1), 0) == 0
        v16 = jnp.where(first, vf, 0.0).astype(BF16)
        k16 = jnp.where(first, w_intra * kf, 0.0).astype(BF16)
        c_out[h] = w_inter * c0 + lax.dot_general(v16, k16, TN_DIMS, preferred_element_type=F32)
        n_out[h:h + 1, :] = w_inter * n0 + w_intra * kf
        m_acc = m_acc + jnp.where(lane_h == h, m_t, 0.0)
        hn = _rms_rows(hid, gain_ref[:, h * dv:(h + 1) * dv])
        zz = z_ref[:, h * dv:(h + 1) * dv]
        gate_o = _sigmoid(o_ref[:, h * dv:(h + 1) * dv])
        a_ref[:, h * dv:(h + 1) * dv] = (zz * _sigmoid(zz) * (gate_o * hn)).astype(a_ref.dtype)
    m_out[...] = m_acc


def _mlstm_sample(hs3, gs3, m0, out_gain, c0, n0, cols, *, dqk, dv):
    N = hs3.shape[0]
    W = ML_HEADS * dv
    WQ = ML_HEADS * dqk
    qo, ko, vo, oo, zo = (cols[n] for n in ("mq", "mk", "mv", "mo", "mz"))
    row = lambda w, off: pl.BlockSpec((None, 1, w), lambda b, off=off, w=w: (b, 0, off // w))
    kern = functools.partial(_mlstm_step_kernel, dqk=dqk, dv=dv)
    return pl.pallas_call(
        kern,
        grid=(N,),
        in_specs=[
            row(WQ, qo), row(WQ, ko), row(W, vo), row(W, oo), row(W, zo),
            pl.BlockSpec((None, 1, GATE_LANES), lambda b: (b, 0, 0)),
            pl.BlockSpec((None, 1, ML_HEADS), lambda b: (b, 0, 0)),
            pl.BlockSpec((1, W), lambda b: (0, 0)),
            pl.BlockSpec((None, ML_HEADS, dv, dqk), lambda b: (b, 0, 0, 0)),
            pl.BlockSpec((None, ML_HEADS, dqk), lambda b: (b, 0, 0)),
        ],
        out_specs=[
            pl.BlockSpec((None, 1, W), lambda b: (b, 0, 0)),
            pl.BlockSpec((None, ML_HEADS, dv, dqk), lambda b: (b, 0, 0, 0)),
            pl.BlockSpec((None, ML_HEADS, dqk), lambda b: (b, 0, 0)),
            pl.BlockSpec((None, 1, ML_HEADS), lambda b: (b, 0, 0)),
        ],
        out_shape=[
            jax.ShapeDtypeStruct((N, 1, W), BF16),
            jax.ShapeDtypeStruct((N, ML_HEADS, dv, dqk), F32),
            jax.ShapeDtypeStruct((N, ML_HEADS, dqk), F32),
            jax.ShapeDtypeStruct((N, 1, ML_HEADS), F32),
        ],
        compiler_params=_params("parallel"),
        name="mlstm_step",
    )(hs3, hs3, hs3, hs3, hs3, gs3, m0, out_gain.reshape(1, W), c0, n0)


N_AUG = 128


def _qk_norm_kernel(q_ref, k_ref, qg_ref, kg_ref, qn_ref, kn_ref, *, dh):
    nh = q_ref.shape[1] // dh
    for h in range(nh):
        sl = slice(h * dh, (h + 1) * dh)
        qn_ref[:, sl] = _rms_rows(q_ref[:, sl], qg_ref[...]).astype(qn_ref.dtype)
        kn_ref[:, sl] = _rms_rows(k_ref[:, sl], kg_ref[...])


def _qk_norm(h3, q_gain, k_gain, cols, *, width, dh, tq):
    B, T, _ = h3.shape
    tq = min(tq, T)
    spec = lambda off: pl.BlockSpec((None, tq, width), lambda b, i, off=off: (b, i, off // width))
    out = pl.BlockSpec((None, tq, width), lambda b, i: (b, i, 0))
    gain = pl.BlockSpec((1, dh), lambda b, i: (0, 0))
    return pl.pallas_call(
        functools.partial(_qk_norm_kernel, dh=dh),
        grid=(B, T // tq),
        in_specs=[spec(cols["fq"]), spec(cols["fk"]), gain, gain],
        out_specs=[out, out],
        out_shape=[jax.ShapeDtypeStruct((B, T, width), BF16), jax.ShapeDtypeStruct((B, T, width), F32)],
        compiler_params=_params("parallel", "parallel"),
        name="qk_norm",
    )(h3, h3, q_gain.reshape(1, dh), k_gain.reshape(1, dh))


def _fox_prep_kernel(q_ref, k_ref, v_ref, g_ref, qg_ref, kg_ref,
                     qa_ref, ka_ref, vb_ref, kn_ref, vf_ref, carry_sc, *, dh, qscale):
    tp = q_ref.shape[0]
    nh = q_ref.shape[1] // dh

    @pl.when(pl.program_id(1) == 0)
    def _():
        carry_sc[...] = jnp.zeros_like(carry_sc)

    row = lax.broadcasted_iota(jnp.int32, (tp, tp), 0)
    col = lax.broadcasted_iota(jnp.int32, (tp, tp), 1)
    tri = jnp.where(col <= row, 1.0, 0.0).astype(BF16)
    cum = _dot_sel_left(tri, g_ref[...]) + carry_sc[...]
    carry_sc[...] = cum[tp - 1:tp, :]

    v = v_ref[...]
    vf_ref[...] = v
    vb_ref[...] = v.astype(BF16)
    lane = lax.broadcasted_iota(jnp.int32, (tp, N_AUG), 1)
    off = 2 * ML_HEADS
    for h in range(nh):
        sl = slice(h * dh, (h + 1) * dh)
        kn = _rms_rows(k_ref[:, sl], kg_ref[...])
        kn_ref[:, sl] = kn
        qn = _rms_rows(q_ref[:, sl], qg_ref[...]) * qscale
        f = cum[:, off + h:off + h + 1] * LOG2E
        f1 = f.astype(BF16).astype(F32)
        r1 = f - f1
        f2 = r1.astype(BF16).astype(F32)
        f3 = r1 - f2
        eq = jnp.where(lane < 3, 1.0, jnp.where(lane == 3, f1, jnp.where(lane == 4, f2,
                       jnp.where(lane == 5, f3, 0.0))))
        ek = jnp.where(lane == 0, -f1, jnp.where(lane == 1, -f2, jnp.where(lane == 2, -f3,
                       jnp.where(lane < 6, 1.0, 0.0))))
        base = h * (dh + N_AUG)
        qa_ref[:, base:base + dh] = qn.astype(BF16)
        qa_ref[:, base + dh:base + dh + N_AUG] = eq.astype(BF16)
        ka_ref[:, base:base + dh] = kn.astype(BF16)
        ka_ref[:, base + dh:base + dh + N_AUG] = ek.astype(BF16)


def _fox_prep(h3, g3, q_gain, k_gain, cols, *, width, dh, tp):
    B, T, _ = h3.shape
    tp = min(tp, T)
    nh = width // dh
    wa = nh * (dh + N_AUG)
    spec = lambda off: pl.BlockSpec((None, tp, width), lambda b, i, off=off: (b, i, off // width))
    out = lambda w: pl.BlockSpec((None, tp, w), lambda b, i: (b, i, 0))
    gain = pl.BlockSpec((1, dh), lambda b, i: (0, 0))
    return pl.pallas_call(
        functools.partial(_fox_prep_kernel, dh=dh, qscale=dh ** -0.5 * LOG2E),
        grid=(B, T // tp),
        in_specs=[spec(cols["fq"]), spec(cols["fk"]), spec(cols["fv"]),
                  pl.BlockSpec((None, tp, GATE_LANES), lambda b, i: (b, i, 0)), gain, gain],
        out_specs=[out(wa), out(wa), out(width), out(width), out(width)],
        out_shape=[jax.ShapeDtypeStruct((B, T, wa), BF16), jax.ShapeDtypeStruct((B, T, wa), BF16),
                   jax.ShapeDtypeStruct((B, T, width), BF16),
                   jax.ShapeDtypeStruct((B, T, width), F32), jax.ShapeDtypeStruct((B, T, width), F32)],
        scratch_shapes=[pltpu.VMEM((1, GATE_LANES), F32)],
        compiler_params=_params("parallel", "arbitrary"),
        name="fox_prep",
    )(h3, h3, h3, g3, q_gain.reshape(1, dh), k_gain.reshape(1, dh))


def _lane_tiles_reduce(x, op):
    acc = x[:, :128]
    for j in range(1, x.shape[1] // 128):
        acc = op(acc, x[:, j * 128:(j + 1) * 128])
    return acc


def _fox_flash_kernel(q_ref, k_ref, v_ref, z_ref, o_ref, m_sc, l_sc, acc_sc, mx_sc, s_sc, p_sc, *, t, sub):
    nq = q_ref.shape[0] // t
    ns = t // sub
    row_l = lax.broadcasted_iota(jnp.int32, (sub, sub), 0)
    col_l = lax.broadcasted_iota(jnp.int32, (sub, sub), 1)

    def block(diag, q0, k0):
        n_chunks = lambda i: i + 1 if diag else ns
        for i in range(ns):
            rows = slice(i * sub, (i + 1) * sub)
            q_i = q_ref[pl.ds(q0 + i * sub, sub), :]
            mx = None
            for c in range(n_chunks(i)):
                cs = slice(c * sub, (c + 1) * sub)
                s = lax.dot_general(q_i, k_ref[pl.ds(k0 + c * sub, sub), :], NT_DIMS,
                                    preferred_element_type=F32)
                if diag and c == i:
                    s = jnp.where(col_l <= row_l, s, -jnp.inf)
                s_sc[rows, cs] = s
                cm = _lane_tiles_reduce(s, jnp.maximum)
                mx = cm if mx is None else jnp.maximum(mx, cm)
            mx_sc[rows, :] = mx
        for i in range(ns):
            rows = slice(i * sub, (i + 1) * sub)
            m_prev = m_sc[rows, :]
            m_new = jnp.maximum(m_prev, jnp.max(mx_sc[rows, :], axis=1, keepdims=True))
            alpha = jnp.exp2(m_prev - m_new)
            ls = None
            for c in range(n_chunks(i)):
                cs = slice(c * sub, (c + 1) * sub)
                p = jnp.exp2(s_sc[rows, cs] - m_new)
                p_sc[rows, cs] = p.astype(BF16)
                lp = _lane_tiles_reduce(p, jnp.add)
                ls = lp if ls is None else ls + lp
            w = n_chunks(i) * sub
            l_sc[rows, :] = alpha * l_sc[rows, :] + jnp.sum(ls, axis=1, keepdims=True)
            acc_sc[rows, :] = alpha * acc_sc[rows, :] + jnp.dot(p_sc[rows, :w], v_ref[pl.ds(k0, w), :],
                                                                preferred_element_type=F32)
            m_sc[rows, :] = m_new

    def q_tile(qi, carry):
        q0 = pl.multiple_of(qi * t, t)
        m_sc[...] = jnp.full_like(m_sc, -jnp.inf)
        l_sc[...] = jnp.zeros_like(l_sc)
        acc_sc[...] = jnp.zeros_like(acc_sc)

        def k_tile(ki, c):
            block(False, q0, pl.multiple_of(ki * t, t))
            return c

        lax.fori_loop(0, qi, k_tile, 0)
        block(True, q0, q0)
        zz = z_ref[pl.ds(q0, t), :]
        o_ref[pl.ds(q0, t), :] = (zz * _sigmoid(zz) * (acc_sc[...] / l_sc[...])).astype(o_ref.dtype)
        return carry

    lax.fori_loop(0, nq, q_tile, 0)


def _fox_flash(qa, ka, vb, h3, cols, *, dh, t, sub):
    B, T, W = vb.shape
    t = min(t, T)
    sub = min(sub, t)
    da = dh + N_AUG
    zo = cols["fz"] // dh
    head = lambda w, off=0: pl.BlockSpec((None, T, w), lambda b, h, off=off: (b, 0, off + h))
    return pl.pallas_call(
        functools.partial(_fox_flash_kernel, t=t, sub=sub),
        grid=(B, W // dh),
        in_specs=[head(da), head(da), head(dh), head(dh, zo)],
        out_specs=head(dh),
        out_shape=jax.ShapeDtypeStruct((B, T, W), BF16),
        scratch_shapes=[pltpu.VMEM((t, 1), F32), pltpu.VMEM((t, 1), F32), pltpu.VMEM((t, dh), F32),
                        pltpu.VMEM((t, 128), F32), pltpu.VMEM((t, t), F32), pltpu.VMEM((t, t), BF16)],
        compiler_params=_params("parallel", "parallel"),
        name="fox_flash",
    )(qa, ka, vb, h3)


def _decay_selector(P, nh):
    t_src = jnp.arange(P)[:, None]
    t_dst = jnp.arange(P * nh)[None, :] // nh
    later = (t_src > t_dst).astype(BF16)
    return jnp.concatenate([later, jnp.ones((P, 128), BF16)], axis=1)


def _fox_decode_kernel(pt_ref, q_ref, kn_ref, vn_ref, z_ref, lfn_ref, u_ref, *rest, scale, nh, G):
    k_refs, v_refs, lf_refs = rest[:G], rest[G:2 * G], rest[2 * G:3 * G]
    o_ref, m_sc, l_sc, acc_sc, carry_sc = rest[3 * G:]
    p = pl.program_id(1)
    q = q_ref[...]

    @pl.when(p == 0)
    def _():
        s_new = jnp.sum(q.astype(F32) * kn_ref[...], axis=1, keepdims=True) * scale
        m_sc[...] = s_new
        l_sc[...] = jnp.ones_like(l_sc)
        acc_sc[...] = vn_ref[...]
        carry_sc[...] = lfn_ref[...]

    P = k_refs[0].shape[0]
    n = P * nh
    sub = lax.broadcasted_iota(jnp.int32, (nh, n), 0)
    lane = lax.broadcasted_iota(jnp.int32, (nh, n), 1)
    diag = (lane & (nh - 1)) == sub
    carry = carry_sc[...]
    scores = []
    for g in range(G):
        k2 = k_refs[g][...].reshape(n, k_refs[g].shape[2]).astype(BF16)
        s = lax.dot_general(q, k2, NT_DIMS, preferred_element_type=F32) * scale
        l1, l2, l3 = _split3(lf_refs[g][...])
        zz = jnp.dot(jnp.concatenate([l1, l2, l3], axis=0), u_ref[...], preferred_element_type=F32)
        zs = zz[0:nh] + zz[nh:2 * nh] + zz[2 * nh:3 * nh]
        s = s + zs[:, :n] + jnp.tile(carry, (1, n // 128))
        carry = carry + zs[:, n:]
        scores.append(jnp.where(diag, s, -jnp.inf))
    carry_sc[...] = carry

    m_prev = m_sc[...]
    m_new = m_prev
    for s in scores:
        m_new = jnp.maximum(m_new, jnp.max(s, axis=1, keepdims=True))
    alpha = jnp.exp(m_prev - m_new)
    l_new = alpha * l_sc[...]
    acc = alpha * acc_sc[...]
    for g in range(G):
        pr = jnp.exp(scores[g] - m_new)
        v2 = v_refs[g][...].reshape(n, v_refs[g].shape[2]).astype(BF16)
        l_new = l_new + jnp.sum(pr, axis=1, keepdims=True)
        acc = acc + jnp.dot(pr.astype(BF16), v2, preferred_element_type=F32)
    l_sc[...] = l_new
    acc_sc[...] = acc
    m_sc[...] = m_new

    @pl.when(p == pl.num_programs(1) - 1)
    def _():
        zz = z_ref[...]
        o_ref[...] = zz * _sigmoid(zz) * (acc / l_new)


def _fox_decode(page_table, q3, kn3, vn3, z3, lfn3, k_pool, v_pool, lf_pool_t, *, dh, G):
    N, nh, _ = q3.shape
    n_pages = page_table.shape[1]
    P = k_pool.shape[1]
    G = math.gcd(G, n_pages)
    assert nh & (nh - 1) == 0
    u = _decay_selector(P, nh)
    per = lambda: pl.BlockSpec((None, nh, dh), lambda b, p, pt: (b, 0, 0))
    page = lambda g: (lambda b, p, pt, g=g: (pt[b, n_pages - 1 - (p * G + g)], 0, 0, 0))
    page3 = lambda g: (lambda b, p, pt, g=g: (pt[b, n_pages - 1 - (p * G + g)], 0, 0))
    grid_spec = pltpu.PrefetchScalarGridSpec(
        num_scalar_prefetch=1,
        grid=(N, n_pages // G),
        in_specs=[per(), per(), per(), per(),
                  pl.BlockSpec((None, nh, 128), lambda b, p, pt: (b, 0, 0)),
                  pl.BlockSpec(u.shape, lambda b, p, pt: (0, 0))]
                 + [pl.BlockSpec((None, P, nh, dh), page(g)) for g in range(G)]
                 + [pl.BlockSpec((None, P, nh, dh), page(g)) for g in range(G)]
                 + [pl.BlockSpec((None, nh, P), page3(g)) for g in range(G)],
        out_specs=pl.BlockSpec((None, nh, dh), lambda b, p, pt: (b, 0, 0)),
        scratch_shapes=[pltpu.VMEM((nh, 1), F32), pltpu.VMEM((nh, 1), F32), pltpu.VMEM((nh, dh), F32),
                        pltpu.VMEM((nh, 128), F32)],
    )
    return pl.pallas_call(
        functools.partial(_fox_decode_kernel, scale=dh ** -0.5, nh=nh, G=G),
        grid_spec=grid_spec,
        out_shape=jax.ShapeDtypeStruct((N, nh, dh), F32),
        compiler_params=_params("parallel", "arbitrary"),
        name="fox_decode",
    )(page_table, q3, kn3, vn3, z3, lfn3, u, *([k_pool] * G), *([v_pool] * G), *([lf_pool_t] * G))


def _mem_attn_kernel(q_ref, z_ref, k_ref, v_ref, qg_ref, a_ref, *, dh, scale):
    nh = q_ref.shape[1] // dh
    for h in range(nh):
        sl = slice(h * dh, (h + 1) * dh)
        q = _rms_rows(q_ref[:, sl], qg_ref[...]).astype(BF16)
        s = lax.dot_general(q, k_ref[:, sl].astype(BF16), NT_DIMS, preferred_element_type=F32) * scale
        e = jnp.exp(s - jnp.max(s, axis=1, keepdims=True))
        o = jnp.dot(e.astype(BF16), v_ref[:, sl].astype(BF16), preferred_element_type=F32)
        o = o / jnp.sum(e, axis=1, keepdims=True)
        zz = z_ref[:, sl]
        a_ref[:, sl] = (zz * _sigmoid(zz) * o).astype(a_ref.dtype)


def _mem_attn_prompt(h3, mk, mv, q_gain, cols, *, width, dh, tq):
    B, T, _ = h3.shape
    n_mem = mk.shape[1]
    tq = min(tq, T)
    spec = lambda off: pl.BlockSpec((None, tq, width), lambda b, i, off=off: (b, i, off // width))
    mem = pl.BlockSpec((None, n_mem, width), lambda b, i: (b, 0, 0))
    return pl.pallas_call(
        functools.partial(_mem_attn_kernel, dh=dh, scale=dh ** -0.5),
        grid=(B, T // tq),
        in_specs=[spec(cols["cq"]), spec(cols["cz"]), mem, mem, pl.BlockSpec((1, dh), lambda b, i: (0, 0))],
        out_specs=pl.BlockSpec((None, tq, width), lambda b, i: (b, i, 0)),
        out_shape=jax.ShapeDtypeStruct((B, T, width), BF16),
        compiler_params=_params("parallel", "parallel"),
        name="mem_attn",
    )(h3, h3, mk, mv, q_gain.reshape(1, dh))


def _mem_decode_kernel(q_ref, z_ref, k_ref, v_ref, qg_ref, o_ref, *, scale, nh):
    q = _rms_rows(q_ref[...], qg_ref[...]).astype(BF16)
    kp = k_ref[...]
    vp = v_ref[...]
    rows = kp.shape[0] * kp.shape[1]
    k2 = kp.reshape(rows, kp.shape[2]).astype(BF16)
    v2 = vp.reshape(rows, vp.shape[2]).astype(BF16)
    s = lax.dot_general(q, k2, NT_DIMS, preferred_element_type=F32) * scale
    sub = lax.broadcasted_iota(jnp.int32, s.shape, 0)
    lane = lax.broadcasted_iota(jnp.int32, s.shape, 1)
    s = jnp.where((lane & (nh - 1)) == sub, s, -jnp.inf)
    m = jnp.maximum(jnp.max(s, axis=1, keepdims=True), -1e30)
    e = jnp.exp(s - m)
    den = jnp.maximum(jnp.sum(e, axis=1, keepdims=True), 1e-30)
    o = jnp.dot(e.astype(BF16), v2, preferred_element_type=F32) / den
    zz = z_ref[...]
    o_ref[...] = zz * _sigmoid(zz) * o


def _mem_decode(q3, z3, k_cache, v_cache, q_gain, *, dh):
    N, n_mem, nh, _ = k_cache.shape
    assert nh & (nh - 1) == 0
    per = pl.BlockSpec((None, 8, dh), lambda b: (b, 0, 0))
    cache = pl.BlockSpec((None, n_mem, nh, dh), lambda b: (b, 0, 0, 0))
    return pl.pallas_call(
        functools.partial(_mem_decode_kernel, scale=dh ** -0.5, nh=nh),
        grid=(N,),
        in_specs=[per, per, cache, cache, pl.BlockSpec((1, dh), lambda b: (0, 0))],
        out_specs=per,
        out_shape=jax.ShapeDtypeStruct((N, 8, dh), F32),
        compiler_params=_params("parallel"),
        name="mem_decode",
    )(q3, z3, k_cache, v_cache, q_gain.reshape(1, dh))


def _merge_kernel(am_ref, af_ref, ac_ref, wm_ref, wf_ref, wc_ref, gm_ref, gf_ref, gc_ref, u_ref):
    d = lambda a, w: jnp.dot(a[...], w[...], preferred_element_type=F32)
    u = _sigmoid(gm_ref[...]) * d(am_ref, wm_ref)
    u = u + _sigmoid(gf_ref[...]) * d(af_ref, wf_ref)
    u = u + _sigmoid(gc_ref[...]) * d(ac_ref, wc_ref)
    u_ref[...] = u.astype(u_ref.dtype)


def _merge(am, af, ac, wm, wf, wc, h2, g_off, *, tm, tn):
    m, k = am.shape
    n = wm.shape[1]
    tm = min(tm, m)
    a_spec = pl.BlockSpec((tm, k), lambda i, j: (i, 0))
    w_spec = pl.BlockSpec((k, tn), lambda i, j: (0, j))
    g_spec = lambda b: pl.BlockSpec((tm, tn), lambda i, j, b=b: (i, (g_off + b * n) // tn + j))
    return pl.pallas_call(
        _merge_kernel,
        grid=(m // tm, n // tn),
        in_specs=[a_spec, a_spec, a_spec, w_spec, w_spec, w_spec, g_spec(0), g_spec(1), g_spec(2)],
        out_specs=pl.BlockSpec((tm, tn), lambda i, j: (i, j)),
        out_shape=jax.ShapeDtypeStruct((m, n), BF16),
        compiler_params=_params("parallel", "arbitrary"),
        name="merge",
    )(am, af, ac, wm, wf, wc, h2, h2, h2)


def _out_proj_kernel(u_ref, w_ref, x_ref, y_ref):
    y_ref[...] = x_ref[...] + jnp.dot(u_ref[...], w_ref[...], preferred_element_type=F32)


def _out_proj(u, w, x, *, tm, tn):
    m, k = u.shape
    n = w.shape[1]
    tm = min(tm, m)
    return pl.pallas_call(
        _out_proj_kernel,
        grid=(m // tm, n // tn),
        in_specs=[
            pl.BlockSpec((tm, k), lambda i, j: (i, 0)),
            pl.BlockSpec((k, tn), lambda i, j: (0, j)),
            pl.BlockSpec((tm, tn), lambda i, j: (i, j)),
        ],
        out_specs=pl.BlockSpec((tm, tn), lambda i, j: (i, j)),
        out_shape=jax.ShapeDtypeStruct((m, n), F32),
        compiler_params=_params("parallel", "arbitrary"),
        name="out_proj",
    )(u, w, x)


def _column_layout(d_model):
    ml_dv = d_model // ML_HEADS
    ml_qkw = ML_HEADS * (ml_dv // 2)
    names = ("mq", "mk", "mv", "mi", "mf", "mo", "mz", "fq", "fk", "fv", "ff", "fz", "cq", "cz", "g")
    widths = (ml_qkw, ml_qkw, d_model, ML_HEADS, ML_HEADS, d_model, d_model,
              d_model, d_model, d_model, FOX_HEADS, d_model, d_model, d_model, N_BRANCH * d_model)
    src, off = {}, 0
    for n, w in zip(names, widths):
        src[n] = (off, w)
        off += w
    big, boff = {}, 0
    for n in names:
        if n in ("mi", "mf", "ff"):
            continue
        big[n] = boff
        boff += src[n][1]
    return src, big, boff


def _layer(x_p, x_s, mem_p, fox_k_pool, fox_v_pool, fox_lf_pool, page_table, mem_k_s, mem_v_s,
           c0_s, n0_s, m0_s, norm_in, norm_mem, w_in, w_mem_kv, b_i, b_f, b_ff, ml_out_norm,
           fq_norm, fk_norm, cq_norm, ck_norm, w_br_m, w_br_f, w_br_c, w_out):
    B, T, D = x_p.shape
    N = x_s.shape[0]
    dv = D // ML_HEADS
    dqk = dv // 2
    fox_dh = D // FOX_HEADS
    mem_dh = D // MEM_HEADS
    src, cols, nbig = _column_layout(D)
    w_in_t = w_in.T
    gap1, gap2 = cols["mo"], cols["fz"]
    skip1, skip2 = 2 * ML_HEADS, 2 * ML_HEADS + FOX_HEADS

    def big_rows(tn):
        assert gap1 % tn == 0 and gap2 % tn == 0
        return lambda j: j * tn + jnp.where(j * tn >= gap2, skip2, jnp.where(j * tn >= gap1, skip1, 0))

    rows = lambda n: w_in_t[src[n][0]:src[n][0] + src[n][1]]
    n_gate = 2 * ML_HEADS + FOX_HEADS
    w_gate_t = jnp.concatenate([rows("mi"), rows("mf"), rows("ff"),
                                jnp.zeros((GATE_LANES - n_gate, D), F32)], axis=0)
    b_gate = jnp.concatenate([b_i, b_f, b_ff, jnp.zeros((GATE_LANES - n_gate,), F32)]).reshape(1, GATE_LANES)
    wm, wf, wc, wo = (w.astype(BF16) for w in (w_br_m, w_br_f, w_br_c, w_out))

    xp2 = x_p.reshape(B * T, D)
    xs2 = x_s.reshape(N, D)

    h2 = _norm_proj(xp2, norm_in, w_in_t, tm=1024, tn=1024, n_out=nbig, w_index=big_rows(1024), transposed=True)
    g2 = _gates(xp2, norm_in, w_gate_t, b_gate, tm=1024)
    h3 = h2.reshape(B, T, nbig)
    g3 = g2.reshape(B, T, GATE_LANES)

    n_mem = mem_p.shape[1]
    mem2 = mem_p.reshape(B * n_mem, D)
    heads_k = D // mem_dh
    mk_p = _norm_proj(mem2, norm_mem, w_mem_kv, tm=512, tn=mem_dh, n_out=D, w_index=lambda j: j,
                      transposed=False, head_gain=ck_norm)
    mv_p = _norm_proj(mem2, norm_mem, w_mem_kv, tm=512, tn=mem_dh, n_out=D, w_index=lambda j: heads_k + j,
                      transposed=False)

    a_m, c_p, n_p, m_p = _mlstm_prompt(h3, g3, ml_out_norm, cols, dqk=dqk, dv=dv, chunks_per_step=8)

    qa, ka, vb, kn, fv_p = _fox_prep(h3, g3, fq_norm, fk_norm, cols, width=D, dh=fox_dh, tp=256)
    a_f = _fox_flash(qa, ka, vb, h3, cols, dh=fox_dh, t=1024, sub=256)

    a_c = _mem_attn_prompt(h3, mk_p.reshape(B, n_mem, D), mv_p.reshape(B, n_mem, D), cq_norm, cols,
                           width=D, dh=mem_dh, tq=512)

    u_p = _merge(a_m.reshape(B * T, D), a_f.reshape(B * T, D), a_c.reshape(B * T, D),
                 wm, wf, wc, h2, cols["g"], tm=1024, tn=256)
    y_p = _out_proj(u_p, wo, xp2, tm=1024, tn=1024).reshape(B, T, D)

    hs2 = _norm_proj(xs2, norm_in, w_in_t, tm=128, tn=1024, n_out=nbig, w_index=big_rows(1024), transposed=True)
    gs2 = _gates(xs2, norm_in, w_gate_t, b_gate, tm=128)
    hs3 = hs2.reshape(N, 1, nbig)
    gs3 = gs2.reshape(N, 1, GATE_LANES)

    a_ms, c_s, n_s, m_s = _mlstm_sample(hs3, gs3, m0_s.reshape(N, 1, ML_HEADS), ml_out_norm, c0_s, n0_s,
                                        cols, dqk=dqk, dv=dv)

    hs1 = hs2.reshape(1, N, nbig)
    qn_s, kn_s = _qk_norm(hs1, fq_norm, fk_norm, cols, width=D, dh=fox_dh, tq=128)
    seg = lambda n: hs2[:, cols[n]:cols[n] + D]
    fv_s = seg("fv")
    flf_s = gs2[:, 2 * ML_HEADS:n_gate]
    lfn = jnp.broadcast_to(flf_s[:, :, None], (N, FOX_HEADS, 128))
    to_heads = lambda a: a.reshape(N, FOX_HEADS, fox_dh)
    o_f = _fox_decode(page_table, to_heads(qn_s.reshape(N, D)), to_heads(kn_s.reshape(N, D)), to_heads(fv_s),
                      to_heads(seg("fz")), lfn, fox_k_pool, fox_v_pool, jnp.swapaxes(fox_lf_pool, 1, 2), dh=fox_dh, G=8)
    a_fs = o_f.reshape(N, D).astype(BF16)

    pad8 = lambda a: jnp.pad(a.reshape(N, MEM_HEADS, mem_dh), ((0, 0), (0, 8 - MEM_HEADS), (0, 0)))
    o_c = _mem_decode(pad8(seg("cq")), pad8(seg("cz")), mem_k_s, mem_v_s, cq_norm, dh=mem_dh)
    a_cs = o_c[:, :MEM_HEADS].reshape(N, D).astype(BF16)

    u_s = _merge(a_ms.reshape(N, D), a_fs, a_cs, wm, wf, wc, hs2, cols["g"], tm=128, tn=512)
    y_s = _out_proj(u_s, wo, xs2, tm=128, tn=1024).reshape(N, 1, D)

    outs_p = (kn.reshape(B, T, FOX_HEADS, fox_dh), fv_p.reshape(B, T, FOX_HEADS, fox_dh),
              g3[:, :, 2 * ML_HEADS:n_gate],
              mk_p.reshape(B, n_mem, MEM_HEADS, mem_dh), mv_p.reshape(B, n_mem, MEM_HEADS, mem_dh),
              c_p, n_p.reshape(B, ML_HEADS, dqk), m_p.reshape(B, ML_HEADS))
    outs_s = (kn_s.reshape(N, 1, FOX_HEADS, fox_dh), fv_s.reshape(N, 1, FOX_HEADS, fox_dh),
              flf_s.reshape(N, 1, FOX_HEADS), c_s, n_s, m_s.reshape(N, ML_HEADS))
    return y_p, y_s, outs_p, outs_s


def kernel(x_prompt, x_sample, mem_prompt, cache_fox_k, cache_fox_v, cache_fox_logf, page_table, cache_mem_k, cache_mem_v, state_mlstm_C, state_mlstm_n, state_mlstm_m, norm_in, norm_mem, w_in, w_mem_kv, b_mlstm_i, b_mlstm_f, b_fox_f, mlstm_out_norm, fox_q_norm, fox_k_norm, mem_q_norm, mem_k_norm, w_br_mlstm, w_br_fox, w_br_mem, w_out):
    depth = w_in.shape[0]
    y_p, y_s = x_prompt, x_sample
    per_layer = []
    for l in range(depth):
        y_p, y_s, outs_p, outs_s = _layer(
            y_p, y_s, mem_prompt, cache_fox_k[l], cache_fox_v[l], cache_fox_logf[l], page_table,
            cache_mem_k[l], cache_mem_v[l], state_mlstm_C[l], state_mlstm_n[l], state_mlstm_m[l],
            norm_in[l], norm_mem[l], w_in[l], w_mem_kv[l], b_mlstm_i[l], b_mlstm_f[l], b_fox_f[l],
            mlstm_out_norm[l], fox_q_norm[l], fox_k_norm[l], mem_q_norm[l], mem_k_norm[l],
            w_br_mlstm[l], w_br_fox[l], w_br_mem[l], w_out[l])
        per_layer.append(outs_p + outs_s)
    stacked = tuple(jnp.stack([lay[i] for lay in per_layer]) for i in range(len(per_layer[0])))
    return (y_p, y_s) + stacked
```

```python
import functools
import math

import jax
import jax.numpy as jnp
from jax import lax
from jax.experimental import pallas as pl
from jax.experimental.pallas import tpu as pltpu

F32 = jnp.float32
BF16 = jnp.bfloat16

ML_HEADS = 4
FOX_HEADS = 16
MEM_HEADS = 4
N_BRANCH = 3
ML_CHUNK = 128
EPS = 1e-6
LOG2E = 1.4426950408889634
GATE_LANES = 128
VMEM_LIMIT = 56 * 1024 * 1024

NT_DIMS = (((1,), (1,)), ((), ()))
TN_DIMS = (((0,), (0,)), ((), ()))


def _params(*sem):
    return pltpu.CompilerParams(dimension_semantics=sem, vmem_limit_bytes=VMEM_LIMIT)


def _sigmoid(z):
    return 1.0 / (1.0 + jnp.exp(-z))


def _log_sigmoid(z):
    return jnp.minimum(z, 0.0) - jnp.log1p(jnp.exp(-jnp.abs(z)))


def _split3(x):
    x1 = x.astype(BF16)
    r1 = x - x1.astype(F32)
    x2 = r1.astype(BF16)
    r2 = r1 - x2.astype(F32)
    return x1, x2, r2.astype(BF16)


def _dot_sel_left(sel, x):
    x1, x2, x3 = _split3(x)
    d = lambda p: jnp.dot(sel, p, preferred_element_type=F32)
    return d(x1) + d(x2) + d(x3)


def _rms_rows(x, gain):
    return x * lax.rsqrt(jnp.mean(x * x, axis=-1, keepdims=True) + EPS) * gain


def _norm_proj_kernel(*refs, head_norm, transposed, rider):
    refs = list(refs)
    x_ref = refs.pop(0)
    xr_ref = refs.pop(0) if rider else None
    g_ref, w_ref = refs.pop(0), refs.pop(0)
    hg_ref = refs.pop(0) if head_norm else None
    o_ref = refs.pop(0)
    or_ref = refs.pop(0) if rider else None
    xn_ref, = refs
    tm = x_ref.shape[0]

    @pl.when(pl.program_id(1) == 0)
    def _():
        xn_ref[:tm, :] = _rms_rows(x_ref[...], g_ref[...]).astype(BF16)
        if rider:
            xn_ref[tm:, :] = _rms_rows(xr_ref[...], g_ref[...]).astype(BF16)

    w = w_ref[...].astype(BF16)
    if transposed:
        acc = lax.dot_general(xn_ref[...], w, NT_DIMS, preferred_element_type=F32)
    else:
        acc = jnp.dot(xn_ref[...], w, preferred_element_type=F32)
    if head_norm:
        acc = _rms_rows(acc, hg_ref[...])
    o_ref[...] = acc[:tm].astype(o_ref.dtype)
    if rider:
        or_ref[...] = acc[tm:].astype(or_ref.dtype)


def _norm_proj(x, gain, w, *, tm, tn, n_out, w_index, transposed, head_gain=None, x_rider=None):
    m, k = x.shape
    n = n_out
    tm = min(tm, m)
    assert m % tm == 0 and n % tn == 0
    rider = x_rider is not None
    tr = 0
    if rider:
        tr = x_rider.shape[0] // (m // tm)
        assert tr * (m // tm) == x_rider.shape[0] and tr % 16 == 0
    if transposed:
        w_spec = pl.BlockSpec((pl.Element(tn), pl.Element(k)), lambda i, j: (pl.multiple_of(w_index(j), 8), 0))
    else:
        w_spec = pl.BlockSpec((k, tn), lambda i, j: (0, w_index(j)))
    in_specs = [pl.BlockSpec((tm, k), lambda i, j: (i, 0))]
    args = [x]
    if rider:
        in_specs.append(pl.BlockSpec((tr, k), lambda i, j: (i, 0)))
        args.append(x_rider)
    in_specs += [pl.BlockSpec((1, k), lambda i, j: (0, 0)), w_spec]
    args += [gain.reshape(1, k), w]
    if head_gain is not None:
        assert head_gain.shape == (tn,)
        in_specs.append(pl.BlockSpec((1, tn), lambda i, j: (0, 0)))
        args.append(head_gain.reshape(1, tn))
    out_specs = [pl.BlockSpec((tm, tn), lambda i, j: (i, j))]
    out_shape = [jax.ShapeDtypeStruct((m, n), F32)]
    if rider:
        out_specs.append(pl.BlockSpec((tr, tn), lambda i, j: (i, j)))
        out_shape.append(jax.ShapeDtypeStruct((x_rider.shape[0], n), F32))
    outs = pl.pallas_call(
        functools.partial(_norm_proj_kernel, head_norm=head_gain is not None, transposed=transposed, rider=rider),
        grid=(m // tm, n // tn),
        in_specs=in_specs,
        out_specs=out_specs,
        out_shape=out_shape,
        scratch_shapes=[pltpu.VMEM((tm + tr, k), BF16)],
        compiler_params=_params("parallel", "arbitrary"),
        name="norm_proj",
    )(*args)
    return (outs[0], outs[1]) if rider else (outs[0], None)


def _gates_kernel(x_ref, g_ref, w_ref, b_ref, o_ref):
    xn = _rms_rows(x_ref[...], g_ref[...])
    x1, x2, _ = _split3(xn)
    w1, w2, _ = _split3(w_ref[...])
    d = lambda a, b: lax.dot_general(a, b, NT_DIMS, preferred_element_type=F32)
    z = d(x1, w1) + d(x1, w2) + d(x2, w1) + b_ref[...]
    lane = lax.broadcasted_iota(jnp.int32, z.shape, 1)
    o_ref[...] = jnp.where(lane < ML_HEADS, z, _log_sigmoid(z))


def _gates(x, gain, w_gate, bias, *, tm):
    m, k = x.shape
    tm = min(tm, m)
    return pl.pallas_call(
        _gates_kernel,
        grid=(m // tm,),
        in_specs=[
            pl.BlockSpec((tm, k), lambda i: (i, 0)),
            pl.BlockSpec((1, k), lambda i: (0, 0)),
            pl.BlockSpec((GATE_LANES, k), lambda i: (0, 0)),
            pl.BlockSpec((1, GATE_LANES), lambda i: (0, 0)),
        ],
        out_specs=pl.BlockSpec((tm, GATE_LANES), lambda i: (i, 0)),
        out_shape=jax.ShapeDtypeStruct((m, GATE_LANES), F32),
        compiler_params=_params("parallel"),
        name="gates",
    )(x, gain.reshape(1, k), w_gate, bias)


def _mlstm_chunk_kernel(q_ref, k_ref, v_ref, o_ref, z_ref, g_ref, gain_ref,
                        a_ref, c_out, n_out, m_out, c_sc, n_sc, m_sc, *, dqk, dv, L, HP):
    hp = pl.program_id(1)
    step = pl.program_id(2)
    n_chunks = q_ref.shape[0] // L

    @pl.when(step == 0)
    def _():
        c_sc[...] = jnp.zeros_like(c_sc)
        n_sc[...] = jnp.zeros_like(n_sc)
        m_sc[...] = jnp.zeros_like(m_sc)

    row = lax.broadcasted_iota(jnp.int32, (L, L), 0)
    col = lax.broadcasted_iota(jnp.int32, (L, L), 1)
    causal = col <= row
    tri = jnp.where(causal, 1.0, 0.0).astype(BF16)
    lane = lax.broadcasted_iota(jnp.int32, (L, GATE_LANES), 1)
    sub = lax.broadcasted_iota(jnp.int32, (GATE_LANES, L), 0)
    pick_col = lambda x, idx: jnp.sum(jnp.where(lane == idx, x, 0.0), axis=1, keepdims=True)
    pick_row = lambda x, idx: jnp.sum(jnp.where(sub == idx, x, 0.0), axis=0, keepdims=True)

    def chunk(ci, carry):
        rows = pl.ds(pl.multiple_of(ci * L, L), L)
        g = g_ref[rows, :]
        cum = _dot_sel_left(tri, g)
        g_t = g.T
        cum_t = cum.T
        for hh in range(HP):
            h = hp * HP + hh
            qs = slice(hh * dqk, (hh + 1) * dqk)
            vs = slice(hh * dv, (hh + 1) * dv)
            ig_col = pick_col(g, h)
            b_col = pick_col(cum, ML_HEADS + h)
            ig_row = pick_row(g_t, h)
            b_row = pick_row(cum_t, ML_HEADS + h)

            m_prev = m_sc[hh]
            a_col = b_col + m_prev
            dmat = jnp.where(causal, b_col - b_row + ig_row, -jnp.inf)
            m_t = jnp.maximum(a_col, jnp.max(dmat, axis=1, keepdims=True))
            w_inter = jnp.exp(a_col - m_t)
            w_intra = jnp.exp(dmat - m_t)

            qf = q_ref[rows, qs] * (dqk ** -0.5)
            kf = k_ref[rows, qs]
            vf = v_ref[rows, vs]
            q = qf.astype(BF16)
            k = kf.astype(BF16)
            v = vf.astype(BF16)
            c_prev = c_sc[hh]
            n_prev = n_sc[hh]

            s = lax.dot_general(q, k, NT_DIMS, preferred_element_type=F32) * w_intra
            inter = lax.dot_general(q, c_prev.astype(BF16), NT_DIMS, preferred_element_type=F32)
            num = w_inter * inter + jnp.dot(s.astype(BF16), v, preferred_element_type=F32)
            nq = w_inter * jnp.sum(qf * n_prev, axis=1, keepdims=True) + jnp.sum(s, axis=1, keepdims=True)
            hid = num / jnp.maximum(jnp.abs(nq), jnp.exp(-m_t))

            b_last = b_col[L - 1:L, :]
            m_new = m_t[L - 1:L, :]
            w_c = jnp.exp(b_last + m_prev - m_new)
            w_s = jnp.exp(b_last - b_col + ig_col - m_new)
            wv = (w_s * vf).astype(BF16)
            c_sc[hh] = w_c * c_prev + lax.dot_general(wv, k, TN_DIMS, preferred_element_type=F32)
            n_sc[hh] = w_c * n_prev + jnp.sum(w_s * kf, axis=0, keepdims=True)
            m_sc[hh] = m_new

            hn = _rms_rows(hid, gain_ref[:, vs])
            zz = z_ref[rows, vs]
            a_ref[rows, vs] = (zz * _sigmoid(zz) * (_sigmoid(o_ref[rows, vs]) * hn)).astype(a_ref.dtype)
        return carry

    lax.fori_loop(0, n_chunks, chunk, 0)

    @pl.when(step == pl.num_programs(2) - 1)
    def _():
        c_out[...] = c_sc[...]
        n_out[...] = n_sc[...]
        m_out[...] = m_sc[...]


def _mlstm_prompt(h3, g3, out_gain, cols, *, dqk, dv, chunks_per_step, heads_per_step):
    B, T, _ = h3.shape
    L = math.gcd(T, ML_CHUNK)
    rows = L * math.gcd(T // L, chunks_per_step)
    HP = math.gcd(ML_HEADS, heads_per_step)
    qo, ko, vo, oo, zo = (cols[n] for n in ("mq", "mk", "mv", "mo", "mz"))
    blk = lambda w, off: pl.BlockSpec((None, rows, HP * w),
                                      lambda b, h, c, off=off, w=w: (b, c, off // (HP * w) + h))
    kern = functools.partial(_mlstm_chunk_kernel, dqk=dqk, dv=dv, L=L, HP=HP)
    return pl.pallas_call(
        kern,
        grid=(B, ML_HEADS // HP, T // rows),
        in_specs=[
            blk(dqk, qo), blk(dqk, ko), blk(dv, vo), blk(dv, oo), blk(dv, zo),
            pl.BlockSpec((None, rows, GATE_LANES), lambda b, h, c: (b, c, 0)),
            pl.BlockSpec((1, HP * dv), lambda b, h, c: (0, h)),
        ],
        out_specs=[
            pl.BlockSpec((None, rows, HP * dv), lambda b, h, c: (b, c, h)),
            pl.BlockSpec((None, HP, dv, dqk), lambda b, h, c: (b, h, 0, 0)),
            pl.BlockSpec((None, HP, 1, dqk), lambda b, h, c: (b, h, 0, 0)),
            pl.BlockSpec((None, HP, 1, 1), lambda b, h, c: (b, h, 0, 0)),
        ],
        out_shape=[
            jax.ShapeDtypeStruct((B, T, ML_HEADS * dv), BF16),
            jax.ShapeDtypeStruct((B, ML_HEADS, dv, dqk), F32),
            jax.ShapeDtypeStruct((B, ML_HEADS, 1, dqk), F32),
            jax.ShapeDtypeStruct((B, ML_HEADS, 1, 1), F32),
        ],
        scratch_shapes=[pltpu.VMEM((HP, dv, dqk), F32), pltpu.VMEM((HP, 1, dqk), F32),
                        pltpu.VMEM((HP, 1, 1), F32)],
        compiler_params=_params("parallel", "parallel", "arbitrary"),
        name="mlstm_chunk",
    )(h3, h3, h3, h3, h3, g3, out_gain.reshape(1, ML_HEADS * dv))


def _mlstm_step_kernel(q_ref, k_ref, v_ref, o_ref, z_ref, g_ref, m0_ref, gain_ref, c0_ref, n0_ref,
                       a_ref, c_out, n_out, m_out, *, dqk, dv):
    lane_h = lax.broadcasted_iota(jnp.int32, (1, ML_HEADS), 1)
    first = lax.broadcasted_iota(jnp.int32, (16, 1), 0) == 0
    for s_i in range(q_ref.shape[0]):
        g = g_ref[s_i]
        m0 = m0_ref[s_i]
        m_acc = jnp.zeros((1, ML_HEADS), F32)
        for h in range(ML_HEADS):
            ig = g[:, h:h + 1]
            lf = g[:, ML_HEADS + h:ML_HEADS + h + 1]
            a = lf + m0[:, h:h + 1]
            m_t = jnp.maximum(a, ig)
            w_inter = jnp.exp(a - m_t)
            w_intra = jnp.exp(ig - m_t)
            qf = q_ref[s_i, :, h * dqk:(h + 1) * dqk] * (dqk ** -0.5)
            kf = k_ref[s_i, :, h * dqk:(h + 1) * dqk]
            vf = v_ref[s_i, :, h * dv:(h + 1) * dv]
            c0 = c0_ref[s_i, h]
            n0 = n0_ref[s_i, h:h + 1, :]
            s = jnp.sum(qf * kf, axis=1, keepdims=True) * w_intra
            q8 = jnp.broadcast_to(qf, (8, dqk)).astype(BF16)
            cq = lax.dot_general(q8, c0.astype(BF16), NT_DIMS, preferred_element_type=F32)[0:1, :]
            num = w_inter * cq + s * vf
            nq = w_inter * jnp.sum(n0 * qf, axis=1, keepdims=True) + s
            hid = num / jnp.maximum(jnp.abs(nq), jnp.exp(-m_t))
            v16 = jnp.where(first, vf, 0.0).astype(BF16)
            k16 = jnp.where(first, w_intra * kf, 0.0).astype(BF16)
            c_out[s_i, h] = w_inter * c0 + lax.dot_general(v16, k16, TN_DIMS, preferred_element_type=F32)
            n_out[s_i, h:h + 1, :] = w_inter * n0 + w_intra * kf
            m_acc = m_acc + jnp.where(lane_h == h, m_t, 0.0)
            hn = _rms_rows(hid, gain_ref[:, h * dv:(h + 1) * dv])
            zz = z_ref[s_i, :, h * dv:(h + 1) * dv]
            gate_o = _sigmoid(o_ref[s_i, :, h * dv:(h + 1) * dv])
            a_ref[s_i, :, h * dv:(h + 1) * dv] = (zz * _sigmoid(zz) * (gate_o * hn)).astype(a_ref.dtype)
        m_out[s_i] = m_acc


def _mlstm_sample(hs3, gs3, m0, out_gain, c0, n0, cols, *, dqk, dv, samples_per_step):
    N = hs3.shape[0]
    S = math.gcd(N, samples_per_step)
    W = ML_HEADS * dv
    WQ = ML_HEADS * dqk
    qo, ko, vo, oo, zo = (cols[n] for n in ("mq", "mk", "mv", "mo", "mz"))
    row = lambda w, off: pl.BlockSpec((S, 1, w), lambda b, off=off, w=w: (b, 0, off // w))
    kern = functools.partial(_mlstm_step_kernel, dqk=dqk, dv=dv)
    return pl.pallas_call(
        kern,
        grid=(N // S,),
        in_specs=[
            row(WQ, qo), row(WQ, ko), row(W, vo), row(W, oo), row(W, zo),
            pl.BlockSpec((S, 1, GATE_LANES), lambda b: (b, 0, 0)),
            pl.BlockSpec((S, 1, ML_HEADS), lambda b: (b, 0, 0)),
            pl.BlockSpec((1, W), lambda b: (0, 0)),
            pl.BlockSpec((S, ML_HEADS, dv, dqk), lambda b: (b, 0, 0, 0)),
            pl.BlockSpec((S, ML_HEADS, dqk), lambda b: (b, 0, 0)),
        ],
        out_specs=[
            pl.BlockSpec((S, 1, W), lambda b: (b, 0, 0)),
            pl.BlockSpec((S, ML_HEADS, dv, dqk), lambda b: (b, 0, 0, 0)),
            pl.BlockSpec((S, ML_HEADS, dqk), lambda b: (b, 0, 0)),
            pl.BlockSpec((S, 1, ML_HEADS), lambda b: (b, 0, 0)),
        ],
        out_shape=[
            jax.ShapeDtypeStruct((N, 1, W), BF16),
            jax.ShapeDtypeStruct((N, ML_HEADS, dv, dqk), F32),
            jax.ShapeDtypeStruct((N, ML_HEADS, dqk), F32),
            jax.ShapeDtypeStruct((N, 1, ML_HEADS), F32),
        ],
        compiler_params=_params("parallel"),
        name="mlstm_step",
    )(hs3, hs3, hs3, hs3, hs3, gs3, m0, out_gain.reshape(1, W), c0, n0)


N_AUG = 128


def _qk_norm_kernel(q_ref, k_ref, qg_ref, kg_ref, qn_ref, kn_ref, *, dh):
    nh = q_ref.shape[1] // dh
    for h in range(nh):
        sl = slice(h * dh, (h + 1) * dh)
        qn_ref[:, sl] = _rms_rows(q_ref[:, sl], qg_ref[...]).astype(qn_ref.dtype)
        kn_ref[:, sl] = _rms_rows(k_ref[:, sl], kg_ref[...])


def _qk_norm(h3, q_gain, k_gain, cols, *, width, dh, tq):
    B, T, _ = h3.shape
    tq = min(tq, T)
    spec = lambda off: pl.BlockSpec((None, tq, width), lambda b, i, off=off: (b, i, off // width))
    out = pl.BlockSpec((None, tq, width), lambda b, i: (b, i, 0))
    gain = pl.BlockSpec((1, dh), lambda b, i: (0, 0))
    return pl.pallas_call(
        functools.partial(_qk_norm_kernel, dh=dh),
        grid=(B, T // tq),
        in_specs=[spec(cols["fq"]), spec(cols["fk"]), gain, gain],
        out_specs=[out, out],
        out_shape=[jax.ShapeDtypeStruct((B, T, width), BF16), jax.ShapeDtypeStruct((B, T, width), F32)],
        compiler_params=_params("parallel", "parallel"),
        name="qk_norm",
    )(h3, h3, q_gain.reshape(1, dh), k_gain.reshape(1, dh))


def _fox_prep_kernel(q_ref, k_ref, v_ref, g_ref, qg_ref, kg_ref,
                     qa_ref, ka_ref, vb_ref, kn_ref, vf_ref, carry_sc, *, dh, qscale):
    tp = q_ref.shape[0]
    nh = q_ref.shape[1] // dh

    @pl.when(pl.program_id(1) == 0)
    def _():
        carry_sc[...] = jnp.zeros_like(carry_sc)

    row = lax.broadcasted_iota(jnp.int32, (tp, tp), 0)
    col = lax.broadcasted_iota(jnp.int32, (tp, tp), 1)
    tri = jnp.where(col <= row, 1.0, 0.0).astype(BF16)
    cum = _dot_sel_left(tri, g_ref[...]) + carry_sc[...]
    carry_sc[...] = cum[tp - 1:tp, :]

    v = v_ref[...]
    vf_ref[...] = v
    vb_ref[...] = v.astype(BF16)
    lane = lax.broadcasted_iota(jnp.int32, (tp, N_AUG), 1)
    off = 2 * ML_HEADS
    for h in range(nh):
        sl = slice(h * dh, (h + 1) * dh)
        kn = _rms_rows(k_ref[:, sl], kg_ref[...])
        kn_ref[:, sl] = kn
        qn = _rms_rows(q_ref[:, sl], qg_ref[...]) * qscale
        f = cum[:, off + h:off + h + 1] * LOG2E
        f1 = f.astype(BF16).astype(F32)
        r1 = f - f1
        f2 = r1.astype(BF16).astype(F32)
        f3 = r1 - f2
        eq = jnp.where(lane < 3, 1.0, jnp.where(lane == 3, f1, jnp.where(lane == 4, f2,
                       jnp.where(lane == 5, f3, 0.0))))
        ek = jnp.where(lane == 0, -f1, jnp.where(lane == 1, -f2, jnp.where(lane == 2, -f3,
                       jnp.where(lane < 6, 1.0, 0.0))))
        base = h * (dh + N_AUG)
        qa_ref[:, base:base + dh] = qn.astype(BF16)
        qa_ref[:, base + dh:base + dh + N_AUG] = eq.astype(BF16)
        ka_ref[:, base:base + dh] = kn.astype(BF16)
        ka_ref[:, base + dh:base + dh + N_AUG] = ek.astype(BF16)


def _fox_prep(h3, g3, q_gain, k_gain, cols, *, width, dh, tp):
    B, T, _ = h3.shape
    tp = min(tp, T)
    nh = width // dh
    wa = nh * (dh + N_AUG)
    spec = lambda off: pl.BlockSpec((None, tp, width), lambda b, i, off=off: (b, i, off // width))
    out = lambda w: pl.BlockSpec((None, tp, w), lambda b, i: (b, i, 0))
    gain = pl.BlockSpec((1, dh), lambda b, i: (0, 0))
    return pl.pallas_call(
        functools.partial(_fox_prep_kernel, dh=dh, qscale=dh ** -0.5 * LOG2E),
        grid=(B, T // tp),
        in_specs=[spec(cols["fq"]), spec(cols["fk"]), spec(cols["fv"]),
                  pl.BlockSpec((None, tp, GATE_LANES), lambda b, i: (b, i, 0)), gain, gain],
        out_specs=[out(wa), out(wa), out(width), out(width), out(width)],
        out_shape=[jax.ShapeDtypeStruct((B, T, wa), BF16), jax.ShapeDtypeStruct((B, T, wa), BF16),
                   jax.ShapeDtypeStruct((B, T, width), BF16),
                   jax.ShapeDtypeStruct((B, T, width), F32), jax.ShapeDtypeStruct((B, T, width), F32)],
        scratch_shapes=[pltpu.VMEM((1, GATE_LANES), F32)],
        compiler_params=_params("parallel", "arbitrary"),
        name="fox_prep",
    )(h3, h3, h3, g3, q_gain.reshape(1, dh), k_gain.reshape(1, dh))


def _lane_tiles_reduce(x, op):
    acc = x[:, :128]
    for j in range(1, x.shape[1] // 128):
        acc = op(acc, x[:, j * 128:(j + 1) * 128])
    return acc


def _fox_flash_kernel(q_ref, k_ref, v_ref, z_ref, o_ref, m_sc, l_sc, acc_sc, mx_sc, s_sc, p_sc, *, t, sub):
    nq = q_ref.shape[0] // t
    ns = t // sub
    row_l = lax.broadcasted_iota(jnp.int32, (sub, sub), 0)
    col_l = lax.broadcasted_iota(jnp.int32, (sub, sub), 1)

    def block(diag, q0, k0):
        n_chunks = lambda i: i + 1 if diag else ns
        for i in range(ns):
            rows = slice(i * sub, (i + 1) * sub)
            q_i = q_ref[pl.ds(q0 + i * sub, sub), :]
            mx = None
            for c in range(n_chunks(i)):
                cs = slice(c * sub, (c + 1) * sub)
                s = lax.dot_general(q_i, k_ref[pl.ds(k0 + c * sub, sub), :], NT_DIMS,
                                    preferred_element_type=F32)
                if diag and c == i:
                    s = jnp.where(col_l <= row_l, s, -jnp.inf)
                s_sc[rows, cs] = s
                cm = _lane_tiles_reduce(s, jnp.maximum)
                mx = cm if mx is None else jnp.maximum(mx, cm)
            mx_sc[rows, :] = mx
        for i in range(ns):
            rows = slice(i * sub, (i + 1) * sub)
            m_prev = m_sc[rows, :]
            m_new = jnp.maximum(m_prev, jnp.max(mx_sc[rows, :], axis=1, keepdims=True))
            alpha = jnp.exp2(m_prev - m_new)
            ls = None
            for c in range(n_chunks(i)):
                cs = slice(c * sub, (c + 1) * sub)
                p = jnp.exp2(s_sc[rows, cs] - m_new)
                p_sc[rows, cs] = p.astype(BF16)
                lp = _lane_tiles_reduce(p, jnp.add)
                ls = lp if ls is None else ls + lp
            w = n_chunks(i) * sub
            l_sc[rows, :] = alpha * l_sc[rows, :] + jnp.sum(ls, axis=1, keepdims=True)
            acc_sc[rows, :] = alpha * acc_sc[rows, :] + jnp.dot(p_sc[rows, :w], v_ref[pl.ds(k0, w), :],
                                                                preferred_element_type=F32)
            m_sc[rows, :] = m_new

    def q_tile(qi, carry):
        q0 = pl.multiple_of(qi * t, t)
        m_sc[...] = jnp.full_like(m_sc, -jnp.inf)
        l_sc[...] = jnp.zeros_like(l_sc)
        acc_sc[...] = jnp.zeros_like(acc_sc)

        def k_tile(ki, c):
            block(False, q0, pl.multiple_of(ki * t, t))
            return c

        lax.fori_loop(0, qi, k_tile, 0)
        block(True, q0, q0)
        zz = z_ref[pl.ds(q0, t), :]
        o_ref[pl.ds(q0, t), :] = (zz * _sigmoid(zz) * (acc_sc[...] / l_sc[...])).astype(o_ref.dtype)
        return carry

    lax.fori_loop(0, nq, q_tile, 0)


def _fox_flash(qa, ka, vb, h3, cols, *, dh, t, sub):
    B, T, W = vb.shape
    t = min(t, T)
    sub = min(sub, t)
    da = dh + N_AUG
    zo = cols["fz"] // dh
    head = lambda w, off=0: pl.BlockSpec((None, T, w), lambda b, h, off=off: (b, 0, off + h))
    return pl.pallas_call(
        functools.partial(_fox_flash_kernel, t=t, sub=sub),
        grid=(B, W // dh),
        in_specs=[head(da), head(da), head(dh), head(dh, zo)],
        out_specs=head(dh),
        out_shape=jax.ShapeDtypeStruct((B, T, W), BF16),
        scratch_shapes=[pltpu.VMEM((t, 1), F32), pltpu.VMEM((t, 1), F32), pltpu.VMEM((t, dh), F32),
                        pltpu.VMEM((t, 128), F32), pltpu.VMEM((t, t), F32), pltpu.VMEM((t, t), BF16)],
        compiler_params=_params("parallel", "parallel"),
        name="fox_flash",
    )(qa, ka, vb, h3)


def _decay_selector(P, nh):
    t_src = jnp.arange(P)[:, None]
    t_dst = jnp.arange(P * nh)[None, :] // nh
    later = (t_src > t_dst).astype(BF16)
    return jnp.concatenate([later, jnp.ones((P, 128), BF16)], axis=1)


def _fox_decode_kernel(pt_ref, q_ref, kn_ref, vn_ref, z_ref, lfn_ref, u_ref, *rest, scale, nh, G):
    k_refs, v_refs, lf_refs = rest[:G], rest[G:2 * G], rest[2 * G:3 * G]
    o_ref, m_sc, l_sc, acc_sc, carry_sc = rest[3 * G:]
    p = pl.program_id(1)
    q = q_ref[...]

    @pl.when(p == 0)
    def _():
        s_new = jnp.sum(q.astype(F32) * kn_ref[...], axis=1, keepdims=True) * scale
        m_sc[...] = s_new
        l_sc[...] = jnp.ones_like(l_sc)
        acc_sc[...] = vn_ref[...]
        carry_sc[...] = lfn_ref[...]

    P = k_refs[0].shape[0]
    n = P * nh
    sub = lax.broadcasted_iota(jnp.int32, (nh, n), 0)
    lane = lax.broadcasted_iota(jnp.int32, (nh, n), 1)
    diag = (lane & (nh - 1)) == sub
    carry = carry_sc[...]
    scores = []
    for g in range(G):
        k2 = k_refs[g][...].reshape(n, k_refs[g].shape[2]).astype(BF16)
        s = lax.dot_general(q, k2, NT_DIMS, preferred_element_type=F32) * scale
        l1, l2, l3 = _split3(lf_refs[g][...])
        zz = jnp.dot(jnp.concatenate([l1, l2, l3], axis=0), u_ref[...], preferred_element_type=F32)
        zs = zz[0:nh] + zz[nh:2 * nh] + zz[2 * nh:3 * nh]
        s = s + zs[:, :n] + jnp.tile(carry, (1, n // 128))
        carry = carry + zs[:, n:]
        scores.append(jnp.where(diag, s, -jnp.inf))
    carry_sc[...] = carry

    m_prev = m_sc[...]
    m_new = m_prev
    for s in scores:
        m_new = jnp.maximum(m_new, jnp.max(s, axis=1, keepdims=True))
    alpha = jnp.exp(m_prev - m_new)
    l_new = alpha * l_sc[...]
    acc = alpha * acc_sc[...]
    for g in range(G):
        pr = jnp.exp(scores[g] - m_new)
        v2 = v_refs[g][...].reshape(n, v_refs[g].shape[2]).astype(BF16)
        l_new = l_new + jnp.sum(pr, axis=1, keepdims=True)
        acc = acc + jnp.dot(pr.astype(BF16), v2, preferred_element_type=F32)
    l_sc[...] = l_new
    acc_sc[...] = acc
    m_sc[...] = m_new

    @pl.when(p == pl.num_programs(1) - 1)
    def _():
        zz = z_ref[...]
        o_ref[...] = zz * _sigmoid(zz) * (acc / l_new)


def _fox_decode(page_table, q3, kn3, vn3, z3, lfn3, k_pool, v_pool, lf_pool_t, *, dh, G):
    N, nh, _ = q3.shape
    n_pages = page_table.shape[1]
    P = k_pool.shape[1]
    G = math.gcd(G, n_pages)
    assert nh & (nh - 1) == 0
    u = _decay_selector(P, nh)
    per = lambda: pl.BlockSpec((None, nh, dh), lambda b, p, pt: (b, 0, 0))
    page = lambda g: (lambda b, p, pt, g=g: (pt[b, n_pages - 1 - (p * G + g)], 0, 0, 0))
    page3 = lambda g: (lambda b, p, pt, g=g: (pt[b, n_pages - 1 - (p * G + g)], 0, 0))
    grid_spec = pltpu.PrefetchScalarGridSpec(
        num_scalar_prefetch=1,
        grid=(N, n_pages // G),
        in_specs=[per(), per(), per(), per(),
                  pl.BlockSpec((None, nh, 128), lambda b, p, pt: (b, 0, 0)),
                  pl.BlockSpec(u.shape, lambda b, p, pt: (0, 0))]
                 + [pl.BlockSpec((None, P, nh, dh), page(g)) for g in range(G)]
                 + [pl.BlockSpec((None, P, nh, dh), page(g)) for g in range(G)]
                 + [pl.BlockSpec((None, nh, P), page3(g)) for g in range(G)],
        out_specs=pl.BlockSpec((None, nh, dh), lambda b, p, pt: (b, 0, 0)),
        scratch_shapes=[pltpu.VMEM((nh, 1), F32), pltpu.VMEM((nh, 1), F32), pltpu.VMEM((nh, dh), F32),
                        pltpu.VMEM((nh, 128), F32)],
    )
    return pl.pallas_call(
        functools.partial(_fox_decode_kernel, scale=dh ** -0.5, nh=nh, G=G),
        grid_spec=grid_spec,
        out_shape=jax.ShapeDtypeStruct((N, nh, dh), F32),
        compiler_params=_params("parallel", "arbitrary"),
        name="fox_decode",
    )(page_table, q3, kn3, vn3, z3, lfn3, u, *([k_pool] * G), *([v_pool] * G), *([lf_pool_t] * G))


def _mem_attn_kernel(q_ref, z_ref, k_ref, v_ref, qg_ref, a_ref, *, dh, scale):
    nh = q_ref.shape[1] // dh
    for h in range(nh):
        sl = slice(h * dh, (h + 1) * dh)
        q = _rms_rows(q_ref[:, sl], qg_ref[...]).astype(BF16)
        s = lax.dot_general(q, k_ref[:, sl].astype(BF16), NT_DIMS, preferred_element_type=F32) * scale
        e = jnp.exp(s - jnp.max(s, axis=1, keepdims=True))
        o = jnp.dot(e.astype(BF16), v_ref[:, sl].astype(BF16), preferred_element_type=F32)
        o = o / jnp.sum(e, axis=1, keepdims=True)
        zz = z_ref[:, sl]
        a_ref[:, sl] = (zz * _sigmoid(zz) * o).astype(a_ref.dtype)


def _mem_attn_prompt(h3, mk, mv, q_gain, cols, *, width, dh, tq):
    B, T, _ = h3.shape
    n_mem = mk.shape[1]
    tq = min(tq, T)
    spec = lambda off: pl.BlockSpec((None, tq, width), lambda b, i, off=off: (b, i, off // width))
    mem = pl.BlockSpec((None, n_mem, width), lambda b, i: (b, 0, 0))
    return pl.pallas_call(
        functools.partial(_mem_attn_kernel, dh=dh, scale=dh ** -0.5),
        grid=(B, T // tq),
        in_specs=[spec(cols["cq"]), spec(cols["cz"]), mem, mem, pl.BlockSpec((1, dh), lambda b, i: (0, 0))],
        out_specs=pl.BlockSpec((None, tq, width), lambda b, i: (b, i, 0)),
        out_shape=jax.ShapeDtypeStruct((B, T, width), BF16),
        compiler_params=_params("parallel", "parallel"),
        name="mem_attn",
    )(h3, h3, mk, mv, q_gain.reshape(1, dh))


def _mem_decode_kernel(q_ref, z_ref, k_ref, v_ref, qg_ref, o_ref, *, scale, nh):
    for s_i in range(q_ref.shape[0]):
        q = _rms_rows(q_ref[s_i], qg_ref[...]).astype(BF16)
        kp = k_ref[s_i]
        vp = v_ref[s_i]
        rows = kp.shape[0] * kp.shape[1]
        k2 = kp.reshape(rows, kp.shape[2]).astype(BF16)
        v2 = vp.reshape(rows, vp.shape[2]).astype(BF16)
        s = lax.dot_general(q, k2, NT_DIMS, preferred_element_type=F32) * scale
        sub = lax.broadcasted_iota(jnp.int32, s.shape, 0)
        lane = lax.broadcasted_iota(jnp.int32, s.shape, 1)
        s = jnp.where((lane & (nh - 1)) == sub, s, -jnp.inf)
        m = jnp.maximum(jnp.max(s, axis=1, keepdims=True), -1e30)
        e = jnp.exp(s - m)
        den = jnp.maximum(jnp.sum(e, axis=1, keepdims=True), 1e-30)
        o = jnp.dot(e.astype(BF16), v2, preferred_element_type=F32) / den
        zz = z_ref[s_i]
        o_ref[s_i] = zz * _sigmoid(zz) * o


def _mem_decode(q3, z3, k_cache, v_cache, q_gain, *, dh, samples_per_step):
    N, n_mem, nh, _ = k_cache.shape
    S = math.gcd(N, samples_per_step)
    assert nh & (nh - 1) == 0
    per = pl.BlockSpec((S, 8, dh), lambda b: (b, 0, 0))
    cache = pl.BlockSpec((S, n_mem, nh, dh), lambda b: (b, 0, 0, 0))
    return pl.pallas_call(
        functools.partial(_mem_decode_kernel, scale=dh ** -0.5, nh=nh),
        grid=(N // S,),
        in_specs=[per, per, cache, cache, pl.BlockSpec((1, dh), lambda b: (0, 0))],
        out_specs=per,
        out_shape=jax.ShapeDtypeStruct((N, 8, dh), F32),
        compiler_params=_params("parallel"),
        name="mem_decode",
    )(q3, z3, k_cache, v_cache, q_gain.reshape(1, dh))


def _merge_kernel(am_ref, af_ref, ac_ref, wm_ref, wf_ref, wc_ref, gm_ref, gf_ref, gc_ref, u_ref):
    d = lambda a, w: jnp.dot(a[...], w[...], preferred_element_type=F32)
    u = _sigmoid(gm_ref[...]) * d(am_ref, wm_ref)
    u = u + _sigmoid(gf_ref[...]) * d(af_ref, wf_ref)
    u = u + _sigmoid(gc_ref[...]) * d(ac_ref, wc_ref)
    u_ref[...] = u.astype(u_ref.dtype)


def _merge(am, af, ac, wm, wf, wc, h2, g_off, *, tm, tn):
    m, k = am.shape
    n = wm.shape[1]
    tm = min(tm, m)
    a_spec = pl.BlockSpec((tm, k), lambda i, j: (i, 0))
    w_spec = pl.BlockSpec((k, tn), lambda i, j: (0, j))
    g_spec = lambda b: pl.BlockSpec((tm, tn), lambda i, j, b=b: (i, (g_off + b * n) // tn + j))
    return pl.pallas_call(
        _merge_kernel,
        grid=(m // tm, n // tn),
        in_specs=[a_spec, a_spec, a_spec, w_spec, w_spec, w_spec, g_spec(0), g_spec(1), g_spec(2)],
        out_specs=pl.BlockSpec((tm, tn), lambda i, j: (i, j)),
        out_shape=jax.ShapeDtypeStruct((m, n), BF16),
        compiler_params=_params("parallel", "arbitrary"),
        name="merge",
    )(am, af, ac, wm, wf, wc, h2, h2, h2)


def _out_proj_kernel(u_ref, w_ref, x_ref, y_ref):
    y_ref[...] = x_ref[...] + jnp.dot(u_ref[...], w_ref[...], preferred_element_type=F32)


def _out_proj(u, w, x, *, tm, tn):
    m, k = u.shape
    n = w.shape[1]
    tm = min(tm, m)
    return pl.pallas_call(
        _out_proj_kernel,
        grid=(m // tm, n // tn),
        in_specs=[
            pl.BlockSpec((tm, k), lambda i, j: (i, 0)),
            pl.BlockSpec((k, tn), lambda i, j: (0, j)),
            pl.BlockSpec((tm, tn), lambda i, j: (i, j)),
        ],
        out_specs=pl.BlockSpec((tm, tn), lambda i, j: (i, j)),
        out_shape=jax.ShapeDtypeStruct((m, n), F32),
        compiler_params=_params("parallel", "arbitrary"),
        name="out_proj",
    )(u, w, x)


def _column_layout(d_model):
    ml_dv = d_model // ML_HEADS
    ml_qkw = ML_HEADS * (ml_dv // 2)
    names = ("mq", "mk", "mv", "mi", "mf", "mo", "mz", "fq", "fk", "fv", "ff", "fz", "cq", "cz", "g")
    widths = (ml_qkw, ml_qkw, d_model, ML_HEADS, ML_HEADS, d_model, d_model,
              d_model, d_model, d_model, FOX_HEADS, d_model, d_model, d_model, N_BRANCH * d_model)
    src, off = {}, 0
    for n, w in zip(names, widths):
        src[n] = (off, w)
        off += w
    big, boff = {}, 0
    for n in names:
        if n in ("mi", "mf", "ff"):
            continue
        big[n] = boff
        boff += src[n][1]
    return src, big, boff


def _layer(x_p, x_s, mem_p, fox_k_pool, fox_v_pool, fox_lf_pool, page_table, mem_k_s, mem_v_s,
           c0_s, n0_s, m0_s, norm_in, norm_mem, w_in, w_mem_kv, b_i, b_f, b_ff, ml_out_norm,
           fq_norm, fk_norm, cq_norm, ck_norm, w_br_m, w_br_f, w_br_c, w_out):
    B, T, D = x_p.shape
    N = x_s.shape[0]
    dv = D // ML_HEADS
    dqk = dv // 2
    fox_dh = D // FOX_HEADS
    mem_dh = D // MEM_HEADS
    src, cols, nbig = _column_layout(D)
    w_in_t = w_in.T
    gap1, gap2 = cols["mo"], cols["fz"]
    skip1, skip2 = 2 * ML_HEADS, 2 * ML_HEADS + FOX_HEADS

    def big_rows(tn):
        assert gap1 % tn == 0 and gap2 % tn == 0
        return lambda j: j * tn + jnp.where(j * tn >= gap2, skip2, jnp.where(j * tn >= gap1, skip1, 0))

    rows = lambda n: w_in_t[src[n][0]:src[n][0] + src[n][1]]
    n_gate = 2 * ML_HEADS + FOX_HEADS
    w_gate_t = jnp.concatenate([rows("mi"), rows("mf"), rows("ff"),
                                jnp.zeros((GATE_LANES - n_gate, D), F32)], axis=0)
    b_gate = jnp.concatenate([b_i, b_f, b_ff, jnp.zeros((GATE_LANES - n_gate,), F32)]).reshape(1, GATE_LANES)
    wm, wf, wc, wo = (w.astype(BF16) for w in (w_br_m, w_br_f, w_br_c, w_out))

    xp2 = x_p.reshape(B * T, D)
    xs2 = x_s.reshape(N, D)

    h2, hs2 = _norm_proj(xp2, norm_in, w_in_t, tm=1024, tn=1024, n_out=nbig, w_index=big_rows(1024),
                         transposed=True, x_rider=xs2)
    g2 = _gates(xp2, norm_in, w_gate_t, b_gate, tm=1024)
    h3 = h2.reshape(B, T, nbig)
    g3 = g2.reshape(B, T, GATE_LANES)

    n_mem = mem_p.shape[1]
    mem2 = mem_p.reshape(B * n_mem, D)
    heads_k = D // mem_dh
    mk_p, _ = _norm_proj(mem2, norm_mem, w_mem_kv, tm=512, tn=mem_dh, n_out=D, w_index=lambda j: j,
                         transposed=False, head_gain=ck_norm)
    mv_p, _ = _norm_proj(mem2, norm_mem, w_mem_kv, tm=512, tn=mem_dh, n_out=D, w_index=lambda j: heads_k + j,
                         transposed=False)

    a_m, c_p, n_p, m_p = _mlstm_prompt(h3, g3, ml_out_norm, cols, dqk=dqk, dv=dv,
                                       chunks_per_step=4, heads_per_step=4)

    qa, ka, vb, kn, fv_p = _fox_prep(h3, g3, fq_norm, fk_norm, cols, width=D, dh=fox_dh, tp=256)
    a_f = _fox_flash(qa, ka, vb, h3, cols, dh=fox_dh, t=1024, sub=256)

    a_c = _mem_attn_prompt(h3, mk_p.reshape(B, n_mem, D), mv_p.reshape(B, n_mem, D), cq_norm, cols,
                           width=D, dh=mem_dh, tq=512)

    u_p = _merge(a_m.reshape(B * T, D), a_f.reshape(B * T, D), a_c.reshape(B * T, D),
                 wm, wf, wc, h2, cols["g"], tm=1024, tn=256)
    y_p = _out_proj(u_p, wo, xp2, tm=1024, tn=1024).reshape(B, T, D)

    gs2 = _gates(xs2, norm_in, w_gate_t, b_gate, tm=128)
    hs3 = hs2.reshape(N, 1, nbig)
    gs3 = gs2.reshape(N, 1, GATE_LANES)

    a_ms, c_s, n_s, m_s = _mlstm_sample(hs3, gs3, m0_s.reshape(N, 1, ML_HEADS), ml_out_norm, c0_s, n0_s,
                                        cols, dqk=dqk, dv=dv, samples_per_step=4)

    hs1 = hs2.reshape(1, N, nbig)
    qn_s, kn_s = _qk_norm(hs1, fq_norm, fk_norm, cols, width=D, dh=fox_dh, tq=128)
    seg = lambda n: hs2[:, cols[n]:cols[n] + D]
    fv_s = seg("fv")
    flf_s = gs2[:, 2 * ML_HEADS:n_gate]
    lfn = jnp.broadcast_to(flf_s[:, :, None], (N, FOX_HEADS, 128))
    to_heads = lambda a: a.reshape(N, FOX_HEADS, fox_dh)
    o_f = _fox_decode(page_table, to_heads(qn_s.reshape(N, D)), to_heads(kn_s.reshape(N, D)), to_heads(fv_s),
                      to_heads(seg("fz")), lfn, fox_k_pool, fox_v_pool, jnp.swapaxes(fox_lf_pool, 1, 2),
                      dh=fox_dh, G=8)
    a_fs = o_f.reshape(N, D).astype(BF16)

    pad8 = lambda a: jnp.pad(a.reshape(N, MEM_HEADS, mem_dh), ((0, 0), (0, 8 - MEM_HEADS), (0, 0)))
    o_c = _mem_decode(pad8(seg("cq")), pad8(seg("cz")), mem_k_s, mem_v_s, cq_norm, dh=mem_dh, samples_per_step=4)
    a_cs = o_c[:, :MEM_HEADS].reshape(N, D).astype(BF16)

    u_s = _merge(a_ms.reshape(N, D), a_fs, a_cs, wm, wf, wc, hs2, cols["g"], tm=128, tn=512)
    y_s = _out_proj(u_s, wo, xs2, tm=128, tn=1024).reshape(N, 1, D)

    outs_p = (kn.reshape(B, T, FOX_HEADS, fox_dh), fv_p.reshape(B, T, FOX_HEADS, fox_dh),
              g3[:, :, 2 * ML_HEADS:n_gate],
              mk_p.reshape(B, n_mem, MEM_HEADS, mem_dh), mv_p.reshape(B, n_mem, MEM_HEADS, mem_dh),
              c_p, n_p.reshape(B, ML_HEADS, dqk), m_p.reshape(B, ML_HEADS))
    outs_s = (kn_s.reshape(N, 1, FOX_HEADS, fox_dh), fv_s.reshape(N, 1, FOX_HEADS, fox_dh),
              flf_s.reshape(N, 1, FOX_HEADS), c_s, n_s, m_s.reshape(N, ML_HEADS))
    return y_p, y_s, outs_p, outs_s


def kernel(x_prompt, x_sample, mem_prompt, cache_fox_k, cache_fox_v, cache_fox_logf, page_table, cache_mem_k, cache_mem_v, state_mlstm_C, state_mlstm_n, state_mlstm_m, norm_in, norm_mem, w_in, w_mem_kv, b_mlstm_i, b_mlstm_f, b_fox_f, mlstm_out_norm, fox_q_norm, fox_k_norm, mem_q_norm, mem_k_norm, w_br_mlstm, w_br_fox, w_br_mem, w_out):
    depth = w_in.shape[0]
    y_p, y_s = x_prompt, x_sample
    per_layer = []
    for l in range(depth):
        y_p, y_s, outs_p, outs_s = _layer(
            y_p, y_s, mem_prompt, cache_fox_k[l], cache_fox_v[l], cache_fox_logf[l], page_table,
            cache_mem_k[l], cache_mem_v[l], state_mlstm_C[l], state_mlstm_n[l], state_mlstm_m[l],
            norm_in[l], norm_mem[l], w_in[l], w_mem_kv[l], b_mlstm_i[l], b_mlstm_f[l], b_fox_f[l],
            mlstm_out_norm[l], fox_q_norm[l], fox_k_norm[l], mem_q_norm[l], mem_k_norm[l],
            w_br_mlstm[l], w_br_fox[l], w_br_mem[l], w_out[l])
        per_layer.append(outs_p + outs_s)
    stacked = tuple(jnp.stack([lay[i] for lay in per_layer]) for i in range(len(per_layer[0])))
    return (y_p, y_s) + stacked
```

```python
import functools
import math

import jax
import jax.numpy as jnp
from jax import lax
from jax.experimental import pallas as pl
from jax.experimental.pallas import tpu as pltpu

F32 = jnp.float32
BF16 = jnp.bfloat16

ML_HEADS = 4
FOX_HEADS = 16
MEM_HEADS = 4
N_BRANCH = 3
ML_CHUNK = 128
EPS = 1e-6
LOG2E = 1.4426950408889634
GATE_LANES = 128
VMEM_LIMIT = 56 * 1024 * 1024

NT_DIMS = (((1,), (1,)), ((), ()))
TN_DIMS = (((0,), (0,)), ((), ()))


def _params(*sem):
    return pltpu.CompilerParams(dimension_semantics=sem, vmem_limit_bytes=VMEM_LIMIT)


def _sigmoid(z):
    return 1.0 / (1.0 + jnp.exp(-z))


def _log_sigmoid(z):
    return jnp.minimum(z, 0.0) - jnp.log1p(jnp.exp(-jnp.abs(z)))


def _split3(x):
    x1 = x.astype(BF16)
    r1 = x - x1.astype(F32)
    x2 = r1.astype(BF16)
    r2 = r1 - x2.astype(F32)
    return x1, x2, r2.astype(BF16)


def _dot_sel_left(sel, x):
    x1, x2, x3 = _split3(x)
    d = lambda p: jnp.dot(sel, p, preferred_element_type=F32)
    return d(x1) + d(x2) + d(x3)


def _rms_rows(x, gain):
    return x * lax.rsqrt(jnp.mean(x * x, axis=-1, keepdims=True) + EPS) * gain


def _norm_proj_kernel(*refs, head_norm, transposed, rider):
    refs = list(refs)
    x_ref = refs.pop(0)
    xr_ref = refs.pop(0) if rider else None
    g_ref, w_ref = refs.pop(0), refs.pop(0)
    hg_ref = refs.pop(0) if head_norm else None
    o_ref = refs.pop(0)
    or_ref = refs.pop(0) if rider else None
    xn_ref, = refs
    tm = x_ref.shape[0]

    @pl.when(pl.program_id(1) == 0)
    def _():
        xn_ref[:tm, :] = _rms_rows(x_ref[...], g_ref[...]).astype(BF16)
        if rider:
            xn_ref[tm:, :] = _rms_rows(xr_ref[...], g_ref[...]).astype(BF16)

    w = w_ref[...].astype(BF16)
    if transposed:
        acc = lax.dot_general(xn_ref[...], w, NT_DIMS, preferred_element_type=F32)
    else:
        acc = jnp.dot(xn_ref[...], w, preferred_element_type=F32)
    if head_norm:
        acc = _rms_rows(acc, hg_ref[...])
    o_ref[...] = acc[:tm].astype(o_ref.dtype)
    if rider:
        or_ref[...] = acc[tm:].astype(or_ref.dtype)


def _norm_proj(x, gain, w, *, tm, tn, n_out, w_index, transposed, head_gain=None, x_rider=None):
    m, k = x.shape
    n = n_out
    tm = min(tm, m)
    assert m % tm == 0 and n % tn == 0
    rider = x_rider is not None
    tr = 0
    if rider:
        tr = x_rider.shape[0] // (m // tm)
        assert tr * (m // tm) == x_rider.shape[0] and tr % 16 == 0
    if transposed:
        w_spec = pl.BlockSpec((pl.Element(tn), pl.Element(k)), lambda i, j: (pl.multiple_of(w_index(j), 8), 0))
    else:
        w_spec = pl.BlockSpec((k, tn), lambda i, j: (0, w_index(j)))
    in_specs = [pl.BlockSpec((tm, k), lambda i, j: (i, 0), pipeline_mode=pl.Buffered(1))]
    args = [x]
    if rider:
        in_specs.append(pl.BlockSpec((tr, k), lambda i, j: (i, 0)))
        args.append(x_rider)
    in_specs += [pl.BlockSpec((1, k), lambda i, j: (0, 0)), w_spec]
    args += [gain.reshape(1, k), w]
    if head_gain is not None:
        assert head_gain.shape == (tn,)
        in_specs.append(pl.BlockSpec((1, tn), lambda i, j: (0, 0)))
        args.append(head_gain.reshape(1, tn))
    out_specs = [pl.BlockSpec((tm, tn), lambda i, j: (i, j))]
    out_shape = [jax.ShapeDtypeStruct((m, n), F32)]
    if rider:
        out_specs.append(pl.BlockSpec((tr, tn), lambda i, j: (i, j)))
        out_shape.append(jax.ShapeDtypeStruct((x_rider.shape[0], n), F32))
    outs = pl.pallas_call(
        functools.partial(_norm_proj_kernel, head_norm=head_gain is not None, transposed=transposed, rider=rider),
        grid=(m // tm, n // tn),
        in_specs=in_specs,
        out_specs=out_specs,
        out_shape=out_shape,
        scratch_shapes=[pltpu.VMEM((tm + tr, k), BF16)],
        compiler_params=_params("parallel", "arbitrary"),
        name="norm_proj",
    )(*args)
    return (outs[0], outs[1]) if rider else (outs[0], None)


def _gates_kernel(x_ref, g_ref, w_ref, b_ref, o_ref):
    xn = _rms_rows(x_ref[...], g_ref[...])
    x1, x2, _ = _split3(xn)
    w1, w2, _ = _split3(w_ref[...])
    d = lambda a, b: lax.dot_general(a, b, NT_DIMS, preferred_element_type=F32)
    z = d(x1, w1) + d(x1, w2) + d(x2, w1) + b_ref[...]
    lane = lax.broadcasted_iota(jnp.int32, z.shape, 1)
    o_ref[...] = jnp.where(lane < ML_HEADS, z, _log_sigmoid(z))


def _gates(x, gain, w_gate, bias, *, tm):
    m, k = x.shape
    tm = min(tm, m)
    return pl.pallas_call(
        _gates_kernel,
        grid=(m // tm,),
        in_specs=[
            pl.BlockSpec((tm, k), lambda i: (i, 0)),
            pl.BlockSpec((1, k), lambda i: (0, 0)),
            pl.BlockSpec((GATE_LANES, k), lambda i: (0, 0)),
            pl.BlockSpec((1, GATE_LANES), lambda i: (0, 0)),
        ],
        out_specs=pl.BlockSpec((tm, GATE_LANES), lambda i: (i, 0)),
        out_shape=jax.ShapeDtypeStruct((m, GATE_LANES), F32),
        compiler_params=_params("parallel"),
        name="gates",
    )(x, gain.reshape(1, k), w_gate, bias)


def _mlstm_chunk_kernel(q_ref, k_ref, v_ref, o_ref, z_ref, g_ref, gain_ref,
                        a_ref, c_out, n_out, m_out, c_sc, n_sc, m_sc, *, dqk, dv, L, HP):
    hp = pl.program_id(1)
    step = pl.program_id(2)
    n_chunks = q_ref.shape[0] // L

    @pl.when(step == 0)
    def _():
        c_sc[...] = jnp.zeros_like(c_sc)
        n_sc[...] = jnp.zeros_like(n_sc)
        m_sc[...] = jnp.zeros_like(m_sc)

    row = lax.broadcasted_iota(jnp.int32, (L, L), 0)
    col = lax.broadcasted_iota(jnp.int32, (L, L), 1)
    causal = col <= row
    tri = jnp.where(causal, 1.0, 0.0).astype(BF16)
    lane = lax.broadcasted_iota(jnp.int32, (L, GATE_LANES), 1)
    sub = lax.broadcasted_iota(jnp.int32, (GATE_LANES, L), 0)
    pick_col = lambda x, idx: jnp.sum(jnp.where(lane == idx, x, 0.0), axis=1, keepdims=True)
    pick_row = lambda x, idx: jnp.sum(jnp.where(sub == idx, x, 0.0), axis=0, keepdims=True)

    def chunk(ci, carry):
        rows = pl.ds(pl.multiple_of(ci * L, L), L)
        g = g_ref[rows, :]
        cum = _dot_sel_left(tri, g)
        g_t = g.T
        cum_t = cum.T
        for hh in range(HP):
            h = hp * HP + hh
            qs = slice(hh * dqk, (hh + 1) * dqk)
            vs = slice(hh * dv, (hh + 1) * dv)
            ig_col = pick_col(g, h)
            b_col = pick_col(cum, ML_HEADS + h)
            ig_row = pick_row(g_t, h)
            b_row = pick_row(cum_t, ML_HEADS + h)

            m_prev = m_sc[hh]
            a_col = b_col + m_prev
            dmat = jnp.where(causal, b_col - b_row + ig_row, -jnp.inf)
            m_t = jnp.maximum(a_col, jnp.max(dmat, axis=1, keepdims=True))
            w_inter = jnp.exp(a_col - m_t)
            w_intra = jnp.exp(dmat - m_t)

            qf = q_ref[rows, qs] * (dqk ** -0.5)
            kf = k_ref[rows, qs]
            vf = v_ref[rows, vs]
            q = qf.astype(BF16)
            k = kf.astype(BF16)
            v = vf.astype(BF16)
            c_prev = c_sc[hh]
            n_prev = n_sc[hh]

            s = lax.dot_general(q, k, NT_DIMS, preferred_element_type=F32) * w_intra
            inter = lax.dot_general(q, c_prev.astype(BF16), NT_DIMS, preferred_element_type=F32)
            num = w_inter * inter + jnp.dot(s.astype(BF16), v, preferred_element_type=F32)
            nq = w_inter * jnp.sum(qf * n_prev, axis=1, keepdims=True) + jnp.sum(s, axis=1, keepdims=True)
            hid = num / jnp.maximum(jnp.abs(nq), jnp.exp(-m_t))

            b_last = b_col[L - 1:L, :]
            m_new = m_t[L - 1:L, :]
            w_c = jnp.exp(b_last + m_prev - m_new)
            w_s = jnp.exp(b_last - b_col + ig_col - m_new)
            wv = (w_s * vf).astype(BF16)
            c_sc[hh] = w_c * c_prev + lax.dot_general(wv, k, TN_DIMS, preferred_element_type=F32)
            n_sc[hh] = w_c * n_prev + jnp.sum(w_s * kf, axis=0, keepdims=True)
            m_sc[hh] = m_new

            hn = _rms_rows(hid, gain_ref[:, vs])
            zz = z_ref[rows, vs]
            a_ref[rows, vs] = (zz * _sigmoid(zz) * (_sigmoid(o_ref[rows, vs]) * hn)).astype(a_ref.dtype)
        return carry

    lax.fori_loop(0, n_chunks, chunk, 0)

    @pl.when(step == pl.num_programs(2) - 1)
    def _():
        c_out[...] = c_sc[...]
        n_out[...] = n_sc[...]
        m_out[...] = m_sc[...]


def _mlstm_prompt(h3, g3, out_gain, cols, *, dqk, dv, chunks_per_step, heads_per_step):
    B, T, _ = h3.shape
    L = math.gcd(T, ML_CHUNK)
    rows = L * math.gcd(T // L, chunks_per_step)
    HP = math.gcd(ML_HEADS, heads_per_step)
    qo, ko, vo, oo, zo = (cols[n] for n in ("mq", "mk", "mv", "mo", "mz"))
    blk = lambda w, off: pl.BlockSpec((None, rows, HP * w),
                                      lambda b, h, c, off=off, w=w: (b, c, off // (HP * w) + h))
    kern = functools.partial(_mlstm_chunk_kernel, dqk=dqk, dv=dv, L=L, HP=HP)
    return pl.pallas_call(
        kern,
        grid=(B, ML_HEADS // HP, T // rows),
        in_specs=[
            blk(dqk, qo), blk(dqk, ko), blk(dv, vo), blk(dv, oo), blk(dv, zo),
            pl.BlockSpec((None, rows, GATE_LANES), lambda b, h, c: (b, c, 0)),
            pl.BlockSpec((1, HP * dv), lambda b, h, c: (0, h)),
        ],
        out_specs=[
            pl.BlockSpec((None, rows, HP * dv), lambda b, h, c: (b, c, h)),
            pl.BlockSpec((None, HP, dv, dqk), lambda b, h, c: (b, h, 0, 0)),
            pl.BlockSpec((None, HP, 1, dqk), lambda b, h, c: (b, h, 0, 0)),
            pl.BlockSpec((None, HP, 1, 1), lambda b, h, c: (b, h, 0, 0)),
        ],
        out_shape=[
            jax.ShapeDtypeStruct((B, T, ML_HEADS * dv), BF16),
            jax.ShapeDtypeStruct((B, ML_HEADS, dv, dqk), F32),
            jax.ShapeDtypeStruct((B, ML_HEADS, 1, dqk), F32),
            jax.ShapeDtypeStruct((B, ML_HEADS, 1, 1), F32),
        ],
        scratch_shapes=[pltpu.VMEM((HP, dv, dqk), F32), pltpu.VMEM((HP, 1, dqk), F32),
                        pltpu.VMEM((HP, 1, 1), F32)],
        compiler_params=_params("parallel", "parallel", "arbitrary"),
        name="mlstm_chunk",
    )(h3, h3, h3, h3, h3, g3, out_gain.reshape(1, ML_HEADS * dv))


def _mlstm_step_kernel(q_ref, k_ref, v_ref, o_ref, z_ref, g_ref, m0_ref, gain_ref, c0_ref, n0_ref,
                       a_ref, c_out, n_out, m_out, *, dqk, dv):
    lane_h = lax.broadcasted_iota(jnp.int32, (1, ML_HEADS), 1)
    first = lax.broadcasted_iota(jnp.int32, (16, 1), 0) == 0
    for s_i in range(q_ref.shape[0]):
        g = g_ref[s_i]
        m0 = m0_ref[s_i]
        m_acc = jnp.zeros((1, ML_HEADS), F32)
        for h in range(ML_HEADS):
            ig = g[:, h:h + 1]
            lf = g[:, ML_HEADS + h:ML_HEADS + h + 1]
            a = lf + m0[:, h:h + 1]
            m_t = jnp.maximum(a, ig)
            w_inter = jnp.exp(a - m_t)
            w_intra = jnp.exp(ig - m_t)
            qf = q_ref[s_i, :, h * dqk:(h + 1) * dqk] * (dqk ** -0.5)
            kf = k_ref[s_i, :, h * dqk:(h + 1) * dqk]
            vf = v_ref[s_i, :, h * dv:(h + 1) * dv]
            c0 = c0_ref[s_i, h]
            n0 = n0_ref[s_i, h:h + 1, :]
            s = jnp.sum(qf * kf, axis=1, keepdims=True) * w_intra
            q8 = jnp.broadcast_to(qf, (8, dqk)).astype(BF16)
            cq = lax.dot_general(q8, c0.astype(BF16), NT_DIMS, preferred_element_type=F32)[0:1, :]
            num = w_inter * cq + s * vf
            nq = w_inter * jnp.sum(n0 * qf, axis=1, keepdims=True) + s
            hid = num / jnp.maximum(jnp.abs(nq), jnp.exp(-m_t))
            v16 = jnp.where(first, vf, 0.0).astype(BF16)
            k16 = jnp.where(first, w_intra * kf, 0.0).astype(BF16)
            c_out[s_i, h] = w_inter * c0 + lax.dot_general(v16, k16, TN_DIMS, preferred_element_type=F32)
            n_out[s_i, h:h + 1, :] = w_inter * n0 + w_intra * kf
            m_acc = m_acc + jnp.where(lane_h == h, m_t, 0.0)
            hn = _rms_rows(hid, gain_ref[:, h * dv:(h + 1) * dv])
            zz = z_ref[s_i, :, h * dv:(h + 1) * dv]
            gate_o = _sigmoid(o_ref[s_i, :, h * dv:(h + 1) * dv])
            a_ref[s_i, :, h * dv:(h + 1) * dv] = (zz * _sigmoid(zz) * (gate_o * hn)).astype(a_ref.dtype)
        m_out[s_i] = m_acc


def _mlstm_sample(hs3, gs3, m0, out_gain, c0, n0, cols, *, dqk, dv, samples_per_step):
    N = hs3.shape[0]
    S = math.gcd(N, samples_per_step)
    W = ML_HEADS * dv
    WQ = ML_HEADS * dqk
    qo, ko, vo, oo, zo = (cols[n] for n in ("mq", "mk", "mv", "mo", "mz"))
    row = lambda w, off: pl.BlockSpec((S, 1, w), lambda b, off=off, w=w: (b, 0, off // w))
    kern = functools.partial(_mlstm_step_kernel, dqk=dqk, dv=dv)
    return pl.pallas_call(
        kern,
        grid=(N // S,),
        in_specs=[
            row(WQ, qo), row(WQ, ko), row(W, vo), row(W, oo), row(W, zo),
            pl.BlockSpec((S, 1, GATE_LANES), lambda b: (b, 0, 0)),
            pl.BlockSpec((S, 1, ML_HEADS), lambda b: (b, 0, 0)),
            pl.BlockSpec((1, W), lambda b: (0, 0)),
            pl.BlockSpec((S, ML_HEADS, dv, dqk), lambda b: (b, 0, 0, 0)),
            pl.BlockSpec((S, ML_HEADS, dqk), lambda b: (b, 0, 0)),
        ],
        out_specs=[
            pl.BlockSpec((S, 1, W), lambda b: (b, 0, 0)),
            pl.BlockSpec((S, ML_HEADS, dv, dqk), lambda b: (b, 0, 0, 0)),
            pl.BlockSpec((S, ML_HEADS, dqk), lambda b: (b, 0, 0)),
            pl.BlockSpec((S, 1, ML_HEADS), lambda b: (b, 0, 0)),
        ],
        out_shape=[
            jax.ShapeDtypeStruct((N, 1, W), BF16),
            jax.ShapeDtypeStruct((N, ML_HEADS, dv, dqk), F32),
            jax.ShapeDtypeStruct((N, ML_HEADS, dqk), F32),
            jax.ShapeDtypeStruct((N, 1, ML_HEADS), F32),
        ],
        compiler_params=_params("parallel"),
        name="mlstm_step",
    )(hs3, hs3, hs3, hs3, hs3, gs3, m0, out_gain.reshape(1, W), c0, n0)


N_AUG = 128


def _qk_norm_kernel(q_ref, k_ref, qg_ref, kg_ref, qn_ref, kn_ref, *, dh):
    nh = q_ref.shape[1] // dh
    for h in range(nh):
        sl = slice(h * dh, (h + 1) * dh)
        qn_ref[:, sl] = _rms_rows(q_ref[:, sl], qg_ref[...]).astype(qn_ref.dtype)
        kn_ref[:, sl] = _rms_rows(k_ref[:, sl], kg_ref[...])


def _qk_norm(h3, q_gain, k_gain, cols, *, width, dh, tq):
    B, T, _ = h3.shape
    tq = min(tq, T)
    spec = lambda off: pl.BlockSpec((None, tq, width), lambda b, i, off=off: (b, i, off // width))
    out = pl.BlockSpec((None, tq, width), lambda b, i: (b, i, 0))
    gain = pl.BlockSpec((1, dh), lambda b, i: (0, 0))
    return pl.pallas_call(
        functools.partial(_qk_norm_kernel, dh=dh),
        grid=(B, T // tq),
        in_specs=[spec(cols["fq"]), spec(cols["fk"]), gain, gain],
        out_specs=[out, out],
        out_shape=[jax.ShapeDtypeStruct((B, T, width), BF16), jax.ShapeDtypeStruct((B, T, width), F32)],
        compiler_params=_params("parallel", "parallel"),
        name="qk_norm",
    )(h3, h3, q_gain.reshape(1, dh), k_gain.reshape(1, dh))


def _aug_placement(nh, lane0):
    r = jnp.arange(3 * GATE_LANES)[None, :, None]
    c = jnp.arange(N_AUG)[None, None, :]
    src_lane = (lane0 + jnp.arange(nh))[:, None, None]
    piece = r // GATE_LANES
    hit = (r % GATE_LANES) == src_lane
    pq = jnp.where(hit & (c == piece + 3), 1.0, 0.0)
    pk = jnp.where(hit & (c == piece), -1.0, 0.0)
    return jnp.stack([pq, pk], axis=1).astype(BF16)


def _fox_prep_kernel(q_ref, k_ref, v_ref, g_ref, qg_ref, kg_ref, place_ref,
                     qa_ref, ka_ref, vb_ref, kn_ref, vf_ref, carry_sc, *, dh, qscale):
    tp = q_ref.shape[0]
    nh = q_ref.shape[1] // dh

    @pl.when(pl.program_id(1) == 0)
    def _():
        carry_sc[...] = jnp.zeros_like(carry_sc)

    row = lax.broadcasted_iota(jnp.int32, (tp, tp), 0)
    col = lax.broadcasted_iota(jnp.int32, (tp, tp), 1)
    tri = jnp.where(col <= row, 1.0, 0.0).astype(BF16)
    cum = _dot_sel_left(tri, g_ref[...]) + carry_sc[...]
    carry_sc[...] = cum[tp - 1:tp, :]
    f123 = jnp.concatenate(_split3(cum * LOG2E), axis=1)

    v = v_ref[...]
    vf_ref[...] = v
    vb_ref[...] = v.astype(BF16)
    lane = lax.broadcasted_iota(jnp.int32, (tp, N_AUG), 1)
    q_ones = lane < 3
    k_ones = jnp.abs(lane - 4) <= 1
    for h in range(nh):
        sl = slice(h * dh, (h + 1) * dh)
        kn = _rms_rows(k_ref[:, sl], kg_ref[...])
        kn_ref[:, sl] = kn
        qn = _rms_rows(q_ref[:, sl], qg_ref[...]) * qscale
        eq = jnp.where(q_ones, 1.0, jnp.dot(f123, place_ref[h, 0], preferred_element_type=F32))
        ek = jnp.where(k_ones, 1.0, jnp.dot(f123, place_ref[h, 1], preferred_element_type=F32))
        base = h * (dh + N_AUG)
        qa_ref[:, base:base + dh] = qn.astype(BF16)
        qa_ref[:, base + dh:base + dh + N_AUG] = eq.astype(BF16)
        ka_ref[:, base:base + dh] = kn.astype(BF16)
        ka_ref[:, base + dh:base + dh + N_AUG] = ek.astype(BF16)


def _fox_prep(h3, g3, q_gain, k_gain, cols, *, width, dh, tp):
    B, T, _ = h3.shape
    tp = min(tp, T)
    nh = width // dh
    wa = nh * (dh + N_AUG)
    place = _aug_placement(nh, 2 * ML_HEADS)
    spec = lambda off: pl.BlockSpec((None, tp, width), lambda b, i, off=off: (b, i, off // width))
    out = lambda w: pl.BlockSpec((None, tp, w), lambda b, i: (b, i, 0))
    gain = pl.BlockSpec((1, dh), lambda b, i: (0, 0))
    return pl.pallas_call(
        functools.partial(_fox_prep_kernel, dh=dh, qscale=dh ** -0.5 * LOG2E),
        grid=(B, T // tp),
        in_specs=[spec(cols["fq"]), spec(cols["fk"]), spec(cols["fv"]),
                  pl.BlockSpec((None, tp, GATE_LANES), lambda b, i: (b, i, 0)), gain, gain,
                  pl.BlockSpec(place.shape, lambda b, i: (0, 0, 0, 0))],
        out_specs=[out(wa), out(wa), out(width), out(width), out(width)],
        out_shape=[jax.ShapeDtypeStruct((B, T, wa), BF16), jax.ShapeDtypeStruct((B, T, wa), BF16),
                   jax.ShapeDtypeStruct((B, T, width), BF16),
                   jax.ShapeDtypeStruct((B, T, width), F32), jax.ShapeDtypeStruct((B, T, width), F32)],
        scratch_shapes=[pltpu.VMEM((1, GATE_LANES), F32)],
        compiler_params=_params("parallel", "arbitrary"),
        name="fox_prep",
    )(h3, h3, h3, g3, q_gain.reshape(1, dh), k_gain.reshape(1, dh), place)


def _lane_tiles_reduce(x, op):
    acc = x[:, :128]
    for j in range(1, x.shape[1] // 128):
        acc = op(acc, x[:, j * 128:(j + 1) * 128])
    return acc


def _fox_flash_kernel(q_ref, k_ref, v_ref, z_ref, o_ref, m_sc, l_sc, acc_sc, mx_sc, s_sc, p_sc, *, t, sub):
    nq = q_ref.shape[0] // t
    ns = t // sub
    row_l = lax.broadcasted_iota(jnp.int32, (sub, sub), 0)
    col_l = lax.broadcasted_iota(jnp.int32, (sub, sub), 1)

    def block(diag, q0, k0):
        n_chunks = lambda i: i + 1 if diag else ns
        for i in range(ns):
            rows = slice(i * sub, (i + 1) * sub)
            q_i = q_ref[pl.ds(q0 + i * sub, sub), :]
            mx = None
            for c in range(n_chunks(i)):
                cs = slice(c * sub, (c + 1) * sub)
                s = lax.dot_general(q_i, k_ref[pl.ds(k0 + c * sub, sub), :], NT_DIMS,
                                    preferred_element_type=F32)
                if diag and c == i:
                    s = jnp.where(col_l <= row_l, s, -jnp.inf)
                s_sc[rows, cs] = s
                cm = _lane_tiles_reduce(s, jnp.maximum)
                mx = cm if mx is None else jnp.maximum(mx, cm)
            mx_sc[rows, :] = mx
        for i in range(ns):
            rows = slice(i * sub, (i + 1) * sub)
            m_prev = m_sc[rows, :]
            m_new = jnp.maximum(m_prev, jnp.max(mx_sc[rows, :], axis=1, keepdims=True))
            alpha = jnp.exp2(m_prev - m_new)
            m_wide = jnp.tile(m_new, (1, sub // 128))
            ls = None
            for c in range(n_chunks(i)):
                cs = slice(c * sub, (c + 1) * sub)
                p = jnp.exp2(s_sc[rows, cs] - m_wide)
                p_sc[rows, cs] = p.astype(BF16)
                lp = _lane_tiles_reduce(p, jnp.add)
                ls = lp if ls is None else ls + lp
            w = n_chunks(i) * sub
            l_sc[rows, :] = alpha * l_sc[rows, :] + jnp.sum(ls, axis=1, keepdims=True)
            acc_sc[rows, :] = alpha * acc_sc[rows, :] + jnp.dot(p_sc[rows, :w], v_ref[pl.ds(k0, w), :],
                                                                preferred_element_type=F32)
            m_sc[rows, :] = m_new

    def q_tile(qi, carry):
        q0 = pl.multiple_of(qi * t, t)
        m_sc[...] = jnp.full_like(m_sc, -jnp.inf)
        l_sc[...] = jnp.zeros_like(l_sc)
        acc_sc[...] = jnp.zeros_like(acc_sc)

        def k_tile(ki, c):
            block(False, q0, pl.multiple_of(ki * t, t))
            return c

        lax.fori_loop(0, qi, k_tile, 0)

        def diag_tile(ki, c):
            block(True, q0, q0)
            return c

        lax.fori_loop(qi, qi + 1, diag_tile, 0)
        zz = z_ref[pl.ds(q0, t), :]
        o_ref[pl.ds(q0, t), :] = (zz * _sigmoid(zz) * (acc_sc[...] / l_sc[...])).astype(o_ref.dtype)
        return carry

    lax.fori_loop(0, nq, q_tile, 0)


def _fox_flash(qa, ka, vb, h3, cols, *, dh, t, sub):
    B, T, W = vb.shape
    assert dh == 128
    t = min(t, T)
    sub = min(sub, t)
    da = dh + N_AUG
    zo = cols["fz"] // dh
    head = lambda w, off=0: pl.BlockSpec((None, T, w), lambda b, h, off=off: (b, 0, off + h))
    return pl.pallas_call(
        functools.partial(_fox_flash_kernel, t=t, sub=sub),
        grid=(B, W // dh),
        in_specs=[head(da), head(da), head(dh), head(dh, zo)],
        out_specs=head(dh),
        out_shape=jax.ShapeDtypeStruct((B, T, W), BF16),
        scratch_shapes=[pltpu.VMEM((t, 128), F32), pltpu.VMEM((t, 128), F32), pltpu.VMEM((t, dh), F32),
                        pltpu.VMEM((t, 128), F32), pltpu.VMEM((t, t), F32), pltpu.VMEM((t, t), BF16)],
        compiler_params=_params("parallel", "parallel"),
        name="fox_flash",
    )(qa, ka, vb, h3)


def _decay_selector(P, nh):
    t_src = jnp.arange(P)[:, None]
    t_dst = jnp.arange(P * nh)[None, :] // nh
    later = (t_src > t_dst).astype(BF16)
    return jnp.concatenate([later, jnp.ones((P, 128), BF16)], axis=1)


def _fox_decode_kernel(pt_ref, q_ref, kn_ref, vn_ref, z_ref, lfn_ref, u_ref, *rest, scale, nh, G):
    k_refs, v_refs, lf_refs = rest[:G], rest[G:2 * G], rest[2 * G:3 * G]
    o_ref, m_sc, l_sc, acc_sc, carry_sc = rest[3 * G:]
    p = pl.program_id(1)
    q = q_ref[...]

    @pl.when(p == 0)
    def _():
        s_new = jnp.sum(q.astype(F32) * kn_ref[...], axis=1, keepdims=True) * scale
        m_sc[...] = s_new
        l_sc[...] = jnp.ones_like(l_sc)
        acc_sc[...] = vn_ref[...]
        carry_sc[...] = lfn_ref[...]

    P = k_refs[0].shape[0]
    n = P * nh
    sub = lax.broadcasted_iota(jnp.int32, (nh, n), 0)
    lane = lax.broadcasted_iota(jnp.int32, (nh, n), 1)
    diag = (lane & (nh - 1)) == sub
    carry = carry_sc[...]
    scores = []
    for g in range(G):
        k2 = k_refs[g][...].reshape(n, k_refs[g].shape[2]).astype(BF16)
        s = lax.dot_general(q, k2, NT_DIMS, preferred_element_type=F32) * scale
        l1, l2, l3 = _split3(lf_refs[g][...])
        zz = jnp.dot(jnp.concatenate([l1, l2, l3], axis=0), u_ref[...], preferred_element_type=F32)
        zs = zz[0:nh] + zz[nh:2 * nh] + zz[2 * nh:3 * nh]
        s = s + zs[:, :n] + jnp.tile(carry, (1, n // 128))
        carry = carry + zs[:, n:]
        scores.append(jnp.where(diag, s, -jnp.inf))
    carry_sc[...] = carry

    m_prev = m_sc[...]
    m_new = m_prev
    for s in scores:
        m_new = jnp.maximum(m_new, jnp.max(s, axis=1, keepdims=True))
    alpha = jnp.exp(m_prev - m_new)
    l_new = alpha * l_sc[...]
    acc = alpha * acc_sc[...]
    for g in range(G):
        pr = jnp.exp(scores[g] - m_new)
        v2 = v_refs[g][...].reshape(n, v_refs[g].shape[2]).astype(BF16)
        l_new = l_new + jnp.sum(pr, axis=1, keepdims=True)
        acc = acc + jnp.dot(pr.astype(BF16), v2, preferred_element_type=F32)
    l_sc[...] = l_new
    acc_sc[...] = acc
    m_sc[...] = m_new

    @pl.when(p == pl.num_programs(1) - 1)
    def _():
        zz = z_ref[...]
        o_ref[...] = zz * _sigmoid(zz) * (acc / l_new)


def _fox_decode(page_table, q3, kn3, vn3, z3, lfn3, k_pool, v_pool, lf_pool_t, *, dh, G):
    N, nh, _ = q3.shape
    n_pages = page_table.shape[1]
    P = k_pool.shape[1]
    G = math.gcd(G, n_pages)
    assert nh & (nh - 1) == 0
    u = _decay_selector(P, nh)
    per = lambda: pl.BlockSpec((None, nh, dh), lambda b, p, pt: (b, 0, 0))
    page = lambda g: (lambda b, p, pt, g=g: (pt[b, n_pages - 1 - (p * G + g)], 0, 0, 0))
    page3 = lambda g: (lambda b, p, pt, g=g: (pt[b, n_pages - 1 - (p * G + g)], 0, 0))
    grid_spec = pltpu.PrefetchScalarGridSpec(
        num_scalar_prefetch=1,
        grid=(N, n_pages // G),
        in_specs=[per(), per(), per(), per(),
                  pl.BlockSpec((None, nh, 128), lambda b, p, pt: (b, 0, 0)),
                  pl.BlockSpec(u.shape, lambda b, p, pt: (0, 0))]
                 + [pl.BlockSpec((None, P, nh, dh), page(g)) for g in range(G)]
                 + [pl.BlockSpec((None, P, nh, dh), page(g)) for g in range(G)]
                 + [pl.BlockSpec((None, nh, P), page3(g)) for g in range(G)],
        out_specs=pl.BlockSpec((None, nh, dh), lambda b, p, pt: (b, 0, 0)),
        scratch_shapes=[pltpu.VMEM((nh, 1), F32), pltpu.VMEM((nh, 1), F32), pltpu.VMEM((nh, dh), F32),
                        pltpu.VMEM((nh, 128), F32)],
    )
    return pl.pallas_call(
        functools.partial(_fox_decode_kernel, scale=dh ** -0.5, nh=nh, G=G),
        grid_spec=grid_spec,
        out_shape=jax.ShapeDtypeStruct((N, nh, dh), F32),
        compiler_params=_params("parallel", "arbitrary"),
        name="fox_decode",
    )(page_table, q3, kn3, vn3, z3, lfn3, u, *([k_pool] * G), *([v_pool] * G), *([lf_pool_t] * G))


def _mem_attn_kernel(q_ref, z_ref, k_ref, v_ref, qg_ref, a_ref, *, dh, scale):
    nh = q_ref.shape[1] // dh
    for h in range(nh):
        sl = slice(h * dh, (h + 1) * dh)
        q = _rms_rows(q_ref[:, sl], qg_ref[...]).astype(BF16)
        s = lax.dot_general(q, k_ref[:, sl].astype(BF16), NT_DIMS, preferred_element_type=F32) * scale
        e = jnp.exp(s - jnp.max(s, axis=1, keepdims=True))
        o = jnp.dot(e.astype(BF16), v_ref[:, sl].astype(BF16), preferred_element_type=F32)
        o = o / jnp.sum(e, axis=1, keepdims=True)
        zz = z_ref[:, sl]
        a_ref[:, sl] = (zz * _sigmoid(zz) * o).astype(a_ref.dtype)


def _mem_attn_prompt(h3, mk, mv, q_gain, cols, *, width, dh, tq):
    B, T, _ = h3.shape
    n_mem = mk.shape[1]
    tq = min(tq, T)
    spec = lambda off: pl.BlockSpec((None, tq, width), lambda b, i, off=off: (b, i, off // width))
    mem = pl.BlockSpec((None, n_mem, width), lambda b, i: (b, 0, 0))
    return pl.pallas_call(
        functools.partial(_mem_attn_kernel, dh=dh, scale=dh ** -0.5),
        grid=(B, T // tq),
        in_specs=[spec(cols["cq"]), spec(cols["cz"]), mem, mem, pl.BlockSpec((1, dh), lambda b, i: (0, 0))],
        out_specs=pl.BlockSpec((None, tq, width), lambda b, i: (b, i, 0)),
        out_shape=jax.ShapeDtypeStruct((B, T, width), BF16),
        compiler_params=_params("parallel", "parallel"),
        name="mem_attn",
    )(h3, h3, mk, mv, q_gain.reshape(1, dh))


def _mem_decode_kernel(q_ref, z_ref, k_ref, v_ref, qg_ref, o_ref, *, scale, nh):
    for s_i in range(q_ref.shape[0]):
        q = _rms_rows(q_ref[s_i], qg_ref[...]).astype(BF16)
        kp = k_ref[s_i]
        vp = v_ref[s_i]
        rows = kp.shape[0] * kp.shape[1]
        k2 = kp.reshape(rows, kp.shape[2]).astype(BF16)
        v2 = vp.reshape(rows, vp.shape[2]).astype(BF16)
        s = lax.dot_general(q, k2, NT_DIMS, preferred_element_type=F32) * scale
        sub = lax.broadcasted_iota(jnp.int32, s.shape, 0)
        lane = lax.broadcasted_iota(jnp.int32, s.shape, 1)
        s = jnp.where((lane & (nh - 1)) == sub, s, -jnp.inf)
        m = jnp.maximum(jnp.max(s, axis=1, keepdims=True), -1e30)
        e = jnp.exp(s - m)
        den = jnp.maximum(jnp.sum(e, axis=1, keepdims=True), 1e-30)
        o = jnp.dot(e.astype(BF16), v2, preferred_element_type=F32) / den
        zz = z_ref[s_i]
        o_ref[s_i] = zz * _sigmoid(zz) * o


def _mem_decode(q3, z3, k_cache, v_cache, q_gain, *, dh, samples_per_step):
    N, n_mem, nh, _ = k_cache.shape
    S = math.gcd(N, samples_per_step)
    assert nh & (nh - 1) == 0
    per = pl.BlockSpec((S, 8, dh), lambda b: (b, 0, 0))
    cache = pl.BlockSpec((S, n_mem, nh, dh), lambda b: (b, 0, 0, 0))
    return pl.pallas_call(
        functools.partial(_mem_decode_kernel, scale=dh ** -0.5, nh=nh),
        grid=(N // S,),
        in_specs=[per, per, cache, cache, pl.BlockSpec((1, dh), lambda b: (0, 0))],
        out_specs=per,
        out_shape=jax.ShapeDtypeStruct((N, 8, dh), F32),
        compiler_params=_params("parallel"),
        name="mem_decode",
    )(q3, z3, k_cache, v_cache, q_gain.reshape(1, dh))


def _merge_kernel(am_ref, af_ref, ac_ref, wm_ref, wf_ref, wc_ref, gm_ref, gf_ref, gc_ref, u_ref):
    d = lambda a, w: jnp.dot(a[...], w[...], preferred_element_type=F32)
    u = _sigmoid(gm_ref[...]) * d(am_ref, wm_ref)
    u = u + _sigmoid(gf_ref[...]) * d(af_ref, wf_ref)
    u = u + _sigmoid(gc_ref[...]) * d(ac_ref, wc_ref)
    u_ref[...] = u.astype(u_ref.dtype)


def _merge(am, af, ac, wm, wf, wc, h2, g_off, *, tm, tn):
    m, k = am.shape
    n = wm.shape[1]
    tm = min(tm, m)
    a_spec = pl.BlockSpec((tm, k), lambda i, j: (i, 0))
    w_spec = pl.BlockSpec((k, tn), lambda i, j: (0, j))
    g_spec = lambda b: pl.BlockSpec((tm, tn), lambda i, j, b=b: (i, (g_off + b * n) // tn + j))
    return pl.pallas_call(
        _merge_kernel,
        grid=(m // tm, n // tn),
        in_specs=[a_spec, a_spec, a_spec, w_spec, w_spec, w_spec, g_spec(0), g_spec(1), g_spec(2)],
        out_specs=pl.BlockSpec((tm, tn), lambda i, j: (i, j)),
        out_shape=jax.ShapeDtypeStruct((m, n), BF16),
        compiler_params=_params("parallel", "arbitrary"),
        name="merge",
    )(am, af, ac, wm, wf, wc, h2, h2, h2)


def _out_proj_kernel(u_ref, w_ref, x_ref, y_ref):
    y_ref[...] = x_ref[...] + jnp.dot(u_ref[...], w_ref[...], preferred_element_type=F32)


def _out_proj(u, w, x, *, tm, tn):
    m, k = u.shape
    n = w.shape[1]
    tm = min(tm, m)
    return pl.pallas_call(
        _out_proj_kernel,
        grid=(m // tm, n // tn),
        in_specs=[
            pl.BlockSpec((tm, k), lambda i, j: (i, 0)),
            pl.BlockSpec((k, tn), lambda i, j: (0, j)),
            pl.BlockSpec((tm, tn), lambda i, j: (i, j)),
        ],
        out_specs=pl.BlockSpec((tm, tn), lambda i, j: (i, j)),
        out_shape=jax.ShapeDtypeStruct((m, n), F32),
        compiler_params=_params("parallel", "arbitrary"),
        name="out_proj",
    )(u, w, x)


def _column_layout(d_model):
    ml_dv = d_model // ML_HEADS
    ml_qkw = ML_HEADS * (ml_dv // 2)
    names = ("mq", "mk", "mv", "mi", "mf", "mo", "mz", "fq", "fk", "fv", "ff", "fz", "cq", "cz", "g")
    widths = (ml_qkw, ml_qkw, d_model, ML_HEADS, ML_HEADS, d_model, d_model,
              d_model, d_model, d_model, FOX_HEADS, d_model, d_model, d_model, N_BRANCH * d_model)
    src, off = {}, 0
    for n, w in zip(names, widths):
        src[n] = (off, w)
        off += w
    big, boff = {}, 0
    for n in names:
        if n in ("mi", "mf", "ff"):
            continue
        big[n] = boff
        boff += src[n][1]
    return src, big, boff


def _layer(x_p, x_s, mem_p, fox_k_pool, fox_v_pool, fox_lf_pool, page_table, mem_k_s, mem_v_s,
           c0_s, n0_s, m0_s, norm_in, norm_mem, w_in, w_mem_kv, b_i, b_f, b_ff, ml_out_norm,
           fq_norm, fk_norm, cq_norm, ck_norm, w_br_m, w_br_f, w_br_c, w_out):
    B, T, D = x_p.shape
    N = x_s.shape[0]
    dv = D // ML_HEADS
    dqk = dv // 2
    fox_dh = D // FOX_HEADS
    mem_dh = D // MEM_HEADS
    src, cols, nbig = _column_layout(D)
    w_in_t = w_in.T
    gap1, gap2 = cols["mo"], cols["fz"]
    skip1, skip2 = 2 * ML_HEADS, 2 * ML_HEADS + FOX_HEADS

    def big_rows(tn):
        assert gap1 % tn == 0 and gap2 % tn == 0
        return lambda j: j * tn + jnp.where(j * tn >= gap2, skip2, jnp.where(j * tn >= gap1, skip1, 0))

    rows = lambda n: w_in_t[src[n][0]:src[n][0] + src[n][1]]
    n_gate = 2 * ML_HEADS + FOX_HEADS
    w_gate_t = jnp.concatenate([rows("mi"), rows("mf"), rows("ff"),
                                jnp.zeros((GATE_LANES - n_gate, D), F32)], axis=0)
    b_gate = jnp.concatenate([b_i, b_f, b_ff, jnp.zeros((GATE_LANES - n_gate,), F32)]).reshape(1, GATE_LANES)
    wm, wf, wc, wo = (w.astype(BF16) for w in (w_br_m, w_br_f, w_br_c, w_out))

    xp2 = x_p.reshape(B * T, D)
    xs2 = x_s.reshape(N, D)

    h2, hs2 = _norm_proj(xp2, norm_in, w_in_t, tm=2048, tn=512, n_out=nbig, w_index=big_rows(512),
                         transposed=True, x_rider=xs2)
    g2 = _gates(xp2, norm_in, w_gate_t, b_gate, tm=1024)
    h3 = h2.reshape(B, T, nbig)
    g3 = g2.reshape(B, T, GATE_LANES)

    n_mem = mem_p.shape[1]
    mem2 = mem_p.reshape(B * n_mem, D)
    heads_k = D // mem_dh
    mk_p, _ = _norm_proj(mem2, norm_mem, w_mem_kv, tm=512, tn=mem_dh, n_out=D, w_index=lambda j: j,
                         transposed=False, head_gain=ck_norm)
    mv_p, _ = _norm_proj(mem2, norm_mem, w_mem_kv, tm=512, tn=mem_dh, n_out=D, w_index=lambda j: heads_k + j,
                         transposed=False)

    a_m, c_p, n_p, m_p = _mlstm_prompt(h3, g3, ml_out_norm, cols, dqk=dqk, dv=dv,
                                       chunks_per_step=4, heads_per_step=4)

    qa, ka, vb, kn, fv_p = _fox_prep(h3, g3, fq_norm, fk_norm, cols, width=D, dh=fox_dh, tp=256)
    a_f = _fox_flash(qa, ka, vb, h3, cols, dh=fox_dh, t=1024, sub=256)

    a_c = _mem_attn_prompt(h3, mk_p.reshape(B, n_mem, D), mv_p.reshape(B, n_mem, D), cq_norm, cols,
                           width=D, dh=mem_dh, tq=512)

    u_p = _merge(a_m.reshape(B * T, D), a_f.reshape(B * T, D), a_c.reshape(B * T, D),
                 wm, wf, wc, h2, cols["g"], tm=1024, tn=512)
    y_p = _out_proj(u_p, wo, xp2, tm=1024, tn=1024).reshape(B, T, D)

    gs2 = _gates(xs2, norm_in, w_gate_t, b_gate, tm=128)
    hs3 = hs2.reshape(N, 1, nbig)
    gs3 = gs2.reshape(N, 1, GATE_LANES)

    a_ms, c_s, n_s, m_s = _mlstm_sample(hs3, gs3, m0_s.reshape(N, 1, ML_HEADS), ml_out_norm, c0_s, n0_s,
                                        cols, dqk=dqk, dv=dv, samples_per_step=4)

    hs1 = hs2.reshape(1, N, nbig)
    qn_s, kn_s = _qk_norm(hs1, fq_norm, fk_norm, cols, width=D, dh=fox_dh, tq=128)
    seg = lambda n: hs2[:, cols[n]:cols[n] + D]
    fv_s = seg("fv")
    flf_s = gs2[:, 2 * ML_HEADS:n_gate]
    lfn = jnp.broadcast_to(flf_s[:, :, None], (N, FOX_HEADS, 128))
    to_heads = lambda a: a.reshape(N, FOX_HEADS, fox_dh)
    o_f = _fox_decode(page_table, to_heads(qn_s.reshape(N, D)), to_heads(kn_s.reshape(N, D)), to_heads(fv_s),
                      to_heads(seg("fz")), lfn, fox_k_pool, fox_v_pool, jnp.swapaxes(fox_lf_pool, 1, 2),
                      dh=fox_dh, G=8)
    a_fs = o_f.reshape(N, D).astype(BF16)

    pad8 = lambda a: jnp.pad(a.reshape(N, MEM_HEADS, mem_dh), ((0, 0), (0, 8 - MEM_HEADS), (0, 0)))
    o_c = _mem_decode(pad8(seg("cq")), pad8(seg("cz")), mem_k_s, mem_v_s, cq_norm, dh=mem_dh, samples_per_step=4)
    a_cs = o_c[:, :MEM_HEADS].reshape(N, D).astype(BF16)

    u_s = _merge(a_ms.reshape(N, D), a_fs, a_cs, wm, wf, wc, hs2, cols["g"], tm=128, tn=512)
    y_s = _out_proj(u_s, wo, xs2, tm=128, tn=1024).reshape(N, 1, D)

    outs_p = (kn.reshape(B, T, FOX_HEADS, fox_dh), fv_p.reshape(B, T, FOX_HEADS, fox_dh),
              g3[:, :, 2 * ML_HEADS:n_gate],
              mk_p.reshape(B, n_mem, MEM_HEADS, mem_dh), mv_p.reshape(B, n_mem, MEM_HEADS, mem_dh),
              c_p, n_p.reshape(B, ML_HEADS, dqk), m_p.reshape(B, ML_HEADS))
    outs_s = (kn_s.reshape(N, 1, FOX_HEADS, fox_dh), fv_s.reshape(N, 1, FOX_HEADS, fox_dh),
              flf_s.reshape(N, 1, FOX_HEADS), c_s, n_s, m_s.reshape(N, ML_HEADS))
    return y_p, y_s, outs_p, outs_s


def kernel(x_prompt, x_sample, mem_prompt, cache_fox_k, cache_fox_v, cache_fox_logf, page_table, cache_mem_k, cache_mem_v, state_mlstm_C, state_mlstm_n, state_mlstm_m, norm_in, norm_mem, w_in, w_mem_kv, b_mlstm_i, b_mlstm_f, b_fox_f, mlstm_out_norm, fox_q_norm, fox_k_norm, mem_q_norm, mem_k_norm, w_br_mlstm, w_br_fox, w_br_mem, w_out):
    depth = w_in.shape[0]
    y_p, y_s = x_prompt, x_sample
    per_layer = []
    for l in range(depth):
        y_p, y_s, outs_p, outs_s = _layer(
            y_p, y_s, mem_prompt, cache_fox_k[l], cache_fox_v[l], cache_fox_logf[l], page_table,
            cache_mem_k[l], cache_mem_v[l], state_mlstm_C[l], state_mlstm_n[l], state_mlstm_m[l],
            norm_in[l], norm_mem[l], w_in[l], w_mem_kv[l], b_mlstm_i[l], b_mlstm_f[l], b_fox_f[l],
            mlstm_out_norm[l], fox_q_norm[l], fox_k_norm[l], mem_q_norm[l], mem_k_norm[l],
            w_br_mlstm[l], w_br_fox[l], w_br_mem[l], w_out[l])
        per_layer.append(outs_p + outs_s)
    stacked = tuple(jnp.stack([lay[i] for lay in per_layer]) for i in range(len(per_layer[0])))
    return (y_p, y_s) + stacked
```

```python
import functools
import math

import jax
import jax.numpy as jnp
from jax import lax
from jax.experimental import pallas as pl
from jax.experimental.pallas import tpu as pltpu

F32 = jnp.float32
BF16 = jnp.bfloat16

ML_HEADS = 4
FOX_HEADS = 16
MEM_HEADS = 4
N_BRANCH = 3
ML_CHUNK = 128
EPS = 1e-6
LOG2E = 1.4426950408889634
GATE_LANES = 128
VMEM_LIMIT = 56 * 1024 * 1024

NT_DIMS = (((1,), (1,)), ((), ()))
TN_DIMS = (((0,), (0,)), ((), ()))


def _params(*sem):
    return pltpu.CompilerParams(dimension_semantics=sem, vmem_limit_bytes=VMEM_LIMIT)


def _sigmoid(z):
    return 1.0 / (1.0 + jnp.exp(-z))


def _log_sigmoid(z):
    return jnp.minimum(z, 0.0) - jnp.log1p(jnp.exp(-jnp.abs(z)))


def _split3(x):
    x1 = x.astype(BF16)
    r1 = x - x1.astype(F32)
    x2 = r1.astype(BF16)
    r2 = r1 - x2.astype(F32)
    return x1, x2, r2.astype(BF16)


def _dot_sel_left(sel, x):
    x1, x2, x3 = _split3(x)
    d = lambda p: jnp.dot(sel, p, preferred_element_type=F32)
    return d(x1) + d(x2) + d(x3)


def _rms_rows(x, gain):
    return x * lax.rsqrt(jnp.mean(x * x, axis=-1, keepdims=True) + EPS) * gain


def _norm_proj_kernel(*refs, head_norm, transposed, rider):
    refs = list(refs)
    x_ref = refs.pop(0)
    xr_ref = refs.pop(0) if rider else None
    g_ref, w_ref = refs.pop(0), refs.pop(0)
    hg_ref = refs.pop(0) if head_norm else None
    o_ref = refs.pop(0)
    or_ref = refs.pop(0) if rider else None
    xn_ref, = refs
    tm = x_ref.shape[0]

    @pl.when(pl.program_id(1) == 0)
    def _():
        xn_ref[:tm, :] = _rms_rows(x_ref[...], g_ref[...]).astype(BF16)
        if rider:
            xn_ref[tm:, :] = _rms_rows(xr_ref[...], g_ref[...]).astype(BF16)

    w = w_ref[...].astype(BF16)
    if transposed:
        acc = lax.dot_general(xn_ref[...], w, NT_DIMS, preferred_element_type=F32)
    else:
        acc = jnp.dot(xn_ref[...], w, preferred_element_type=F32)
    if head_norm:
        acc = _rms_rows(acc, hg_ref[...])
    o_ref[...] = acc[:tm].astype(o_ref.dtype)
    if rider:
        or_ref[...] = acc[tm:].astype(or_ref.dtype)


def _norm_proj(x, gain, w, *, tm, tn, n_out, w_index, transposed, head_gain=None, x_rider=None):
    m, k = x.shape
    n = n_out
    tm = min(tm, m)
    assert m % tm == 0 and n % tn == 0
    rider = x_rider is not None
    tr = 0
    if rider:
        tr = x_rider.shape[0] // (m // tm)
        assert tr * (m // tm) == x_rider.shape[0] and tr % 16 == 0
    if transposed:
        w_spec = pl.BlockSpec((pl.Element(tn), pl.Element(k)), lambda i, j: (pl.multiple_of(w_index(j), 8), 0))
    else:
        w_spec = pl.BlockSpec((k, tn), lambda i, j: (0, w_index(j)))
    in_specs = [pl.BlockSpec((tm, k), lambda i, j: (i, 0), pipeline_mode=pl.Buffered(1))]
    args = [x]
    if rider:
        in_specs.append(pl.BlockSpec((tr, k), lambda i, j: (i, 0)))
        args.append(x_rider)
    in_specs += [pl.BlockSpec((1, k), lambda i, j: (0, 0)), w_spec]
    args += [gain.reshape(1, k), w]
    if head_gain is not None:
        assert head_gain.shape == (tn,)
        in_specs.append(pl.BlockSpec((1, tn), lambda i, j: (0, 0)))
        args.append(head_gain.reshape(1, tn))
    out_specs = [pl.BlockSpec((tm, tn), lambda i, j: (i, j))]
    out_shape = [jax.ShapeDtypeStruct((m, n), F32)]
    if rider:
        out_specs.append(pl.BlockSpec((tr, tn), lambda i, j: (i, j)))
        out_shape.append(jax.ShapeDtypeStruct((x_rider.shape[0], n), F32))
    outs = pl.pallas_call(
        functools.partial(_norm_proj_kernel, head_norm=head_gain is not None, transposed=transposed, rider=rider),
        grid=(m // tm, n // tn),
        in_specs=in_specs,
        out_specs=out_specs,
        out_shape=out_shape,
        scratch_shapes=[pltpu.VMEM((tm + tr, k), BF16)],
        compiler_params=_params("parallel", "arbitrary"),
        name="norm_proj",
    )(*args)
    return (outs[0], outs[1]) if rider else (outs[0], None)


def _gates_kernel(x_ref, g_ref, w_ref, b_ref, o_ref):
    xn = _rms_rows(x_ref[...], g_ref[...])
    x1, x2, _ = _split3(xn)
    w1, w2, _ = _split3(w_ref[...])
    d = lambda a, b: lax.dot_general(a, b, NT_DIMS, preferred_element_type=F32)
    z = d(x1, w1) + d(x1, w2) + d(x2, w1) + b_ref[...]
    lane = lax.broadcasted_iota(jnp.int32, z.shape, 1)
    o_ref[...] = jnp.where(lane < ML_HEADS, z, _log_sigmoid(z))


def _gates(x, gain, w_gate, bias, *, tm):
    m, k = x.shape
    tm = min(tm, m)
    return pl.pallas_call(
        _gates_kernel,
        grid=(m // tm,),
        in_specs=[
            pl.BlockSpec((tm, k), lambda i: (i, 0)),
            pl.BlockSpec((1, k), lambda i: (0, 0)),
            pl.BlockSpec((GATE_LANES, k), lambda i: (0, 0)),
            pl.BlockSpec((1, GATE_LANES), lambda i: (0, 0)),
        ],
        out_specs=pl.BlockSpec((tm, GATE_LANES), lambda i: (i, 0)),
        out_shape=jax.ShapeDtypeStruct((m, GATE_LANES), F32),
        compiler_params=_params("parallel"),
        name="gates",
    )(x, gain.reshape(1, k), w_gate, bias)


def _mlstm_chunk_kernel(q_ref, k_ref, v_ref, o_ref, z_ref, g_ref, gain_ref,
                        a_ref, c_out, n_out, m_out, c_sc, n_sc, m_sc, *, dqk, dv, L, HP):
    hp = pl.program_id(1)
    step = pl.program_id(2)
    n_chunks = q_ref.shape[0] // L

    @pl.when(step == 0)
    def _():
        c_sc[...] = jnp.zeros_like(c_sc)
        n_sc[...] = jnp.zeros_like(n_sc)
        m_sc[...] = jnp.zeros_like(m_sc)

    row = lax.broadcasted_iota(jnp.int32, (L, L), 0)
    col = lax.broadcasted_iota(jnp.int32, (L, L), 1)
    causal = col <= row
    tri = jnp.where(causal, 1.0, 0.0).astype(BF16)
    lane = lax.broadcasted_iota(jnp.int32, (L, GATE_LANES), 1)
    sub = lax.broadcasted_iota(jnp.int32, (GATE_LANES, L), 0)
    pick_col = lambda x, idx: jnp.sum(jnp.where(lane == idx, x, 0.0), axis=1, keepdims=True)
    pick_row = lambda x, idx: jnp.sum(jnp.where(sub == idx, x, 0.0), axis=0, keepdims=True)

    def chunk(ci, carry):
        rows = pl.ds(pl.multiple_of(ci * L, L), L)
        g = g_ref[rows, :]
        cum = _dot_sel_left(tri, g)
        g_t = g.T
        cum_t = cum.T
        for hh in range(HP):
            h = hp * HP + hh
            qs = slice(hh * dqk, (hh + 1) * dqk)
            vs = slice(hh * dv, (hh + 1) * dv)
            ig_col = pick_col(g, h)
            b_col = pick_col(cum, ML_HEADS + h)
            ig_row = pick_row(g_t, h)
            b_row = pick_row(cum_t, ML_HEADS + h)

            m_prev = m_sc[hh]
            a_col = b_col + m_prev
            dmat = jnp.where(causal, b_col - b_row + ig_row, -jnp.inf)
            m_t = jnp.maximum(a_col, jnp.max(dmat, axis=1, keepdims=True))
            w_inter = jnp.exp(a_col - m_t)
            w_intra = jnp.exp(dmat - m_t)

            qf = q_ref[rows, qs] * (dqk ** -0.5)
            kf = k_ref[rows, qs]
            vf = v_ref[rows, vs]
            q = qf.astype(BF16)
            k = kf.astype(BF16)
            v = vf.astype(BF16)
            c_prev = c_sc[hh]
            n_prev = n_sc[hh]

            s = lax.dot_general(q, k, NT_DIMS, preferred_element_type=F32) * w_intra
            inter = lax.dot_general(q, c_prev.astype(BF16), NT_DIMS, preferred_element_type=F32)
            num = w_inter * inter + jnp.dot(s.astype(BF16), v, preferred_element_type=F32)
            nq = w_inter * jnp.sum(qf * n_prev, axis=1, keepdims=True) + jnp.sum(s, axis=1, keepdims=True)
            hid = num / jnp.maximum(jnp.abs(nq), jnp.exp(-m_t))

            b_last = b_col[L - 1:L, :]
            m_new = m_t[L - 1:L, :]
            w_c = jnp.exp(b_last + m_prev - m_new)
            w_s = jnp.exp(b_last - b_col + ig_col - m_new)
            wv = (w_s * vf).astype(BF16)
            c_sc[hh] = w_c * c_prev + lax.dot_general(wv, k, TN_DIMS, preferred_element_type=F32)
            n_sc[hh] = w_c * n_prev + jnp.sum(w_s * kf, axis=0, keepdims=True)
            m_sc[hh] = m_new

            hn = _rms_rows(hid, gain_ref[:, vs])
            zz = z_ref[rows, vs]
            a_ref[rows, vs] = (zz * _sigmoid(zz) * (_sigmoid(o_ref[rows, vs]) * hn)).astype(a_ref.dtype)
        return carry

    lax.fori_loop(0, n_chunks, chunk, 0)

    @pl.when(step == pl.num_programs(2) - 1)
    def _():
        c_out[...] = c_sc[...]
        n_out[...] = n_sc[...]
        m_out[...] = m_sc[...]


def _mlstm_prompt(h3, g3, out_gain, cols, *, dqk, dv, chunks_per_step, heads_per_step):
    B, T, _ = h3.shape
    L = math.gcd(T, ML_CHUNK)
    rows = L * math.gcd(T // L, chunks_per_step)
    HP = math.gcd(ML_HEADS, heads_per_step)
    qo, ko, vo, oo, zo = (cols[n] for n in ("mq", "mk", "mv", "mo", "mz"))
    blk = lambda w, off: pl.BlockSpec((None, rows, HP * w),
                                      lambda b, h, c, off=off, w=w: (b, c, off // (HP * w) + h))
    kern = functools.partial(_mlstm_chunk_kernel, dqk=dqk, dv=dv, L=L, HP=HP)
    return pl.pallas_call(
        kern,
        grid=(B, ML_HEADS // HP, T // rows),
        in_specs=[
            blk(dqk, qo), blk(dqk, ko), blk(dv, vo), blk(dv, oo), blk(dv, zo),
            pl.BlockSpec((None, rows, GATE_LANES), lambda b, h, c: (b, c, 0)),
            pl.BlockSpec((1, HP * dv), lambda b, h, c: (0, h)),
        ],
        out_specs=[
            pl.BlockSpec((None, rows, HP * dv), lambda b, h, c: (b, c, h)),
            pl.BlockSpec((None, HP, dv, dqk), lambda b, h, c: (b, h, 0, 0)),
            pl.BlockSpec((None, HP, 1, dqk), lambda b, h, c: (b, h, 0, 0)),
            pl.BlockSpec((None, HP, 1, 1), lambda b, h, c: (b, h, 0, 0)),
        ],
        out_shape=[
            jax.ShapeDtypeStruct((B, T, ML_HEADS * dv), BF16),
            jax.ShapeDtypeStruct((B, ML_HEADS, dv, dqk), F32),
            jax.ShapeDtypeStruct((B, ML_HEADS, 1, dqk), F32),
            jax.ShapeDtypeStruct((B, ML_HEADS, 1, 1), F32),
        ],
        scratch_shapes=[pltpu.VMEM((HP, dv, dqk), F32), pltpu.VMEM((HP, 1, dqk), F32),
                        pltpu.VMEM((HP, 1, 1), F32)],
        compiler_params=_params("parallel", "parallel", "arbitrary"),
        name="mlstm_chunk",
    )(h3, h3, h3, h3, h3, g3, out_gain.reshape(1, ML_HEADS * dv))


def _mlstm_step_kernel(q_ref, k_ref, v_ref, o_ref, z_ref, g_ref, m0_ref, gain_ref, c0_ref, n0_ref,
                       a_ref, c_out, n_out, m_out, *, dqk, dv):
    lane_h = lax.broadcasted_iota(jnp.int32, (1, ML_HEADS), 1)
    first = lax.broadcasted_iota(jnp.int32, (16, 1), 0) == 0
    for s_i in range(q_ref.shape[0]):
        g = g_ref[s_i]
        m0 = m0_ref[s_i]
        m_acc = jnp.zeros((1, ML_HEADS), F32)
        for h in range(ML_HEADS):
            ig = g[:, h:h + 1]
            lf = g[:, ML_HEADS + h:ML_HEADS + h + 1]
            a = lf + m0[:, h:h + 1]
            m_t = jnp.maximum(a, ig)
            w_inter = jnp.exp(a - m_t)
            w_intra = jnp.exp(ig - m_t)
            qf = q_ref[s_i, :, h * dqk:(h + 1) * dqk] * (dqk ** -0.5)
            kf = k_ref[s_i, :, h * dqk:(h + 1) * dqk]
            vf = v_ref[s_i, :, h * dv:(h + 1) * dv]
            c0 = c0_ref[s_i, h]
            n0 = n0_ref[s_i, h:h + 1, :]
            s = jnp.sum(qf * kf, axis=1, keepdims=True) * w_intra
            q8 = jnp.broadcast_to(qf, (8, dqk)).astype(BF16)
            cq = lax.dot_general(q8, c0.astype(BF16), NT_DIMS, preferred_element_type=F32)[0:1, :]
            num = w_inter * cq + s * vf
            nq = w_inter * jnp.sum(n0 * qf, axis=1, keepdims=True) + s
            hid = num / jnp.maximum(jnp.abs(nq), jnp.exp(-m_t))
            v16 = jnp.where(first, vf, 0.0).astype(BF16)
            k16 = jnp.where(first, w_intra * kf, 0.0).astype(BF16)
            c_out[s_i, h] = w_inter * c0 + lax.dot_general(v16, k16, TN_DIMS, preferred_element_type=F32)
            n_out[s_i, h:h + 1, :] = w_inter * n0 + w_intra * kf
            m_acc = m_acc + jnp.where(lane_h == h, m_t, 0.0)
            hn = _rms_rows(hid, gain_ref[:, h * dv:(h + 1) * dv])
            zz = z_ref[s_i, :, h * dv:(h + 1) * dv]
            gate_o = _sigmoid(o_ref[s_i, :, h * dv:(h + 1) * dv])
            a_ref[s_i, :, h * dv:(h + 1) * dv] = (zz * _sigmoid(zz) * (gate_o * hn)).astype(a_ref.dtype)
        m_out[s_i] = m_acc


def _mlstm_sample(hs3, gs3, m0, out_gain, c0, n0, cols, *, dqk, dv, samples_per_step):
    N = hs3.shape[0]
    S = math.gcd(N, samples_per_step)
    W = ML_HEADS * dv
    WQ = ML_HEADS * dqk
    qo, ko, vo, oo, zo = (cols[n] for n in ("mq", "mk", "mv", "mo", "mz"))
    row = lambda w, off: pl.BlockSpec((S, 1, w), lambda b, off=off, w=w: (b, 0, off // w))
    kern = functools.partial(_mlstm_step_kernel, dqk=dqk, dv=dv)
    return pl.pallas_call(
        kern,
        grid=(N // S,),
        in_specs=[
            row(WQ, qo), row(WQ, ko), row(W, vo), row(W, oo), row(W, zo),
            pl.BlockSpec((S, 1, GATE_LANES), lambda b: (b, 0, 0)),
            pl.BlockSpec((S, 1, ML_HEADS), lambda b: (b, 0, 0)),
            pl.BlockSpec((1, W), lambda b: (0, 0)),
            pl.BlockSpec((S, ML_HEADS, dv, dqk), lambda b: (b, 0, 0, 0)),
            pl.BlockSpec((S, ML_HEADS, dqk), lambda b: (b, 0, 0)),
        ],
        out_specs=[
            pl.BlockSpec((S, 1, W), lambda b: (b, 0, 0)),
            pl.BlockSpec((S, ML_HEADS, dv, dqk), lambda b: (b, 0, 0, 0)),
            pl.BlockSpec((S, ML_HEADS, dqk), lambda b: (b, 0, 0)),
            pl.BlockSpec((S, 1, ML_HEADS), lambda b: (b, 0, 0)),
        ],
        out_shape=[
            jax.ShapeDtypeStruct((N, 1, W), BF16),
            jax.ShapeDtypeStruct((N, ML_HEADS, dv, dqk), F32),
            jax.ShapeDtypeStruct((N, ML_HEADS, dqk), F32),
            jax.ShapeDtypeStruct((N, 1, ML_HEADS), F32),
        ],
        compiler_params=_params("parallel"),
        name="mlstm_step",
    )(hs3, hs3, hs3, hs3, hs3, gs3, m0, out_gain.reshape(1, W), c0, n0)


N_AUG = 128


def _qk_norm_kernel(q_ref, k_ref, qg_ref, kg_ref, qn_ref, kn_ref, *, dh):
    nh = q_ref.shape[1] // dh
    for h in range(nh):
        sl = slice(h * dh, (h + 1) * dh)
        qn_ref[:, sl] = _rms_rows(q_ref[:, sl], qg_ref[...]).astype(qn_ref.dtype)
        kn_ref[:, sl] = _rms_rows(k_ref[:, sl], kg_ref[...])


def _qk_norm(h3, q_gain, k_gain, cols, *, width, dh, tq):
    B, T, _ = h3.shape
    tq = min(tq, T)
    spec = lambda off: pl.BlockSpec((None, tq, width), lambda b, i, off=off: (b, i, off // width))
    out = pl.BlockSpec((None, tq, width), lambda b, i: (b, i, 0))
    gain = pl.BlockSpec((1, dh), lambda b, i: (0, 0))
    return pl.pallas_call(
        functools.partial(_qk_norm_kernel, dh=dh),
        grid=(B, T // tq),
        in_specs=[spec(cols["fq"]), spec(cols["fk"]), gain, gain],
        out_specs=[out, out],
        out_shape=[jax.ShapeDtypeStruct((B, T, width), BF16), jax.ShapeDtypeStruct((B, T, width), F32)],
        compiler_params=_params("parallel", "parallel"),
        name="qk_norm",
    )(h3, h3, q_gain.reshape(1, dh), k_gain.reshape(1, dh))


def _aug_placement(nh, lane0):
    r = jnp.arange(3 * GATE_LANES)[None, :, None]
    c = jnp.arange(N_AUG)[None, None, :]
    src_lane = (lane0 + jnp.arange(nh))[:, None, None]
    piece = r // GATE_LANES
    hit = (r % GATE_LANES) == src_lane
    pq = jnp.where(hit & (c == piece + 3), 1.0, 0.0)
    pk = jnp.where(hit & (c == piece), -1.0, 0.0)
    return jnp.stack([pq, pk], axis=1).astype(BF16)


def _fox_prep_kernel(q_ref, k_ref, v_ref, g_ref, qg_ref, kg_ref, place_ref,
                     qa_ref, ka_ref, vb_ref, kn_ref, vf_ref, carry_sc, *, dh, qscale):
    tp = q_ref.shape[0]
    nh = q_ref.shape[1] // dh

    @pl.when(pl.program_id(1) == 0)
    def _():
        carry_sc[...] = jnp.zeros_like(carry_sc)

    row = lax.broadcasted_iota(jnp.int32, (tp, tp), 0)
    col = lax.broadcasted_iota(jnp.int32, (tp, tp), 1)
    tri = jnp.where(col <= row, 1.0, 0.0).astype(BF16)
    cum = _dot_sel_left(tri, g_ref[...]) + carry_sc[...]
    carry_sc[...] = cum[tp - 1:tp, :]
    f123 = jnp.concatenate(_split3(cum * LOG2E), axis=1)

    v = v_ref[...]
    vf_ref[...] = v
    vb_ref[...] = v.astype(BF16)
    lane = lax.broadcasted_iota(jnp.int32, (tp, N_AUG), 1)
    q_ones = lane < 3
    k_ones = jnp.abs(lane - 4) <= 1
    for h in range(nh):
        sl = slice(h * dh, (h + 1) * dh)
        kn = _rms_rows(k_ref[:, sl], kg_ref[...])
        kn_ref[:, sl] = kn
        qn = _rms_rows(q_ref[:, sl], qg_ref[...]) * qscale
        eq = jnp.where(q_ones, 1.0, jnp.dot(f123, place_ref[h, 0], preferred_element_type=F32))
        ek = jnp.where(k_ones, 1.0, jnp.dot(f123, place_ref[h, 1], preferred_element_type=F32))
        base = h * (dh + N_AUG)
        qa_ref[:, base:base + dh] = qn.astype(BF16)
        qa_ref[:, base + dh:base + dh + N_AUG] = eq.astype(BF16)
        ka_ref[:, base:base + dh] = kn.astype(BF16)
        ka_ref[:, base + dh:base + dh + N_AUG] = ek.astype(BF16)


def _fox_prep(h3, g3, q_gain, k_gain, cols, *, width, dh, tp):
    B, T, _ = h3.shape
    tp = min(tp, T)
    nh = width // dh
    wa = nh * (dh + N_AUG)
    place = _aug_placement(nh, 2 * ML_HEADS)
    spec = lambda off: pl.BlockSpec((None, tp, width), lambda b, i, off=off: (b, i, off // width))
    out = lambda w: pl.BlockSpec((None, tp, w), lambda b, i: (b, i, 0))
    gain = pl.BlockSpec((1, dh), lambda b, i: (0, 0))
    return pl.pallas_call(
        functools.partial(_fox_prep_kernel, dh=dh, qscale=dh ** -0.5 * LOG2E),
        grid=(B, T // tp),
        in_specs=[spec(cols["fq"]), spec(cols["fk"]), spec(cols["fv"]),
                  pl.BlockSpec((None, tp, GATE_LANES), lambda b, i: (b, i, 0)), gain, gain,
                  pl.BlockSpec(place.shape, lambda b, i: (0, 0, 0, 0))],
        out_specs=[out(wa), out(wa), out(width), out(width), out(width)],
        out_shape=[jax.ShapeDtypeStruct((B, T, wa), BF16), jax.ShapeDtypeStruct((B, T, wa), BF16),
                   jax.ShapeDtypeStruct((B, T, width), BF16),
                   jax.ShapeDtypeStruct((B, T, width), F32), jax.ShapeDtypeStruct((B, T, width), F32)],
        scratch_shapes=[pltpu.VMEM((1, GATE_LANES), F32)],
        compiler_params=_params("parallel", "arbitrary"),
        name="fox_prep",
    )(h3, h3, h3, g3, q_gain.reshape(1, dh), k_gain.reshape(1, dh), place)


def _lane_tiles_reduce(x, op):
    acc = x[:, :128]
    for j in range(1, x.shape[1] // 128):
        acc = op(acc, x[:, j * 128:(j + 1) * 128])
    return acc


def _fox_flash_kernel(q_ref, k_ref, v_ref, z_ref, o_ref, m_sc, l_sc, acc_sc, mx_sc, s_sc, p_sc,
                      *, t, sub, dh, HP):
    nq = q_ref.shape[0] // t
    ns = t // sub
    da = dh + N_AUG
    row_l = lax.broadcasted_iota(jnp.int32, (sub, sub), 0)
    col_l = lax.broadcasted_iota(jnp.int32, (sub, sub), 1)

    def block(diag, q0, k0):
        n_chunks = lambda i: i + 1 if diag else ns
        for hh in range(HP):
            ha = slice(hh * da, (hh + 1) * da)
            for i in range(ns):
                rows = slice(i * sub, (i + 1) * sub)
                q_i = q_ref[pl.ds(q0 + i * sub, sub), ha]
                mx = None
                for c in range(n_chunks(i)):
                    cs = slice(c * sub, (c + 1) * sub)
                    s = lax.dot_general(q_i, k_ref[pl.ds(k0 + c * sub, sub), ha], NT_DIMS,
                                        preferred_element_type=F32)
                    if diag and c == i:
                        s = jnp.where(col_l <= row_l, s, -jnp.inf)
                    s_sc[hh, rows, cs] = s
                    cm = _lane_tiles_reduce(s, jnp.maximum)
                    mx = cm if mx is None else jnp.maximum(mx, cm)
                mx_sc[hh, rows, :] = mx
        for hh in range(HP):
            hd = slice(hh * dh, (hh + 1) * dh)
            for i in range(ns):
                rows = slice(i * sub, (i + 1) * sub)
                m_prev = m_sc[hh, rows, :]
                m_new = jnp.maximum(m_prev, jnp.max(mx_sc[hh, rows, :], axis=1, keepdims=True))
                alpha = jnp.exp2(m_prev - m_new)
                m_wide = jnp.tile(m_new, (1, sub // 128))
                ls = None
                for c in range(n_chunks(i)):
                    cs = slice(c * sub, (c + 1) * sub)
                    p = jnp.exp2(s_sc[hh, rows, cs] - m_wide)
                    p_sc[hh, rows, cs] = p.astype(BF16)
                    lp = _lane_tiles_reduce(p, jnp.add)
                    ls = lp if ls is None else ls + lp
                w = n_chunks(i) * sub
                l_sc[hh, rows, :] = alpha * l_sc[hh, rows, :] + jnp.sum(ls, axis=1, keepdims=True)
                acc_sc[hh, rows, :] = alpha * acc_sc[hh, rows, :] + jnp.dot(
                    p_sc[hh, rows, :w], v_ref[pl.ds(k0, w), hd], preferred_element_type=F32)
                m_sc[hh, rows, :] = m_new

    def q_tile(qi, carry):
        q0 = pl.multiple_of(qi * t, t)
        m_sc[...] = jnp.full_like(m_sc, -jnp.inf)
        l_sc[...] = jnp.zeros_like(l_sc)
        acc_sc[...] = jnp.zeros_like(acc_sc)

        def k_tile(ki, c):
            block(False, q0, pl.multiple_of(ki * t, t))
            return c

        lax.fori_loop(0, qi, k_tile, 0)

        def diag_tile(ki, c):
            block(True, q0, q0)
            return c

        lax.fori_loop(qi, qi + 1, diag_tile, 0)
        for hh in range(HP):
            hd = slice(hh * dh, (hh + 1) * dh)
            zz = z_ref[pl.ds(q0, t), hd]
            o_ref[pl.ds(q0, t), hd] = (zz * _sigmoid(zz) * (acc_sc[hh] / l_sc[hh])).astype(o_ref.dtype)
        return carry

    lax.fori_loop(0, nq, q_tile, 0)


def _fox_flash(qa, ka, vb, h3, cols, *, dh, t, sub, heads_per_step):
    B, T, W = vb.shape
    assert dh == 128
    t = min(t, T)
    sub = min(sub, t)
    nh = W // dh
    HP = math.gcd(nh, heads_per_step)
    da = dh + N_AUG
    zo = cols["fz"] // (HP * dh)
    head = lambda w, off=0: pl.BlockSpec((None, T, HP * w), lambda b, h, off=off: (b, 0, off + h))
    return pl.pallas_call(
        functools.partial(_fox_flash_kernel, t=t, sub=sub, dh=dh, HP=HP),
        grid=(B, nh // HP),
        in_specs=[head(da), head(da), head(dh), head(dh, zo)],
        out_specs=head(dh),
        out_shape=jax.ShapeDtypeStruct((B, T, W), BF16),
        scratch_shapes=[pltpu.VMEM((HP, t, 128), F32), pltpu.VMEM((HP, t, 128), F32),
                        pltpu.VMEM((HP, t, dh), F32), pltpu.VMEM((HP, t, 128), F32),
                        pltpu.VMEM((HP, t, t), F32), pltpu.VMEM((HP, t, t), BF16)],
        compiler_params=_params("parallel", "parallel"),
        name="fox_flash",
    )(qa, ka, vb, h3)


def _decay_selector(P, nh):
    t_src = jnp.arange(P)[:, None]
    t_dst = jnp.arange(P * nh)[None, :] // nh
    later = (t_src > t_dst).astype(BF16)
    return jnp.concatenate([later, jnp.ones((P, 128), BF16)], axis=1)


def _fox_decode_kernel(pt_ref, q_ref, kn_ref, vn_ref, z_ref, lfn_ref, u_ref, *rest, scale, nh, G):
    k_refs, v_refs, lf_refs = rest[:G], rest[G:2 * G], rest[2 * G:3 * G]
    o_ref, m_sc, l_sc, acc_sc, carry_sc = rest[3 * G:]
    p = pl.program_id(1)
    q = q_ref[...]

    @pl.when(p == 0)
    def _():
        s_new = jnp.sum(q.astype(F32) * kn_ref[...], axis=1, keepdims=True) * scale
        m_sc[...] = s_new
        l_sc[...] = jnp.ones_like(l_sc)
        acc_sc[...] = vn_ref[...]
        carry_sc[...] = lfn_ref[...]

    P = k_refs[0].shape[0]
    n = P * nh
    sub = lax.broadcasted_iota(jnp.int32, (nh, n), 0)
    lane = lax.broadcasted_iota(jnp.int32, (nh, n), 1)
    diag = (lane & (nh - 1)) == sub
    carry = carry_sc[...]
    scores = []
    for g in range(G):
        k2 = k_refs[g][...].reshape(n, k_refs[g].shape[2]).astype(BF16)
        s = lax.dot_general(q, k2, NT_DIMS, preferred_element_type=F32) * scale
        l1, l2, l3 = _split3(lf_refs[g][...])
        zz = jnp.dot(jnp.concatenate([l1, l2, l3], axis=0), u_ref[...], preferred_element_type=F32)
        zs = zz[0:nh] + zz[nh:2 * nh] + zz[2 * nh:3 * nh]
        s = s + zs[:, :n] + jnp.tile(carry, (1, n // 128))
        carry = carry + zs[:, n:]
        scores.append(jnp.where(diag, s, -jnp.inf))
    carry_sc[...] = carry

    m_prev = m_sc[...]
    m_new = m_prev
    for s in scores:
        m_new = jnp.maximum(m_new, jnp.max(s, axis=1, keepdims=True))
    alpha = jnp.exp(m_prev - m_new)
    l_new = alpha * l_sc[...]
    acc = alpha * acc_sc[...]
    for g in range(G):
        pr = jnp.exp(scores[g] - m_new)
        v2 = v_refs[g][...].reshape(n, v_refs[g].shape[2]).astype(BF16)
        l_new = l_new + jnp.sum(pr, axis=1, keepdims=True)
        acc = acc + jnp.dot(pr.astype(BF16), v2, preferred_element_type=F32)
    l_sc[...] = l_new
    acc_sc[...] = acc
    m_sc[...] = m_new

    @pl.when(p == pl.num_programs(1) - 1)
    def _():
        zz = z_ref[...]
        o_ref[...] = zz * _sigmoid(zz) * (acc / l_new)


def _fox_decode(page_table, q3, kn3, vn3, z3, lfn3, k_pool, v_pool, lf_pool_t, *, dh, G):
    N, nh, _ = q3.shape
    n_pages = page_table.shape[1]
    P = k_pool.shape[1]
    G = math.gcd(G, n_pages)
    assert nh & (nh - 1) == 0
    u = _decay_selector(P, nh)
    per = lambda: pl.BlockSpec((None, nh, dh), lambda b, p, pt: (b, 0, 0))
    page = lambda g: (lambda b, p, pt, g=g: (pt[b, n_pages - 1 - (p * G + g)], 0, 0, 0))
    page3 = lambda g: (lambda b, p, pt, g=g: (pt[b, n_pages - 1 - (p * G + g)], 0, 0))
    grid_spec = pltpu.PrefetchScalarGridSpec(
        num_scalar_prefetch=1,
        grid=(N, n_pages // G),
        in_specs=[per(), per(), per(), per(),
                  pl.BlockSpec((None, nh, 128), lambda b, p, pt: (b, 0, 0)),
                  pl.BlockSpec(u.shape, lambda b, p, pt: (0, 0))]
                 + [pl.BlockSpec((None, P, nh, dh), page(g)) for g in range(G)]
                 + [pl.BlockSpec((None, P, nh, dh), page(g)) for g in range(G)]
                 + [pl.BlockSpec((None, nh, P), page3(g)) for g in range(G)],
        out_specs=pl.BlockSpec((None, nh, dh), lambda b, p, pt: (b, 0, 0)),
        scratch_shapes=[pltpu.VMEM((nh, 1), F32), pltpu.VMEM((nh, 1), F32), pltpu.VMEM((nh, dh), F32),
                        pltpu.VMEM((nh, 128), F32)],
    )
    return pl.pallas_call(
        functools.partial(_fox_decode_kernel, scale=dh ** -0.5, nh=nh, G=G),
        grid_spec=grid_spec,
        out_shape=jax.ShapeDtypeStruct((N, nh, dh), F32),
        compiler_params=_params("parallel", "arbitrary"),
        name="fox_decode",
    )(page_table, q3, kn3, vn3, z3, lfn3, u, *([k_pool] * G), *([v_pool] * G), *([lf_pool_t] * G))


def _mem_attn_kernel(q_ref, z_ref, k_ref, v_ref, qg_ref, a_ref, *, dh, scale):
    nh = q_ref.shape[1] // dh
    for h in range(nh):
        sl = slice(h * dh, (h + 1) * dh)
        q = _rms_rows(q_ref[:, sl], qg_ref[...]).astype(BF16)
        s = lax.dot_general(q, k_ref[:, sl].astype(BF16), NT_DIMS, preferred_element_type=F32) * scale
        e = jnp.exp(s - jnp.max(s, axis=1, keepdims=True))
        o = jnp.dot(e.astype(BF16), v_ref[:, sl].astype(BF16), preferred_element_type=F32)
        o = o / jnp.sum(e, axis=1, keepdims=True)
        zz = z_ref[:, sl]
        a_ref[:, sl] = (zz * _sigmoid(zz) * o).astype(a_ref.dtype)


def _mem_attn_prompt(h3, mk, mv, q_gain, cols, *, width, dh, tq):
    B, T, _ = h3.shape
    n_mem = mk.shape[1]
    tq = min(tq, T)
    spec = lambda off: pl.BlockSpec((None, tq, width), lambda b, i, off=off: (b, i, off // width))
    mem = pl.BlockSpec((None, n_mem, width), lambda b, i: (b, 0, 0))
    return pl.pallas_call(
        functools.partial(_mem_attn_kernel, dh=dh, scale=dh ** -0.5),
        grid=(B, T // tq),
        in_specs=[spec(cols["cq"]), spec(cols["cz"]), mem, mem, pl.BlockSpec((1, dh), lambda b, i: (0, 0))],
        out_specs=pl.BlockSpec((None, tq, width), lambda b, i: (b, i, 0)),
        out_shape=jax.ShapeDtypeStruct((B, T, width), BF16),
        compiler_params=_params("parallel", "parallel"),
        name="mem_attn",
    )(h3, h3, mk, mv, q_gain.reshape(1, dh))


def _mem_decode_kernel(q_ref, z_ref, k_ref, v_ref, qg_ref, o_ref, *, scale, nh):
    for s_i in range(q_ref.shape[0]):
        q = _rms_rows(q_ref[s_i], qg_ref[...]).astype(BF16)
        kp = k_ref[s_i]
        vp = v_ref[s_i]
        rows = kp.shape[0] * kp.shape[1]
        k2 = kp.reshape(rows, kp.shape[2]).astype(BF16)
        v2 = vp.reshape(rows, vp.shape[2]).astype(BF16)
        s = lax.dot_general(q, k2, NT_DIMS, preferred_element_type=F32) * scale
        sub = lax.broadcasted_iota(jnp.int32, s.shape, 0)
        lane = lax.broadcasted_iota(jnp.int32, s.shape, 1)
        s = jnp.where((lane & (nh - 1)) == sub, s, -jnp.inf)
        m = jnp.maximum(jnp.max(s, axis=1, keepdims=True), -1e30)
        e = jnp.exp(s - m)
        den = jnp.maximum(jnp.sum(e, axis=1, keepdims=True), 1e-30)
        o = jnp.dot(e.astype(BF16), v2, preferred_element_type=F32) / den
        zz = z_ref[s_i]
        o_ref[s_i] = zz * _sigmoid(zz) * o


def _mem_decode(q3, z3, k_cache, v_cache, q_gain, *, dh, samples_per_step):
    N, n_mem, nh, _ = k_cache.shape
    S = math.gcd(N, samples_per_step)
    assert nh & (nh - 1) == 0
    per = pl.BlockSpec((S, 8, dh), lambda b: (b, 0, 0))
    cache = pl.BlockSpec((S, n_mem, nh, dh), lambda b: (b, 0, 0, 0))
    return pl.pallas_call(
        functools.partial(_mem_decode_kernel, scale=dh ** -0.5, nh=nh),
        grid=(N // S,),
        in_specs=[per, per, cache, cache, pl.BlockSpec((1, dh), lambda b: (0, 0))],
        out_specs=per,
        out_shape=jax.ShapeDtypeStruct((N, 8, dh), F32),
        compiler_params=_params("parallel"),
        name="mem_decode",
    )(q3, z3, k_cache, v_cache, q_gain.reshape(1, dh))


def _merge_kernel(am_ref, af_ref, ac_ref, wm_ref, wf_ref, wc_ref, gm_ref, gf_ref, gc_ref, u_ref):
    d = lambda a, w: jnp.dot(a[...], w[...], preferred_element_type=F32)
    u = _sigmoid(gm_ref[...]) * d(am_ref, wm_ref)
    u = u + _sigmoid(gf_ref[...]) * d(af_ref, wf_ref)
    u = u + _sigmoid(gc_ref[...]) * d(ac_ref, wc_ref)
    u_ref[...] = u.astype(u_ref.dtype)


def _merge(am, af, ac, wm, wf, wc, h2, g_off, *, tm, tn):
    m, k = am.shape
    n = wm.shape[1]
    tm = min(tm, m)
    a_spec = pl.BlockSpec((tm, k), lambda i, j: (i, 0))
    w_spec = pl.BlockSpec((k, tn), lambda i, j: (0, j))
    g_spec = lambda b: pl.BlockSpec((tm, tn), lambda i, j, b=b: (i, (g_off + b * n) // tn + j))
    return pl.pallas_call(
        _merge_kernel,
        grid=(m // tm, n // tn),
        in_specs=[a_spec, a_spec, a_spec, w_spec, w_spec, w_spec, g_spec(0), g_spec(1), g_spec(2)],
        out_specs=pl.BlockSpec((tm, tn), lambda i, j: (i, j)),
        out_shape=jax.ShapeDtypeStruct((m, n), BF16),
        compiler_params=_params("parallel", "arbitrary"),
        name="merge",
    )(am, af, ac, wm, wf, wc, h2, h2, h2)


def _out_proj_kernel(u_ref, w_ref, x_ref, y_ref):
    y_ref[...] = x_ref[...] + jnp.dot(u_ref[...], w_ref[...].astype(BF16), preferred_element_type=F32)


def _out_proj(u, w, x, *, tm, tn):
    m, k = u.shape
    n = w.shape[1]
    tm = min(tm, m)
    return pl.pallas_call(
        _out_proj_kernel,
        grid=(m // tm, n // tn),
        in_specs=[
            pl.BlockSpec((tm, k), lambda i, j: (i, 0)),
            pl.BlockSpec((k, tn), lambda i, j: (0, j)),
            pl.BlockSpec((tm, tn), lambda i, j: (i, j)),
        ],
        out_specs=pl.BlockSpec((tm, tn), lambda i, j: (i, j)),
        out_shape=jax.ShapeDtypeStruct((m, n), F32),
        compiler_params=_params("parallel", "arbitrary"),
        name="out_proj",
    )(u, w, x)


def _column_layout(d_model):
    ml_dv = d_model // ML_HEADS
    ml_qkw = ML_HEADS * (ml_dv // 2)
    names = ("mq", "mk", "mv", "mi", "mf", "mo", "mz", "fq", "fk", "fv", "ff", "fz", "cq", "cz", "g")
    widths = (ml_qkw, ml_qkw, d_model, ML_HEADS, ML_HEADS, d_model, d_model,
              d_model, d_model, d_model, FOX_HEADS, d_model, d_model, d_model, N_BRANCH * d_model)
    src, off = {}, 0
    for n, w in zip(names, widths):
        src[n] = (off, w)
        off += w
    big, boff = {}, 0
    for n in names:
        if n in ("mi", "mf", "ff"):
            continue
        big[n] = boff
        boff += src[n][1]
    return src, big, boff


def _layer(x_p, x_s, mem_p, fox_k_pool, fox_v_pool, fox_lf_pool, page_table, mem_k_s, mem_v_s,
           c0_s, n0_s, m0_s, norm_in, norm_mem, w_in, w_mem_kv, b_i, b_f, b_ff, ml_out_norm,
           fq_norm, fk_norm, cq_norm, ck_norm, w_br_m, w_br_f, w_br_c, w_out):
    B, T, D = x_p.shape
    N = x_s.shape[0]
    dv = D // ML_HEADS
    dqk = dv // 2
    fox_dh = D // FOX_HEADS
    mem_dh = D // MEM_HEADS
    src, cols, nbig = _column_layout(D)
    w_in_t = w_in.T
    gap1, gap2 = cols["mo"], cols["fz"]
    skip1, skip2 = 2 * ML_HEADS, 2 * ML_HEADS + FOX_HEADS

    def big_rows(tn):
        assert gap1 % tn == 0 and gap2 % tn == 0
        return lambda j: j * tn + jnp.where(j * tn >= gap2, skip2, jnp.where(j * tn >= gap1, skip1, 0))

    rows = lambda n: w_in_t[src[n][0]:src[n][0] + src[n][1]]
    n_gate = 2 * ML_HEADS + FOX_HEADS
    w_gate_t = jnp.concatenate([rows("mi"), rows("mf"), rows("ff"),
                                jnp.zeros((GATE_LANES - n_gate, D), F32)], axis=0)
    b_gate = jnp.concatenate([b_i, b_f, b_ff, jnp.zeros((GATE_LANES - n_gate,), F32)]).reshape(1, GATE_LANES)
    wm, wf, wc = (w.astype(BF16) for w in (w_br_m, w_br_f, w_br_c))

    xp2 = x_p.reshape(B * T, D)
    xs2 = x_s.reshape(N, D)

    h2, hs2 = _norm_proj(xp2, norm_in, w_in_t, tm=2048, tn=512, n_out=nbig, w_index=big_rows(512),
                         transposed=True, x_rider=xs2)
    g2 = _gates(xp2, norm_in, w_gate_t, b_gate, tm=1024)
    h3 = h2.reshape(B, T, nbig)
    g3 = g2.reshape(B, T, GATE_LANES)

    n_mem = mem_p.shape[1]
    mem2 = mem_p.reshape(B * n_mem, D)
    heads_k = D // mem_dh
    mk_p, _ = _norm_proj(mem2, norm_mem, w_mem_kv, tm=512, tn=mem_dh, n_out=D, w_index=lambda j: j,
                         transposed=False, head_gain=ck_norm)
    mv_p, _ = _norm_proj(mem2, norm_mem, w_mem_kv, tm=512, tn=mem_dh, n_out=D, w_index=lambda j: heads_k + j,
                         transposed=False)

    a_m, c_p, n_p, m_p = _mlstm_prompt(h3, g3, ml_out_norm, cols, dqk=dqk, dv=dv,
                                       chunks_per_step=4, heads_per_step=4)

    qa, ka, vb, kn, fv_p = _fox_prep(h3, g3, fq_norm, fk_norm, cols, width=D, dh=fox_dh, tp=256)
    a_f = _fox_flash(qa, ka, vb, h3, cols, dh=fox_dh, t=1024, sub=256, heads_per_step=2)

    a_c = _mem_attn_prompt(h3, mk_p.reshape(B, n_mem, D), mv_p.reshape(B, n_mem, D), cq_norm, cols,
                           width=D, dh=mem_dh, tq=512)

    u_p = _merge(a_m.reshape(B * T, D), a_f.reshape(B * T, D), a_c.reshape(B * T, D),
                 wm, wf, wc, h2, cols["g"], tm=1024, tn=512)
    y_p = _out_proj(u_p, w_out, xp2, tm=1024, tn=1024).reshape(B, T, D)

    gs2 = _gates(xs2, norm_in, w_gate_t, b_gate, tm=128)
    hs3 = hs2.reshape(N, 1, nbig)
    gs3 = gs2.reshape(N, 1, GATE_LANES)

    a_ms, c_s, n_s, m_s = _mlstm_sample(hs3, gs3, m0_s.reshape(N, 1, ML_HEADS), ml_out_norm, c0_s, n0_s,
                                        cols, dqk=dqk, dv=dv, samples_per_step=4)

    hs1 = hs2.reshape(1, N, nbig)
    qn_s, kn_s = _qk_norm(hs1, fq_norm, fk_norm, cols, width=D, dh=fox_dh, tq=128)
    seg = lambda n: hs2[:, cols[n]:cols[n] + D]
    fv_s = seg("fv")
    flf_s = gs2[:, 2 * ML_HEADS:n_gate]
    lfn = jnp.broadcast_to(flf_s[:, :, None], (N, FOX_HEADS, 128))
    to_heads = lambda a: a.reshape(N, FOX_HEADS, fox_dh)
    o_f = _fox_decode(page_table, to_heads(qn_s.reshape(N, D)), to_heads(kn_s.reshape(N, D)), to_heads(fv_s),
                      to_heads(seg("fz")), lfn, fox_k_pool, fox_v_pool, jnp.swapaxes(fox_lf_pool, 1, 2),
                      dh=fox_dh, G=8)
    a_fs = o_f.reshape(N, D).astype(BF16)

    pad8 = lambda a: jnp.pad(a.reshape(N, MEM_HEADS, mem_dh), ((0, 0), (0, 8 - MEM_HEADS), (0, 0)))
    o_c = _mem_decode(pad8(seg("cq")), pad8(seg("cz")), mem_k_s, mem_v_s, cq_norm, dh=mem_dh, samples_per_step=4)
    a_cs = o_c[:, :MEM_HEADS].reshape(N, D).astype(BF16)

    u_s = _merge(a_ms.reshape(N, D), a_fs, a_cs, wm, wf, wc, hs2, cols["g"], tm=128, tn=512)
    y_s = _out_proj(u_s, w_out, xs2, tm=128, tn=1024).reshape(N, 1, D)

    outs_p = (kn.reshape(B, T, FOX_HEADS, fox_dh), fv_p.reshape(B, T, FOX_HEADS, fox_dh),
              g3[:, :, 2 * ML_HEADS:n_gate],
              mk_p.reshape(B, n_mem, MEM_HEADS, mem_dh), mv_p.reshape(B, n_mem, MEM_HEADS, mem_dh),
              c_p, n_p.reshape(B, ML_HEADS, dqk), m_p.reshape(B, ML_HEADS))
    outs_s = (kn_s.reshape(N, 1, FOX_HEADS, fox_dh), fv_s.reshape(N, 1, FOX_HEADS, fox_dh),
              flf_s.reshape(N, 1, FOX_HEADS), c_s, n_s, m_s.reshape(N, ML_HEADS))
    return y_p, y_s, outs_p, outs_s


def kernel(x_prompt, x_sample, mem_prompt, cache_fox_k, cache_fox_v, cache_fox_logf, page_table, cache_mem_k, cache_mem_v, state_mlstm_C, state_mlstm_n, state_mlstm_m, norm_in, norm_mem, w_in, w_mem_kv, b_mlstm_i, b_mlstm_f, b_fox_f, mlstm_out_norm, fox_q_norm, fox_k_norm, mem_q_norm, mem_k_norm, w_br_mlstm, w_br_fox, w_br_mem, w_out):
    depth = w_in.shape[0]
    y_p, y_s = x_prompt, x_sample
    per_layer = []
    for l in range(depth):
        y_p, y_s, outs_p, outs_s = _layer(
            y_p, y_s, mem_prompt, cache_fox_k[l], cache_fox_v[l], cache_fox_logf[l], page_table,
            cache_mem_k[l], cache_mem_v[l], state_mlstm_C[l], state_mlstm_n[l], state_mlstm_m[l],
            norm_in[l], norm_mem[l], w_in[l], w_mem_kv[l], b_mlstm_i[l], b_mlstm_f[l], b_fox_f[l],
            mlstm_out_norm[l], fox_q_norm[l], fox_k_norm[l], mem_q_norm[l], mem_k_norm[l],
            w_br_mlstm[l], w_br_fox[l], w_br_mem[l], w_out[l])
        per_layer.append(outs_p + outs_s)
    stacked = tuple(jnp.stack([lay[i] for lay in per_layer]) for i in range(len(per_layer[0])))
    return (y_p, y_s) + stacked
```

```python
import functools
import math

import jax
import jax.numpy as jnp
from jax import lax
from jax.experimental import pallas as pl
from jax.experimental.pallas import tpu as pltpu

F32 = jnp.float32
BF16 = jnp.bfloat16

ML_HEADS = 4
FOX_HEADS = 16
MEM_HEADS = 4
N_BRANCH = 3
ML_CHUNK = 128
EPS = 1e-6
LOG2E = 1.4426950408889634
LANES = 128
GATE_LANES = LANES
VMEM_LIMIT = 56 * 1024 * 1024

NT_DIMS = (((1,), (1,)), ((), ()))
TN_DIMS = (((0,), (0,)), ((), ()))


def _params(*sem):
    return pltpu.CompilerParams(dimension_semantics=sem, vmem_limit_bytes=VMEM_LIMIT)


def _sigmoid(z):
    return 1.0 / (1.0 + jnp.exp(-z))


def _log_sigmoid(z):
    return jnp.minimum(z, 0.0) - jnp.log1p(jnp.exp(-jnp.abs(z)))


def _split3(x):
    x1 = x.astype(BF16)
    r1 = x - x1.astype(F32)
    x2 = r1.astype(BF16)
    r2 = r1 - x2.astype(F32)
    return x1, x2, r2.astype(BF16)


def _dot_sel_left(sel, x):
    x1, x2, x3 = _split3(x)
    d = lambda p: jnp.dot(sel, p, preferred_element_type=F32)
    return d(x1) + d(x2) + d(x3)


def _rms_rows(x, gain):
    return x * lax.rsqrt(jnp.mean(x * x, axis=-1, keepdims=True) + EPS) * gain


def _norm_proj_kernel(*refs, head_norm, transposed, rider):
    refs = list(refs)
    x_ref = refs.pop(0)
    xr_ref = refs.pop(0) if rider else None
    g_ref, w_ref = refs.pop(0), refs.pop(0)
    hg_ref = refs.pop(0) if head_norm else None
    o_ref = refs.pop(0)
    or_ref = refs.pop(0) if rider else None
    xn_ref, = refs
    tm = x_ref.shape[0]

    @pl.when(pl.program_id(1) == 0)
    def _():
        xn_ref[:tm, :] = _rms_rows(x_ref[...], g_ref[...]).astype(BF16)
        if rider:
            xn_ref[tm:, :] = _rms_rows(xr_ref[...], g_ref[...]).astype(BF16)

    w = w_ref[...].astype(BF16)
    if transposed:
        acc = lax.dot_general(xn_ref[...], w, NT_DIMS, preferred_element_type=F32)
    else:
        acc = jnp.dot(xn_ref[...], w, preferred_element_type=F32)
    if head_norm:
        acc = _rms_rows(acc, hg_ref[...])
    o_ref[...] = acc[:tm].astype(o_ref.dtype)
    if rider:
        or_ref[...] = acc[tm:].astype(or_ref.dtype)


def _norm_proj(x, gain, w, *, tm, tn, n_out, w_index, transposed, head_gain=None, x_rider=None):
    m, k = x.shape
    n = n_out
    tm = min(tm, m)
    assert m % tm == 0 and n % tn == 0
    rider = x_rider is not None
    tr = 0
    if rider:
        tr = x_rider.shape[0] // (m // tm)
        assert tr * (m // tm) == x_rider.shape[0] and tr % 16 == 0
    if transposed:
        w_spec = pl.BlockSpec((pl.Element(tn), pl.Element(k)), lambda i, j: (pl.multiple_of(w_index(j), 8), 0))
    else:
        w_spec = pl.BlockSpec((k, tn), lambda i, j: (0, w_index(j)))
    in_specs = [pl.BlockSpec((tm, k), lambda i, j: (i, 0), pipeline_mode=pl.Buffered(1))]
    args = [x]
    if rider:
        in_specs.append(pl.BlockSpec((tr, k), lambda i, j: (i, 0)))
        args.append(x_rider)
    in_specs += [pl.BlockSpec((1, k), lambda i, j: (0, 0)), w_spec]
    args += [gain.reshape(1, k), w]
    if head_gain is not None:
        assert head_gain.shape == (tn,)
        in_specs.append(pl.BlockSpec((1, tn), lambda i, j: (0, 0)))
        args.append(head_gain.reshape(1, tn))
    out_specs = [pl.BlockSpec((tm, tn), lambda i, j: (i, j))]
    out_shape = [jax.ShapeDtypeStruct((m, n), F32)]
    if rider:
        out_specs.append(pl.BlockSpec((tr, tn), lambda i, j: (i, j)))
        out_shape.append(jax.ShapeDtypeStruct((x_rider.shape[0], n), F32))
    outs = pl.pallas_call(
        functools.partial(_norm_proj_kernel, head_norm=head_gain is not None, transposed=transposed, rider=rider),
        grid=(m // tm, n // tn),
        in_specs=in_specs,
        out_specs=out_specs,
        out_shape=out_shape,
        scratch_shapes=[pltpu.VMEM((tm + tr, k), BF16)],
        compiler_params=_params("parallel", "arbitrary"),
        name="norm_proj",
    )(*args)
    return (outs[0], outs[1]) if rider else (outs[0], None)


def _gates_kernel(x_ref, g_ref, w_ref, b_ref, o_ref):
    xn = _rms_rows(x_ref[...], g_ref[...])
    x1, x2, _ = _split3(xn)
    w1, w2, _ = _split3(w_ref[...])
    d = lambda a, b: lax.dot_general(a, b, NT_DIMS, preferred_element_type=F32)
    z = d(x1, w1) + d(x1, w2) + d(x2, w1) + b_ref[...]
    lane = lax.broadcasted_iota(jnp.int32, z.shape, 1)
    o_ref[...] = jnp.where(lane < ML_HEADS, z, _log_sigmoid(z))


def _gates(x, gain, w_gate, bias, *, tm):
    m, k = x.shape
    tm = min(tm, m)
    return pl.pallas_call(
        _gates_kernel,
        grid=(m // tm,),
        in_specs=[
            pl.BlockSpec((tm, k), lambda i: (i, 0)),
            pl.BlockSpec((1, k), lambda i: (0, 0)),
            pl.BlockSpec((GATE_LANES, k), lambda i: (0, 0)),
            pl.BlockSpec((1, GATE_LANES), lambda i: (0, 0)),
        ],
        out_specs=pl.BlockSpec((tm, GATE_LANES), lambda i: (i, 0)),
        out_shape=jax.ShapeDtypeStruct((m, GATE_LANES), F32),
        compiler_params=_params("parallel"),
        name="gates",
    )(x, gain.reshape(1, k), w_gate, bias)


def _mlstm_chunk_kernel(q_ref, k_ref, v_ref, o_ref, z_ref, g_ref, gain_ref,
                        a_ref, c_out, n_out, m_out, c_sc, n_sc, m_sc, *, dqk, dv, L, HP):
    hp = pl.program_id(1)
    step = pl.program_id(2)
    n_chunks = q_ref.shape[0] // L

    @pl.when(step == 0)
    def _():
        c_sc[...] = jnp.zeros_like(c_sc)
        n_sc[...] = jnp.zeros_like(n_sc)
        m_sc[...] = jnp.zeros_like(m_sc)

    row = lax.broadcasted_iota(jnp.int32, (L, L), 0)
    col = lax.broadcasted_iota(jnp.int32, (L, L), 1)
    causal = col <= row
    tri = jnp.where(causal, 1.0, 0.0).astype(BF16)
    lane = lax.broadcasted_iota(jnp.int32, (L, GATE_LANES), 1)
    sub = lax.broadcasted_iota(jnp.int32, (GATE_LANES, L), 0)
    pick_col = lambda x, idx: jnp.sum(jnp.where(lane == idx, x, 0.0), axis=1, keepdims=True)
    pick_row = lambda x, idx: jnp.sum(jnp.where(sub == idx, x, 0.0), axis=0, keepdims=True)

    def chunk(ci, carry):
        rows = pl.ds(pl.multiple_of(ci * L, L), L)
        g = g_ref[rows, :]
        cum = _dot_sel_left(tri, g)
        g_t = g.T
        cum_t = cum.T
        for hh in range(HP):
            h = hp * HP + hh
            qs = slice(hh * dqk, (hh + 1) * dqk)
            vs = slice(hh * dv, (hh + 1) * dv)
            ig_col = pick_col(g, h)
            b_col = pick_col(cum, ML_HEADS + h)
            ig_row = pick_row(g_t, h)
            b_row = pick_row(cum_t, ML_HEADS + h)

            m_prev = m_sc[hh]
            a_col = b_col + m_prev
            dmat = jnp.where(causal, b_col - b_row + ig_row, -jnp.inf)
            m_t = jnp.maximum(a_col, jnp.max(dmat, axis=1, keepdims=True))
            w_inter = jnp.exp(a_col - m_t)
            w_intra = jnp.exp(dmat - m_t)

            qf = q_ref[rows, qs] * (dqk ** -0.5)
            kf = k_ref[rows, qs]
            vf = v_ref[rows, vs]
            q = qf.astype(BF16)
            k = kf.astype(BF16)
            v = vf.astype(BF16)
            c_prev = c_sc[hh]
            n_prev = n_sc[hh]

            s = lax.dot_general(q, k, NT_DIMS, preferred_element_type=F32) * w_intra
            inter = lax.dot_general(q, c_prev.astype(BF16), NT_DIMS, preferred_element_type=F32)
            num = w_inter * inter + jnp.dot(s.astype(BF16), v, preferred_element_type=F32)
            nq = w_inter * jnp.sum(qf * n_prev, axis=1, keepdims=True) + jnp.sum(s, axis=1, keepdims=True)
            hid = num / jnp.maximum(jnp.abs(nq), jnp.exp(-m_t))

            b_last = b_col[L - 1:L, :]
            m_new = m_t[L - 1:L, :]
            w_c = jnp.exp(b_last + m_prev - m_new)
            w_s = jnp.exp(b_last - b_col + ig_col - m_new)
            wv = (w_s * vf).astype(BF16)
            c_sc[hh] = w_c * c_prev + lax.dot_general(wv, k, TN_DIMS, preferred_element_type=F32)
            n_sc[hh] = w_c * n_prev + jnp.sum(w_s * kf, axis=0, keepdims=True)
            m_sc[hh] = m_new

            hn = _rms_rows(hid, gain_ref[:, vs])
            zz = z_ref[rows, vs]
            a_ref[rows, vs] = (zz * _sigmoid(zz) * (_sigmoid(o_ref[rows, vs]) * hn)).astype(a_ref.dtype)
        return carry

    lax.fori_loop(0, n_chunks, chunk, 0, unroll=2)

    @pl.when(step == pl.num_programs(2) - 1)
    def _():
        c_out[...] = c_sc[...]
        n_out[...] = n_sc[...]
        m_out[...] = m_sc[...]


def _mlstm_prompt(h3, g3, out_gain, cols, *, dqk, dv, chunks_per_step, heads_per_step):
    B, T, _ = h3.shape
    L = math.gcd(T, ML_CHUNK)
    rows = L * math.gcd(T // L, chunks_per_step)
    HP = math.gcd(ML_HEADS, heads_per_step)
    qo, ko, vo, oo, zo = (cols[n] for n in ("mq", "mk", "mv", "mo", "mz"))
    blk = lambda w, off: pl.BlockSpec((None, rows, HP * w),
                                      lambda b, h, c, off=off, w=w: (b, c, off // (HP * w) + h))
    kern = functools.partial(_mlstm_chunk_kernel, dqk=dqk, dv=dv, L=L, HP=HP)
    return pl.pallas_call(
        kern,
        grid=(B, ML_HEADS // HP, T // rows),
        in_specs=[
            blk(dqk, qo), blk(dqk, ko), blk(dv, vo), blk(dv, oo), blk(dv, zo),
            pl.BlockSpec((None, rows, GATE_LANES), lambda b, h, c: (b, c, 0)),
            pl.BlockSpec((1, HP * dv), lambda b, h, c: (0, h)),
        ],
        out_specs=[
            pl.BlockSpec((None, rows, HP * dv), lambda b, h, c: (b, c, h)),
            pl.BlockSpec((None, HP, dv, dqk), lambda b, h, c: (b, h, 0, 0)),
            pl.BlockSpec((None, HP, 1, dqk), lambda b, h, c: (b, h, 0, 0)),
            pl.BlockSpec((None, HP, 1, 1), lambda b, h, c: (b, h, 0, 0)),
        ],
        out_shape=[
            jax.ShapeDtypeStruct((B, T, ML_HEADS * dv), BF16),
            jax.ShapeDtypeStruct((B, ML_HEADS, dv, dqk), F32),
            jax.ShapeDtypeStruct((B, ML_HEADS, 1, dqk), F32),
            jax.ShapeDtypeStruct((B, ML_HEADS, 1, 1), F32),
        ],
        scratch_shapes=[pltpu.VMEM((HP, dv, dqk), F32), pltpu.VMEM((HP, 1, dqk), F32),
                        pltpu.VMEM((HP, 1, 1), F32)],
        compiler_params=_params("parallel", "parallel", "arbitrary"),
        name="mlstm_chunk",
    )(h3, h3, h3, h3, h3, g3, out_gain.reshape(1, ML_HEADS * dv))


def _mlstm_step_kernel(q_ref, k_ref, v_ref, o_ref, z_ref, g_ref, m0_ref, gain_ref, c0_ref, n0_ref,
                       a_ref, c_out, n_out, m_out, *, dqk, dv):
    lane_h = lax.broadcasted_iota(jnp.int32, (1, ML_HEADS), 1)
    first = lax.broadcasted_iota(jnp.int32, (16, 1), 0) == 0
    for s_i in range(q_ref.shape[0]):
        g = g_ref[s_i]
        m0 = m0_ref[s_i]
        m_acc = jnp.zeros((1, ML_HEADS), F32)
        for h in range(ML_HEADS):
            ig = g[:, h:h + 1]
            lf = g[:, ML_HEADS + h:ML_HEADS + h + 1]
            a = lf + m0[:, h:h + 1]
            m_t = jnp.maximum(a, ig)
            w_inter = jnp.exp(a - m_t)
            w_intra = jnp.exp(ig - m_t)
            qf = q_ref[s_i, :, h * dqk:(h + 1) * dqk] * (dqk ** -0.5)
            kf = k_ref[s_i, :, h * dqk:(h + 1) * dqk]
            vf = v_ref[s_i, :, h * dv:(h + 1) * dv]
            c0 = c0_ref[s_i, h]
            n0 = n0_ref[s_i, h:h + 1, :]
            s = jnp.sum(qf * kf, axis=1, keepdims=True) * w_intra
            q8 = jnp.broadcast_to(qf, (8, dqk)).astype(BF16)
            cq = lax.dot_general(q8, c0.astype(BF16), NT_DIMS, preferred_element_type=F32)[0:1, :]
            num = w_inter * cq + s * vf
            nq = w_inter * jnp.sum(n0 * qf, axis=1, keepdims=True) + s
            hid = num / jnp.maximum(jnp.abs(nq), jnp.exp(-m_t))
            v16 = jnp.where(first, vf, 0.0).astype(BF16)
            k16 = jnp.where(first, w_intra * kf, 0.0).astype(BF16)
            c_out[s_i, h] = w_inter * c0 + lax.dot_general(v16, k16, TN_DIMS, preferred_element_type=F32)
            n_out[s_i, h:h + 1, :] = w_inter * n0 + w_intra * kf
            m_acc = m_acc + jnp.where(lane_h == h, m_t, 0.0)
            hn = _rms_rows(hid, gain_ref[:, h * dv:(h + 1) * dv])
            zz = z_ref[s_i, :, h * dv:(h + 1) * dv]
            gate_o = _sigmoid(o_ref[s_i, :, h * dv:(h + 1) * dv])
            a_ref[s_i, :, h * dv:(h + 1) * dv] = (zz * _sigmoid(zz) * (gate_o * hn)).astype(a_ref.dtype)
        m_out[s_i] = m_acc


def _mlstm_sample(hs3, gs3, m0, out_gain, c0, n0, cols, *, dqk, dv, samples_per_step):
    N = hs3.shape[0]
    S = math.gcd(N, samples_per_step)
    W = ML_HEADS * dv
    WQ = ML_HEADS * dqk
    qo, ko, vo, oo, zo = (cols[n] for n in ("mq", "mk", "mv", "mo", "mz"))
    row = lambda w, off: pl.BlockSpec((S, 1, w), lambda b, off=off, w=w: (b, 0, off // w))
    kern = functools.partial(_mlstm_step_kernel, dqk=dqk, dv=dv)
    return pl.pallas_call(
        kern,
        grid=(N // S,),
        in_specs=[
            row(WQ, qo), row(WQ, ko), row(W, vo), row(W, oo), row(W, zo),
            pl.BlockSpec((S, 1, GATE_LANES), lambda b: (b, 0, 0)),
            pl.BlockSpec((S, 1, ML_HEADS), lambda b: (b, 0, 0)),
            pl.BlockSpec((1, W), lambda b: (0, 0)),
            pl.BlockSpec((S, ML_HEADS, dv, dqk), lambda b: (b, 0, 0, 0)),
            pl.BlockSpec((S, ML_HEADS, dqk), lambda b: (b, 0, 0)),
        ],
        out_specs=[
            pl.BlockSpec((S, 1, W), lambda b: (b, 0, 0)),
            pl.BlockSpec((S, ML_HEADS, dv, dqk), lambda b: (b, 0, 0, 0)),
            pl.BlockSpec((S, ML_HEADS, dqk), lambda b: (b, 0, 0)),
            pl.BlockSpec((S, 1, ML_HEADS), lambda b: (b, 0, 0)),
        ],
        out_shape=[
            jax.ShapeDtypeStruct((N, 1, W), BF16),
            jax.ShapeDtypeStruct((N, ML_HEADS, dv, dqk), F32),
            jax.ShapeDtypeStruct((N, ML_HEADS, dqk), F32),
            jax.ShapeDtypeStruct((N, 1, ML_HEADS), F32),
        ],
        compiler_params=_params("parallel"),
        name="mlstm_step",
    )(hs3, hs3, hs3, hs3, hs3, gs3, m0, out_gain.reshape(1, W), c0, n0)


N_AUG = LANES


def _qk_norm_kernel(q_ref, k_ref, qg_ref, kg_ref, qn_ref, kn_ref, *, dh):
    nh = q_ref.shape[1] // dh
    for h in range(nh):
        sl = slice(h * dh, (h + 1) * dh)
        qn_ref[:, sl] = _rms_rows(q_ref[:, sl], qg_ref[...]).astype(qn_ref.dtype)
        kn_ref[:, sl] = _rms_rows(k_ref[:, sl], kg_ref[...])


def _qk_norm(h3, q_gain, k_gain, cols, *, width, dh, tq):
    B, T, _ = h3.shape
    tq = min(tq, T)
    spec = lambda off: pl.BlockSpec((None, tq, width), lambda b, i, off=off: (b, i, off // width))
    out = pl.BlockSpec((None, tq, width), lambda b, i: (b, i, 0))
    gain = pl.BlockSpec((1, dh), lambda b, i: (0, 0))
    return pl.pallas_call(
        functools.partial(_qk_norm_kernel, dh=dh),
        grid=(B, T // tq),
        in_specs=[spec(cols["fq"]), spec(cols["fk"]), gain, gain],
        out_specs=[out, out],
        out_shape=[jax.ShapeDtypeStruct((B, T, width), BF16), jax.ShapeDtypeStruct((B, T, width), F32)],
        compiler_params=_params("parallel", "parallel"),
        name="qk_norm",
    )(h3, h3, q_gain.reshape(1, dh), k_gain.reshape(1, dh))


def _aug_placement(nh, lane0):
    r = jnp.arange(3 * GATE_LANES)[None, :, None]
    c = jnp.arange(N_AUG)[None, None, :]
    src_lane = (lane0 + jnp.arange(nh))[:, None, None]
    piece = r // GATE_LANES
    hit = (r % GATE_LANES) == src_lane
    pq = jnp.where(hit & (c == piece + 3), 1.0, 0.0)
    pk = jnp.where(hit & (c == piece), -1.0, 0.0)
    return jnp.stack([pq, pk], axis=1).astype(BF16)


def _fox_prep_kernel(q_ref, k_ref, v_ref, g_ref, qg_ref, kg_ref, place_ref,
                     qa_ref, ka_ref, vb_ref, kn_ref, vf_ref, carry_sc, *, dh, qscale):
    tp = q_ref.shape[0]
    nh = q_ref.shape[1] // dh

    @pl.when(pl.program_id(1) == 0)
    def _():
        carry_sc[...] = jnp.zeros_like(carry_sc)

    row = lax.broadcasted_iota(jnp.int32, (tp, tp), 0)
    col = lax.broadcasted_iota(jnp.int32, (tp, tp), 1)
    tri = jnp.where(col <= row, 1.0, 0.0).astype(BF16)
    cum = _dot_sel_left(tri, g_ref[...]) + carry_sc[...]
    carry_sc[...] = cum[tp - 1:tp, :]
    f123 = jnp.concatenate(_split3(cum * LOG2E), axis=1)

    v = v_ref[...]
    vf_ref[...] = v
    vb_ref[...] = v.astype(BF16)
    lane = lax.broadcasted_iota(jnp.int32, (tp, N_AUG), 1)
    q_ones = lane < 3
    k_ones = jnp.abs(lane - 4) <= 1
    for h in range(nh):
        sl = slice(h * dh, (h + 1) * dh)
        kn = _rms_rows(k_ref[:, sl], kg_ref[...])
        kn_ref[:, sl] = kn
        qn = _rms_rows(q_ref[:, sl], qg_ref[...]) * qscale
        eq = jnp.where(q_ones, 1.0, jnp.dot(f123, place_ref[h, 0], preferred_element_type=F32))
        ek = jnp.where(k_ones, 1.0, jnp.dot(f123, place_ref[h, 1], preferred_element_type=F32))
        base = h * (dh + N_AUG)
        qa_ref[:, base:base + dh] = qn.astype(BF16)
        qa_ref[:, base + dh:base + dh + N_AUG] = eq.astype(BF16)
        ka_ref[:, base:base + dh] = kn.astype(BF16)
        ka_ref[:, base + dh:base + dh + N_AUG] = ek.astype(BF16)


def _fox_prep(h3, g3, q_gain, k_gain, cols, *, width, dh, tp):
    B, T, _ = h3.shape
    tp = min(tp, T)
    nh = width // dh
    wa = nh * (dh + N_AUG)
    place = _aug_placement(nh, 2 * ML_HEADS)
    spec = lambda off: pl.BlockSpec((None, tp, width), lambda b, i, off=off: (b, i, off // width))
    out = lambda w: pl.BlockSpec((None, tp, w), lambda b, i: (b, i, 0))
    gain = pl.BlockSpec((1, dh), lambda b, i: (0, 0))
    return pl.pallas_call(
        functools.partial(_fox_prep_kernel, dh=dh, qscale=dh ** -0.5 * LOG2E),
        grid=(B, T // tp),
        in_specs=[spec(cols["fq"]), spec(cols["fk"]), spec(cols["fv"]),
                  pl.BlockSpec((None, tp, GATE_LANES), lambda b, i: (b, i, 0)), gain, gain,
                  pl.BlockSpec(place.shape, lambda b, i: (0, 0, 0, 0))],
        out_specs=[out(wa), out(wa), out(width), out(width), out(width)],
        out_shape=[jax.ShapeDtypeStruct((B, T, wa), BF16), jax.ShapeDtypeStruct((B, T, wa), BF16),
                   jax.ShapeDtypeStruct((B, T, width), BF16),
                   jax.ShapeDtypeStruct((B, T, width), F32), jax.ShapeDtypeStruct((B, T, width), F32)],
        scratch_shapes=[pltpu.VMEM((1, GATE_LANES), F32)],
        compiler_params=_params("parallel", "arbitrary"),
        name="fox_prep",
    )(h3, h3, h3, g3, q_gain.reshape(1, dh), k_gain.reshape(1, dh), place)


def _lane_tiles_reduce(x, op):
    acc = x[:, :LANES]
    for j in range(1, x.shape[1] // LANES):
        acc = op(acc, x[:, j * LANES:(j + 1) * LANES])
    return acc


def _fox_flash_kernel(q_ref, k_ref, v_ref, z_ref, o_ref, m_sc, l_sc, acc_sc, mx_sc, s_sc, p_sc,
                      *, t, sub, dh, HP):
    nq = q_ref.shape[0] // t
    ns = t // sub
    da = dh + N_AUG
    row_l = lax.broadcasted_iota(jnp.int32, (sub, sub), 0)
    col_l = lax.broadcasted_iota(jnp.int32, (sub, sub), 1)

    def block(diag, q0, k0):
        n_chunks = lambda i: i + 1 if diag else ns
        for hh in range(HP):
            ha = slice(hh * da, (hh + 1) * da)
            for i in range(ns):
                rows = slice(i * sub, (i + 1) * sub)
                q_i = q_ref[pl.ds(q0 + i * sub, sub), ha]
                mx = None
                for c in range(n_chunks(i)):
                    cs = slice(c * sub, (c + 1) * sub)
                    s = lax.dot_general(q_i, k_ref[pl.ds(k0 + c * sub, sub), ha], NT_DIMS,
                                        preferred_element_type=F32)
                    if diag and c == i:
                        s = jnp.where(col_l <= row_l, s, -jnp.inf)
                    s_sc[hh, rows, cs] = s
                    cm = _lane_tiles_reduce(s, jnp.maximum)
                    mx = cm if mx is None else jnp.maximum(mx, cm)
                mx_sc[hh, rows, :] = mx
        for hh in range(HP):
            hd = slice(hh * dh, (hh + 1) * dh)
            for i in range(ns):
                rows = slice(i * sub, (i + 1) * sub)
                m_prev = m_sc[hh, rows, :]
                m_new = jnp.maximum(m_prev, jnp.max(mx_sc[hh, rows, :], axis=1, keepdims=True))
                alpha = jnp.exp2(m_prev - m_new)
                m_wide = jnp.tile(m_new, (1, sub // LANES))
                ls = None
                for c in range(n_chunks(i)):
                    cs = slice(c * sub, (c + 1) * sub)
                    p = jnp.exp2(s_sc[hh, rows, cs] - m_wide)
                    p_sc[hh, rows, cs] = p.astype(BF16)
                    lp = _lane_tiles_reduce(p, jnp.add)
                    ls = lp if ls is None else ls + lp
                w = n_chunks(i) * sub
                l_sc[hh, rows, :] = alpha * l_sc[hh, rows, :] + jnp.sum(ls, axis=1, keepdims=True)
                acc_sc[hh, rows, :] = alpha * acc_sc[hh, rows, :] + jnp.dot(
                    p_sc[hh, rows, :w], v_ref[pl.ds(k0, w), hd], preferred_element_type=F32)
                m_sc[hh, rows, :] = m_new

    def q_tile(qi, carry):
        q0 = pl.multiple_of(qi * t, t)
        m_sc[...] = jnp.full_like(m_sc, -jnp.inf)
        l_sc[...] = jnp.zeros_like(l_sc)
        acc_sc[...] = jnp.zeros_like(acc_sc)

        def k_tile(ki, c):
            block(False, q0, pl.multiple_of(ki * t, t))
            return c

        lax.fori_loop(0, qi, k_tile, 0)

        def diag_tile(ki, c):
            block(True, q0, q0)
            return c

        lax.fori_loop(qi, qi + 1, diag_tile, 0)
        for hh in range(HP):
            hd = slice(hh * dh, (hh + 1) * dh)
            zz = z_ref[pl.ds(q0, t), hd]
            o_ref[pl.ds(q0, t), hd] = (zz * _sigmoid(zz) * (acc_sc[hh] / l_sc[hh])).astype(o_ref.dtype)
        return carry

    lax.fori_loop(0, nq, q_tile, 0)


def _fox_flash(qa, ka, vb, h3, cols, *, dh, t, sub, heads_per_step):
    B, T, W = vb.shape
    assert dh == LANES
    t = min(t, T)
    sub = min(sub, t)
    nh = W // dh
    HP = math.gcd(nh, heads_per_step)
    da = dh + N_AUG
    zo = cols["fz"] // (HP * dh)
    head = lambda w, off=0: pl.BlockSpec((None, T, HP * w), lambda b, h, off=off: (b, 0, off + h))
    return pl.pallas_call(
        functools.partial(_fox_flash_kernel, t=t, sub=sub, dh=dh, HP=HP),
        grid=(B, nh // HP),
        in_specs=[head(da), head(da), head(dh), head(dh, zo)],
        out_specs=head(dh),
        out_shape=jax.ShapeDtypeStruct((B, T, W), BF16),
        scratch_shapes=[pltpu.VMEM((HP, t, LANES), F32), pltpu.VMEM((HP, t, LANES), F32),
                        pltpu.VMEM((HP, t, dh), F32), pltpu.VMEM((HP, t, LANES), F32),
                        pltpu.VMEM((HP, t, t), F32), pltpu.VMEM((HP, t, t), BF16)],
        compiler_params=_params("parallel", "parallel"),
        name="fox_flash",
    )(qa, ka, vb, h3)


def _decay_selector(P, nh):
    t_src = jnp.arange(P)[:, None]
    t_dst = jnp.arange(P * nh)[None, :] // nh
    later = (t_src > t_dst).astype(BF16)
    return jnp.concatenate([later, jnp.ones((P, LANES), BF16)], axis=1)


def _fox_decode_kernel(pt_ref, q_ref, kn_ref, vn_ref, z_ref, lfn_ref, u_ref, *rest, scale, nh, G):
    k_refs, v_refs, lf_refs = rest[:G], rest[G:2 * G], rest[2 * G:3 * G]
    o_ref, m_sc, l_sc, acc_sc, carry_sc = rest[3 * G:]
    p = pl.program_id(1)
    q = q_ref[...]

    @pl.when(p == 0)
    def _():
        s_new = jnp.sum(q.astype(F32) * kn_ref[...], axis=1, keepdims=True) * scale
        m_sc[...] = s_new
        l_sc[...] = jnp.ones_like(l_sc)
        acc_sc[...] = vn_ref[...]
        carry_sc[...] = lfn_ref[...]

    P = k_refs[0].shape[0]
    n = P * nh
    sub = lax.broadcasted_iota(jnp.int32, (nh, n), 0)
    lane = lax.broadcasted_iota(jnp.int32, (nh, n), 1)
    diag = (lane & (nh - 1)) == sub
    carry = carry_sc[...]
    scores = []
    for g in range(G):
        k2 = k_refs[g][...].reshape(n, k_refs[g].shape[2]).astype(BF16)
        s = lax.dot_general(q, k2, NT_DIMS, preferred_element_type=F32) * scale
        l1, l2, l3 = _split3(lf_refs[g][...])
        zz = jnp.dot(jnp.concatenate([l1, l2, l3], axis=0), u_ref[...], preferred_element_type=F32)
        zs = zz[0:nh] + zz[nh:2 * nh] + zz[2 * nh:3 * nh]
        s = s + zs[:, :n] + jnp.tile(carry, (1, n // LANES))
        carry = carry + zs[:, n:]
        scores.append(jnp.where(diag, s, -jnp.inf))
    carry_sc[...] = carry

    m_prev = m_sc[...]
    m_new = m_prev
    for s in scores:
        m_new = jnp.maximum(m_new, jnp.max(s, axis=1, keepdims=True))
    alpha = jnp.exp(m_prev - m_new)
    l_new = alpha * l_sc[...]
    acc = alpha * acc_sc[...]
    for g in range(G):
        pr = jnp.exp(scores[g] - m_new)
        v2 = v_refs[g][...].reshape(n, v_refs[g].shape[2]).astype(BF16)
        l_new = l_new + jnp.sum(pr, axis=1, keepdims=True)
        acc = acc + jnp.dot(pr.astype(BF16), v2, preferred_element_type=F32)
    l_sc[...] = l_new
    acc_sc[...] = acc
    m_sc[...] = m_new

    @pl.when(p == pl.num_programs(1) - 1)
    def _():
        zz = z_ref[...]
        o_ref[...] = zz * _sigmoid(zz) * (acc / l_new)


def _fox_decode(page_table, q3, kn3, vn3, z3, lfn3, k_pool, v_pool, lf_pool_t, *, dh, G):
    N, nh, _ = q3.shape
    n_pages = page_table.shape[1]
    P = k_pool.shape[1]
    G = math.gcd(G, n_pages)
    assert nh & (nh - 1) == 0
    u = _decay_selector(P, nh)
    per = lambda: pl.BlockSpec((None, nh, dh), lambda b, p, pt: (b, 0, 0))
    page = lambda g: (lambda b, p, pt, g=g: (pt[b, n_pages - 1 - (p * G + g)], 0, 0, 0))
    page3 = lambda g: (lambda b, p, pt, g=g: (pt[b, n_pages - 1 - (p * G + g)], 0, 0))
    grid_spec = pltpu.PrefetchScalarGridSpec(
        num_scalar_prefetch=1,
        grid=(N, n_pages // G),
        in_specs=[per(), per(), per(), per(),
                  pl.BlockSpec((None, nh, LANES), lambda b, p, pt: (b, 0, 0)),
                  pl.BlockSpec(u.shape, lambda b, p, pt: (0, 0))]
                 + [pl.BlockSpec((None, P, nh, dh), page(g)) for g in range(G)]
                 + [pl.BlockSpec((None, P, nh, dh), page(g)) for g in range(G)]
                 + [pl.BlockSpec((None, nh, P), page3(g)) for g in range(G)],
        out_specs=pl.BlockSpec((None, nh, dh), lambda b, p, pt: (b, 0, 0)),
        scratch_shapes=[pltpu.VMEM((nh, 1), F32), pltpu.VMEM((nh, 1), F32), pltpu.VMEM((nh, dh), F32),
                        pltpu.VMEM((nh, LANES), F32)],
    )
    return pl.pallas_call(
        functools.partial(_fox_decode_kernel, scale=dh ** -0.5, nh=nh, G=G),
        grid_spec=grid_spec,
        out_shape=jax.ShapeDtypeStruct((N, nh, dh), F32),
        compiler_params=_params("parallel", "arbitrary"),
        name="fox_decode",
    )(page_table, q3, kn3, vn3, z3, lfn3, u, *([k_pool] * G), *([v_pool] * G), *([lf_pool_t] * G))


def _mem_attn_kernel(q_ref, z_ref, k_ref, v_ref, qg_ref, a_ref, *, dh, scale):
    nh = q_ref.shape[1] // dh
    for h in range(nh):
        sl = slice(h * dh, (h + 1) * dh)
        q = _rms_rows(q_ref[:, sl], qg_ref[...]).astype(BF16)
        s = lax.dot_general(q, k_ref[:, sl].astype(BF16), NT_DIMS, preferred_element_type=F32) * scale
        e = jnp.exp(s - jnp.max(s, axis=1, keepdims=True))
        o = jnp.dot(e.astype(BF16), v_ref[:, sl].astype(BF16), preferred_element_type=F32)
        o = o / jnp.sum(e, axis=1, keepdims=True)
        zz = z_ref[:, sl]
        a_ref[:, sl] = (zz * _sigmoid(zz) * o).astype(a_ref.dtype)


def _mem_attn_prompt(h3, mk, mv, q_gain, cols, *, width, dh, tq):
    B, T, _ = h3.shape
    n_mem = mk.shape[1]
    tq = min(tq, T)
    spec = lambda off: pl.BlockSpec((None, tq, width), lambda b, i, off=off: (b, i, off // width))
    mem = pl.BlockSpec((None, n_mem, width), lambda b, i: (b, 0, 0))
    return pl.pallas_call(
        functools.partial(_mem_attn_kernel, dh=dh, scale=dh ** -0.5),
        grid=(B, T // tq),
        in_specs=[spec(cols["cq"]), spec(cols["cz"]), mem, mem, pl.BlockSpec((1, dh), lambda b, i: (0, 0))],
        out_specs=pl.BlockSpec((None, tq, width), lambda b, i: (b, i, 0)),
        out_shape=jax.ShapeDtypeStruct((B, T, width), BF16),
        compiler_params=_params("parallel", "parallel"),
        name="mem_attn",
    )(h3, h3, mk, mv, q_gain.reshape(1, dh))


def _mem_decode_kernel(q_ref, z_ref, k_ref, v_ref, qg_ref, o_ref, *, scale, nh):
    for s_i in range(q_ref.shape[0]):
        q = _rms_rows(q_ref[s_i], qg_ref[...]).astype(BF16)
        kp = k_ref[s_i]
        vp = v_ref[s_i]
        rows = kp.shape[0] * kp.shape[1]
        k2 = kp.reshape(rows, kp.shape[2]).astype(BF16)
        v2 = vp.reshape(rows, vp.shape[2]).astype(BF16)
        s = lax.dot_general(q, k2, NT_DIMS, preferred_element_type=F32) * scale
        sub = lax.broadcasted_iota(jnp.int32, s.shape, 0)
        lane = lax.broadcasted_iota(jnp.int32, s.shape, 1)
        s = jnp.where((lane & (nh - 1)) == sub, s, -jnp.inf)
        m = jnp.maximum(jnp.max(s, axis=1, keepdims=True), -1e30)
        e = jnp.exp(s - m)
        den = jnp.maximum(jnp.sum(e, axis=1, keepdims=True), 1e-30)
        o = jnp.dot(e.astype(BF16), v2, preferred_element_type=F32) / den
        zz = z_ref[s_i]
        o_ref[s_i] = zz * _sigmoid(zz) * o


def _mem_decode(q3, z3, k_cache, v_cache, q_gain, *, dh, samples_per_step):
    N, n_mem, nh, _ = k_cache.shape
    S = math.gcd(N, samples_per_step)
    assert nh & (nh - 1) == 0
    per = pl.BlockSpec((S, 8, dh), lambda b: (b, 0, 0))
    cache = pl.BlockSpec((S, n_mem, nh, dh), lambda b: (b, 0, 0, 0))
    return pl.pallas_call(
        functools.partial(_mem_decode_kernel, scale=dh ** -0.5, nh=nh),
        grid=(N // S,),
        in_specs=[per, per, cache, cache, pl.BlockSpec((1, dh), lambda b: (0, 0))],
        out_specs=per,
        out_shape=jax.ShapeDtypeStruct((N, 8, dh), F32),
        compiler_params=_params("parallel"),
        name="mem_decode",
    )(q3, z3, k_cache, v_cache, q_gain.reshape(1, dh))


def _merge_kernel(am_ref, af_ref, ac_ref, wm_ref, wf_ref, wc_ref, gm_ref, gf_ref, gc_ref, u_ref):
    d = lambda a, w: jnp.dot(a[...], w[...], preferred_element_type=F32)
    u = _sigmoid(gm_ref[...]) * d(am_ref, wm_ref)
    u = u + _sigmoid(gf_ref[...]) * d(af_ref, wf_ref)
    u = u + _sigmoid(gc_ref[...]) * d(ac_ref, wc_ref)
    u_ref[...] = u.astype(u_ref.dtype)


def _merge(am, af, ac, wm, wf, wc, h2, g_off, *, tm, tn):
    m, k = am.shape
    n = wm.shape[1]
    tm = min(tm, m)
    a_spec = pl.BlockSpec((tm, k), lambda i, j: (i, 0))
    w_spec = pl.BlockSpec((k, tn), lambda i, j: (0, j))
    g_spec = lambda b: pl.BlockSpec((tm, tn), lambda i, j, b=b: (i, (g_off + b * n) // tn + j))
    return pl.pallas_call(
        _merge_kernel,
        grid=(m // tm, n // tn),
        in_specs=[a_spec, a_spec, a_spec, w_spec, w_spec, w_spec, g_spec(0), g_spec(1), g_spec(2)],
        out_specs=pl.BlockSpec((tm, tn), lambda i, j: (i, j)),
        out_shape=jax.ShapeDtypeStruct((m, n), BF16),
        compiler_params=_params("parallel", "arbitrary"),
        name="merge",
    )(am, af, ac, wm, wf, wc, h2, h2, h2)


def _out_proj_kernel(u_ref, w_ref, x_ref, y_ref):
    y_ref[...] = x_ref[...] + jnp.dot(u_ref[...], w_ref[...], preferred_element_type=F32)


def _out_proj(u, w, x, *, tm, tn):
    m, k = u.shape
    n = w.shape[1]
    tm = min(tm, m)
    return pl.pallas_call(
        _out_proj_kernel,
        grid=(m // tm, n // tn),
        in_specs=[
            pl.BlockSpec((tm, k), lambda i, j: (i, 0)),
            pl.BlockSpec((k, tn), lambda i, j: (0, j)),
            pl.BlockSpec((tm, tn), lambda i, j: (i, j)),
        ],
        out_specs=pl.BlockSpec((tm, tn), lambda i, j: (i, j)),
        out_shape=jax.ShapeDtypeStruct((m, n), F32),
        compiler_params=_params("parallel", "arbitrary"),
        name="out_proj",
    )(u, w, x)


def _column_layout(d_model):
    ml_dv = d_model // ML_HEADS
    ml_qkw = ML_HEADS * (ml_dv // 2)
    names = ("mq", "mk", "mv", "mi", "mf", "mo", "mz", "fq", "fk", "fv", "ff", "fz", "cq", "cz", "g")
    widths = (ml_qkw, ml_qkw, d_model, ML_HEADS, ML_HEADS, d_model, d_model,
              d_model, d_model, d_model, FOX_HEADS, d_model, d_model, d_model, N_BRANCH * d_model)
    src, off = {}, 0
    for n, w in zip(names, widths):
        src[n] = (off, w)
        off += w
    big, boff = {}, 0
    for n in names:
        if n in ("mi", "mf", "ff"):
            continue
        big[n] = boff
        boff += src[n][1]
    return src, big, boff


def _layer(x_p, x_s, mem_p, fox_k_pool, fox_v_pool, fox_lf_pool, page_table, mem_k_s, mem_v_s,
           c0_s, n0_s, m0_s, norm_in, norm_mem, w_in, w_mem_kv, b_i, b_f, b_ff, ml_out_norm,
           fq_norm, fk_norm, cq_norm, ck_norm, w_br_m, w_br_f, w_br_c, w_out):
    B, T, D = x_p.shape
    N = x_s.shape[0]
    dv = D // ML_HEADS
    dqk = dv // 2
    fox_dh = D // FOX_HEADS
    mem_dh = D // MEM_HEADS
    src, cols, nbig = _column_layout(D)
    w_in_t = w_in.T
    gap1, gap2 = cols["mo"], cols["fz"]
    skip1, skip2 = 2 * ML_HEADS, 2 * ML_HEADS + FOX_HEADS

    def big_rows(tn):
        assert gap1 % tn == 0 and gap2 % tn == 0
        return lambda j: j * tn + jnp.where(j * tn >= gap2, skip2, jnp.where(j * tn >= gap1, skip1, 0))

    rows = lambda n: w_in_t[src[n][0]:src[n][0] + src[n][1]]
    n_gate = 2 * ML_HEADS + FOX_HEADS
    w_gate_t = jnp.concatenate([rows("mi"), rows("mf"), rows("ff"),
                                jnp.zeros((GATE_LANES - n_gate, D), F32)], axis=0)
    b_gate = jnp.concatenate([b_i, b_f, b_ff, jnp.zeros((GATE_LANES - n_gate,), F32)]).reshape(1, GATE_LANES)
    wm, wf, wc, wo = (w.astype(BF16) for w in (w_br_m, w_br_f, w_br_c, w_out))

    xp2 = x_p.reshape(B * T, D)
    xs2 = x_s.reshape(N, D)

    h2, hs2 = _norm_proj(xp2, norm_in, w_in_t, tm=2048, tn=512, n_out=nbig, w_index=big_rows(512),
                         transposed=True, x_rider=xs2)
    g2 = _gates(xp2, norm_in, w_gate_t, b_gate, tm=1024)
    h3 = h2.reshape(B, T, nbig)
    g3 = g2.reshape(B, T, GATE_LANES)

    n_mem = mem_p.shape[1]
    mem2 = mem_p.reshape(B * n_mem, D)
    heads_k = D // mem_dh
    mk_p, _ = _norm_proj(mem2, norm_mem, w_mem_kv, tm=512, tn=mem_dh, n_out=D, w_index=lambda j: j,
                         transposed=False, head_gain=ck_norm)
    mv_p, _ = _norm_proj(mem2, norm_mem, w_mem_kv, tm=512, tn=mem_dh, n_out=D, w_index=lambda j: heads_k + j,
                         transposed=False)

    a_m, c_p, n_p, m_p = _mlstm_prompt(h3, g3, ml_out_norm, cols, dqk=dqk, dv=dv,
                                       chunks_per_step=4, heads_per_step=4)

    qa, ka, vb, kn, fv_p = _fox_prep(h3, g3, fq_norm, fk_norm, cols, width=D, dh=fox_dh, tp=256)
    a_f = _fox_flash(qa, ka, vb, h3, cols, dh=fox_dh, t=1024, sub=256, heads_per_step=2)

    a_c = _mem_attn_prompt(h3, mk_p.reshape(B, n_mem, D), mv_p.reshape(B, n_mem, D), cq_norm, cols,
                           width=D, dh=mem_dh, tq=512)

    u_p = _merge(a_m.reshape(B * T, D), a_f.reshape(B * T, D), a_c.reshape(B * T, D),
                 wm, wf, wc, h2, cols["g"], tm=1024, tn=512)
    y_p = _out_proj(u_p, wo, xp2, tm=1024, tn=1024).reshape(B, T, D)

    gs2 = _gates(xs2, norm_in, w_gate_t, b_gate, tm=128)
    hs3 = hs2.reshape(N, 1, nbig)
    gs3 = gs2.reshape(N, 1, GATE_LANES)

    a_ms, c_s, n_s, m_s = _mlstm_sample(hs3, gs3, m0_s.reshape(N, 1, ML_HEADS), ml_out_norm, c0_s, n0_s,
                                        cols, dqk=dqk, dv=dv, samples_per_step=4)

    hs1 = hs2.reshape(1, N, nbig)
    qn_s, kn_s = _qk_norm(hs1, fq_norm, fk_norm, cols, width=D, dh=fox_dh, tq=128)
    seg = lambda n: hs2[:, cols[n]:cols[n] + D]
    fv_s = seg("fv")
    flf_s = gs2[:, 2 * ML_HEADS:n_gate]
    lfn = jnp.broadcast_to(flf_s[:, :, None], (N, FOX_HEADS, LANES))
    to_heads = lambda a: a.reshape(N, FOX_HEADS, fox_dh)
    o_f = _fox_decode(page_table, to_heads(qn_s.reshape(N, D)), to_heads(kn_s.reshape(N, D)), to_heads(fv_s),
                      to_heads(seg("fz")), lfn, fox_k_pool, fox_v_pool, jnp.swapaxes(fox_lf_pool, 1, 2),
                      dh=fox_dh, G=8)
    a_fs = o_f.reshape(N, D).astype(BF16)

    pad8 = lambda a: jnp.pad(a.reshape(N, MEM_HEADS, mem_dh), ((0, 0), (0, 8 - MEM_HEADS), (0, 0)))
    o_c = _mem_decode(pad8(seg("cq")), pad8(seg("cz")), mem_k_s, mem_v_s, cq_norm, dh=mem_dh, samples_per_step=4)
    a_cs = o_c[:, :MEM_HEADS].reshape(N, D).astype(BF16)

    u_s = _merge(a_ms.reshape(N, D), a_fs, a_cs, wm, wf, wc, hs2, cols["g"], tm=128, tn=512)
    y_s = _out_proj(u_s, wo, xs2, tm=128, tn=1024).reshape(N, 1, D)

    outs_p = (kn.reshape(B, T, FOX_HEADS, fox_dh), fv_p.reshape(B, T, FOX_HEADS, fox_dh),
              g3[:, :, 2 * ML_HEADS:n_gate],
              mk_p.reshape(B, n_mem, MEM_HEADS, mem_dh), mv_p.reshape(B, n_mem, MEM_HEADS, mem_dh),
              c_p, n_p.reshape(B, ML_HEADS, dqk), m_p.reshape(B, ML_HEADS))
    outs_s = (kn_s.reshape(N, 1, FOX_HEADS, fox_dh), fv_s.reshape(N, 1, FOX_HEADS, fox_dh),
              flf_s.reshape(N, 1, FOX_HEADS), c_s, n_s, m_s.reshape(N, ML_HEADS))
    return y_p, y_s, outs_p, outs_s


def kernel(x_prompt, x_sample, mem_prompt, cache_fox_k, cache_fox_v, cache_fox_logf, page_table, cache_mem_k, cache_mem_v, state_mlstm_C, state_mlstm_n, state_mlstm_m, norm_in, norm_mem, w_in, w_mem_kv, b_mlstm_i, b_mlstm_f, b_fox_f, mlstm_out_norm, fox_q_norm, fox_k_norm, mem_q_norm, mem_k_norm, w_br_mlstm, w_br_fox, w_br_mem, w_out):
    depth = w_in.shape[0]
    y_p, y_s = x_prompt, x_sample
    per_layer = []
    for l in range(depth):
        y_p, y_s, outs_p, outs_s = _layer(
            y_p, y_s, mem_prompt, cache_fox_k[l], cache_fox_v[l], cache_fox_logf[l], page_table,
            cache_mem_k[l], cache_mem_v[l], state_mlstm_C[l], state_mlstm_n[l], state_mlstm_m[l],
            norm_in[l], norm_mem[l], w_in[l], w_mem_kv[l], b_mlstm_i[l], b_mlstm_f[l], b_fox_f[l],
            mlstm_out_norm[l], fox_q_norm[l], fox_k_norm[l], mem_q_norm[l], mem_k_norm[l],
            w_br_mlstm[l], w_br_fox[l], w_br_mem[l], w_out[l])
        per_layer.append(outs_p + outs_s)
    stacked = tuple(jnp.stack([lay[i] for lay in per_layer]) for i in range(len(per_layer[0])))
    return (y_p, y_s) + stacked
```

```python
import functools
import math

import jax
import jax.numpy as jnp
from jax import lax
from jax.experimental import pallas as pl
from jax.experimental.pallas import tpu as pltpu

F32 = jnp.float32
BF16 = jnp.bfloat16

ML_HEADS = 4
FOX_HEADS = 16
MEM_HEADS = 4
N_BRANCH = 3
ML_CHUNK = 128
EPS = 1e-6
LOG2E = 1.4426950408889634
LANES = 128
GATE_LANES = LANES
VMEM_LIMIT = 56 * 1024 * 1024

NT_DIMS = (((1,), (1,)), ((), ()))
TN_DIMS = (((0,), (0,)), ((), ()))


def _params(*sem):
    return pltpu.CompilerParams(dimension_semantics=sem, vmem_limit_bytes=VMEM_LIMIT)


def _sigmoid(z):
    return 1.0 / (1.0 + jnp.exp(-z))


def _log_sigmoid(z):
    return jnp.minimum(z, 0.0) - jnp.log1p(jnp.exp(-jnp.abs(z)))


def _split3(x):
    x1 = x.astype(BF16)
    r1 = x - x1.astype(F32)
    x2 = r1.astype(BF16)
    r2 = r1 - x2.astype(F32)
    return x1, x2, r2.astype(BF16)


def _dot_sel_left(sel, x):
    x1, x2, x3 = _split3(x)
    d = lambda p: jnp.dot(sel, p, preferred_element_type=F32)
    return d(x1) + d(x2) + d(x3)


def _rms_rows(x, gain):
    return x * lax.rsqrt(jnp.mean(x * x, axis=-1, keepdims=True) + EPS) * gain


def _norm_proj_kernel(*refs, head_norm, transposed, rider):
    refs = list(refs)
    x_ref = refs.pop(0)
    xr_ref = refs.pop(0) if rider else None
    g_ref, w_ref = refs.pop(0), refs.pop(0)
    hg_ref = refs.pop(0) if head_norm else None
    o_ref = refs.pop(0)
    or_ref = refs.pop(0) if rider else None
    xn_ref, = refs
    tm = x_ref.shape[0]

    @pl.when(pl.program_id(1) == 0)
    def _():
        xn_ref[:tm, :] = _rms_rows(x_ref[...], g_ref[...]).astype(BF16)
        if rider:
            xn_ref[tm:, :] = _rms_rows(xr_ref[...], g_ref[...]).astype(BF16)

    w = w_ref[...].astype(BF16)
    if transposed:
        acc = lax.dot_general(xn_ref[...], w, NT_DIMS, preferred_element_type=F32)
    else:
        acc = jnp.dot(xn_ref[...], w, preferred_element_type=F32)
    if head_norm:
        acc = _rms_rows(acc, hg_ref[...])
    o_ref[...] = acc[:tm].astype(o_ref.dtype)
    if rider:
        or_ref[...] = acc[tm:].astype(or_ref.dtype)


def _norm_proj(x, gain, w, *, tm, tn, n_out, w_index, transposed, head_gain=None, x_rider=None):
    m, k = x.shape
    n = n_out
    tm = min(tm, m)
    assert m % tm == 0 and n % tn == 0
    rider = x_rider is not None
    tr = 0
    if rider:
        tr = x_rider.shape[0] // (m // tm)
        assert tr * (m // tm) == x_rider.shape[0] and tr % 16 == 0
    if transposed:
        w_spec = pl.BlockSpec((pl.Element(tn), pl.Element(k)), lambda i, j: (pl.multiple_of(w_index(j), 8), 0))
    else:
        w_spec = pl.BlockSpec((k, tn), lambda i, j: (0, w_index(j)))
    in_specs = [pl.BlockSpec((tm, k), lambda i, j: (i, 0), pipeline_mode=pl.Buffered(1))]
    args = [x]
    if rider:
        in_specs.append(pl.BlockSpec((tr, k), lambda i, j: (i, 0)))
        args.append(x_rider)
    in_specs += [pl.BlockSpec((1, k), lambda i, j: (0, 0)), w_spec]
    args += [gain.reshape(1, k), w]
    if head_gain is not None:
        assert head_gain.shape == (tn,)
        in_specs.append(pl.BlockSpec((1, tn), lambda i, j: (0, 0)))
        args.append(head_gain.reshape(1, tn))
    out_specs = [pl.BlockSpec((tm, tn), lambda i, j: (i, j))]
    out_shape = [jax.ShapeDtypeStruct((m, n), F32)]
    if rider:
        out_specs.append(pl.BlockSpec((tr, tn), lambda i, j: (i, j)))
        out_shape.append(jax.ShapeDtypeStruct((x_rider.shape[0], n), F32))
    outs = pl.pallas_call(
        functools.partial(_norm_proj_kernel, head_norm=head_gain is not None, transposed=transposed, rider=rider),
        grid=(m // tm, n // tn),
        in_specs=in_specs,
        out_specs=out_specs,
        out_shape=out_shape,
        scratch_shapes=[pltpu.VMEM((tm + tr, k), BF16)],
        compiler_params=_params("parallel", "arbitrary"),
        name="norm_proj",
    )(*args)
    return (outs[0], outs[1]) if rider else (outs[0], None)


def _gates_kernel(x_ref, g_ref, w_ref, b_ref, o_ref):
    xn = _rms_rows(x_ref[...], g_ref[...])
    x1, x2, _ = _split3(xn)
    w1, w2, _ = _split3(w_ref[...])
    d = lambda a, b: lax.dot_general(a, b, NT_DIMS, preferred_element_type=F32)
    z = d(x1, w1) + d(x1, w2) + d(x2, w1) + b_ref[...]
    lane = lax.broadcasted_iota(jnp.int32, z.shape, 1)
    o_ref[...] = jnp.where(lane < ML_HEADS, z, _log_sigmoid(z))


def _gates(x, gain, w_gate, bias, *, tm):
    m, k = x.shape
    tm = min(tm, m)
    return pl.pallas_call(
        _gates_kernel,
        grid=(m // tm,),
        in_specs=[
            pl.BlockSpec((tm, k), lambda i: (i, 0)),
            pl.BlockSpec((1, k), lambda i: (0, 0)),
            pl.BlockSpec((GATE_LANES, k), lambda i: (0, 0)),
            pl.BlockSpec((1, GATE_LANES), lambda i: (0, 0)),
        ],
        out_specs=pl.BlockSpec((tm, GATE_LANES), lambda i: (i, 0)),
        out_shape=jax.ShapeDtypeStruct((m, GATE_LANES), F32),
        compiler_params=_params("parallel"),
        name="gates",
    )(x, gain.reshape(1, k), w_gate, bias)


def _mlstm_chunk_kernel(q_ref, k_ref, v_ref, o_ref, z_ref, g_ref, gain_ref,
                        a_ref, c_out, n_out, m_out, c_sc, n_sc, m_sc, *, dqk, dv, L, HP):
    hp = pl.program_id(1)
    step = pl.program_id(2)
    n_chunks = q_ref.shape[0] // L

    @pl.when(step == 0)
    def _():
        c_sc[...] = jnp.zeros_like(c_sc)
        n_sc[...] = jnp.zeros_like(n_sc)
        m_sc[...] = jnp.zeros_like(m_sc)

    row = lax.broadcasted_iota(jnp.int32, (L, L), 0)
    col = lax.broadcasted_iota(jnp.int32, (L, L), 1)
    causal = col <= row
    tri = jnp.where(causal, 1.0, 0.0).astype(BF16)
    lane = lax.broadcasted_iota(jnp.int32, (L, GATE_LANES), 1)
    sub = lax.broadcasted_iota(jnp.int32, (GATE_LANES, L), 0)
    pick_col = lambda x, idx: jnp.sum(jnp.where(lane == idx, x, 0.0), axis=1, keepdims=True)
    pick_row = lambda x, idx: jnp.sum(jnp.where(sub == idx, x, 0.0), axis=0, keepdims=True)

    def chunk(ci, carry):
        rows = pl.ds(pl.multiple_of(ci * L, L), L)
        g = g_ref[rows, :]
        cum = _dot_sel_left(tri, g)
        g_t = g.T
        cum_t = cum.T
        for hh in range(HP):
            h = hp * HP + hh
            qs = slice(hh * dqk, (hh + 1) * dqk)
            vs = slice(hh * dv, (hh + 1) * dv)
            ig_col = pick_col(g, h)
            b_col = pick_col(cum, ML_HEADS + h)
            ig_row = pick_row(g_t, h)
            b_row = pick_row(cum_t, ML_HEADS + h)

            m_prev = m_sc[hh]
            a_col = b_col + m_prev
            dmat = jnp.where(causal, b_col - b_row + ig_row, -jnp.inf)
            m_t = jnp.maximum(a_col, jnp.max(dmat, axis=1, keepdims=True))
            w_inter = jnp.exp(a_col - m_t)
            w_intra = jnp.exp(dmat - m_t)

            qf = q_ref[rows, qs] * (dqk ** -0.5)
            kf = k_ref[rows, qs]
            vf = v_ref[rows, vs]
            q = qf.astype(BF16)
            k = kf.astype(BF16)
            v = vf.astype(BF16)
            c_prev = c_sc[hh]
            n_prev = n_sc[hh]

            s = lax.dot_general(q, k, NT_DIMS, preferred_element_type=F32) * w_intra
            inter = lax.dot_general(q, c_prev.astype(BF16), NT_DIMS, preferred_element_type=F32)
            num = w_inter * inter + jnp.dot(s.astype(BF16), v, preferred_element_type=F32)
            nq = w_inter * jnp.sum(qf * n_prev, axis=1, keepdims=True) + jnp.sum(s, axis=1, keepdims=True)
            hid = num / jnp.maximum(jnp.abs(nq), jnp.exp(-m_t))

            b_last = b_col[L - 1:L, :]
            m_new = m_t[L - 1:L, :]
            w_c = jnp.exp(b_last + m_prev - m_new)
            w_s = jnp.exp(b_last - b_col + ig_col - m_new)
            wv = (w_s * vf).astype(BF16)
            c_sc[hh] = w_c * c_prev + lax.dot_general(wv, k, TN_DIMS, preferred_element_type=F32)
            n_sc[hh] = w_c * n_prev + jnp.sum(w_s * kf, axis=0, keepdims=True)
            m_sc[hh] = m_new

            hn = _rms_rows(hid, gain_ref[:, vs])
            zz = z_ref[rows, vs]
            a_ref[rows, vs] = (zz * _sigmoid(zz) * (_sigmoid(o_ref[rows, vs]) * hn)).astype(a_ref.dtype)
        return carry

    lax.fori_loop(0, n_chunks, chunk, 0, unroll=2)

    @pl.when(step == pl.num_programs(2) - 1)
    def _():
        c_out[...] = c_sc[...]
        n_out[...] = n_sc[...]
        m_out[...] = m_sc[...]


def _mlstm_prompt(h3, g3, out_gain, cols, *, dqk, dv, chunks_per_step, heads_per_step):
    B, T, _ = h3.shape
    L = math.gcd(T, ML_CHUNK)
    rows = L * math.gcd(T // L, chunks_per_step)
    HP = math.gcd(ML_HEADS, heads_per_step)
    qo, ko, vo, oo, zo = (cols[n] for n in ("mq", "mk", "mv", "mo", "mz"))
    blk = lambda w, off: pl.BlockSpec((None, rows, HP * w),
                                      lambda b, h, c, off=off, w=w: (b, c, off // (HP * w) + h))
    kern = functools.partial(_mlstm_chunk_kernel, dqk=dqk, dv=dv, L=L, HP=HP)
    return pl.pallas_call(
        kern,
        grid=(B, ML_HEADS // HP, T // rows),
        in_specs=[
            blk(dqk, qo), blk(dqk, ko), blk(dv, vo), blk(dv, oo), blk(dv, zo),
            pl.BlockSpec((None, rows, GATE_LANES), lambda b, h, c: (b, c, 0)),
            pl.BlockSpec((1, HP * dv), lambda b, h, c: (0, h)),
        ],
        out_specs=[
            pl.BlockSpec((None, rows, HP * dv), lambda b, h, c: (b, c, h)),
            pl.BlockSpec((None, HP, dv, dqk), lambda b, h, c: (b, h, 0, 0)),
            pl.BlockSpec((None, HP, 1, dqk), lambda b, h, c: (b, h, 0, 0)),
            pl.BlockSpec((None, HP, 1, 1), lambda b, h, c: (b, h, 0, 0)),
        ],
        out_shape=[
            jax.ShapeDtypeStruct((B, T, ML_HEADS * dv), BF16),
            jax.ShapeDtypeStruct((B, ML_HEADS, dv, dqk), F32),
            jax.ShapeDtypeStruct((B, ML_HEADS, 1, dqk), F32),
            jax.ShapeDtypeStruct((B, ML_HEADS, 1, 1), F32),
        ],
        scratch_shapes=[pltpu.VMEM((HP, dv, dqk), F32), pltpu.VMEM((HP, 1, dqk), F32),
                        pltpu.VMEM((HP, 1, 1), F32)],
        compiler_params=_params("parallel", "parallel", "arbitrary"),
        name="mlstm_chunk",
    )(h3, h3, h3, h3, h3, g3, out_gain.reshape(1, ML_HEADS * dv))


def _mlstm_step_kernel(q_ref, k_ref, v_ref, o_ref, z_ref, g_ref, m0_ref, gain_ref, c0_ref, n0_ref,
                       a_ref, c_out, n_out, m_out, *, dqk, dv):
    lane_h = lax.broadcasted_iota(jnp.int32, (1, ML_HEADS), 1)
    first = lax.broadcasted_iota(jnp.int32, (16, 1), 0) == 0
    for s_i in range(q_ref.shape[0]):
        g = g_ref[s_i]
        m0 = m0_ref[s_i]
        m_acc = jnp.zeros((1, ML_HEADS), F32)
        for h in range(ML_HEADS):
            ig = g[:, h:h + 1]
            lf = g[:, ML_HEADS + h:ML_HEADS + h + 1]
            a = lf + m0[:, h:h + 1]
            m_t = jnp.maximum(a, ig)
            w_inter = jnp.exp(a - m_t)
            w_intra = jnp.exp(ig - m_t)
            qf = q_ref[s_i, :, h * dqk:(h + 1) * dqk] * (dqk ** -0.5)
            kf = k_ref[s_i, :, h * dqk:(h + 1) * dqk]
            vf = v_ref[s_i, :, h * dv:(h + 1) * dv]
            c0 = c0_ref[s_i, h]
            n0 = n0_ref[s_i, h:h + 1, :]
            s = jnp.sum(qf * kf, axis=1, keepdims=True) * w_intra
            q8 = jnp.broadcast_to(qf, (8, dqk)).astype(BF16)
            cq = lax.dot_general(q8, c0.astype(BF16), NT_DIMS, preferred_element_type=F32)[0:1, :]
            num = w_inter * cq + s * vf
            nq = w_inter * jnp.sum(n0 * qf, axis=1, keepdims=True) + s
            hid = num / jnp.maximum(jnp.abs(nq), jnp.exp(-m_t))
            v16 = jnp.where(first, vf, 0.0).astype(BF16)
            k16 = jnp.where(first, w_intra * kf, 0.0).astype(BF16)
            c_out[s_i, h] = w_inter * c0 + lax.dot_general(v16, k16, TN_DIMS, preferred_element_type=F32)
            n_out[s_i, h:h + 1, :] = w_inter * n0 + w_intra * kf
            m_acc = m_acc + jnp.where(lane_h == h, m_t, 0.0)
            hn = _rms_rows(hid, gain_ref[:, h * dv:(h + 1) * dv])
            zz = z_ref[s_i, :, h * dv:(h + 1) * dv]
            gate_o = _sigmoid(o_ref[s_i, :, h * dv:(h + 1) * dv])
            a_ref[s_i, :, h * dv:(h + 1) * dv] = (zz * _sigmoid(zz) * (gate_o * hn)).astype(a_ref.dtype)
        m_out[s_i] = m_acc


def _mlstm_sample(hs3, gs3, m0, out_gain, c0, n0, cols, *, dqk, dv, samples_per_step):
    N = hs3.shape[0]
    S = math.gcd(N, samples_per_step)
    W = ML_HEADS * dv
    WQ = ML_HEADS * dqk
    qo, ko, vo, oo, zo = (cols[n] for n in ("mq", "mk", "mv", "mo", "mz"))
    row = lambda w, off: pl.BlockSpec((S, 1, w), lambda b, off=off, w=w: (b, 0, off // w))
    kern = functools.partial(_mlstm_step_kernel, dqk=dqk, dv=dv)
    return pl.pallas_call(
        kern,
        grid=(N // S,),
        in_specs=[
            row(WQ, qo), row(WQ, ko), row(W, vo), row(W, oo), row(W, zo),
            pl.BlockSpec((S, 1, GATE_LANES), lambda b: (b, 0, 0)),
            pl.BlockSpec((S, 1, ML_HEADS), lambda b: (b, 0, 0)),
            pl.BlockSpec((1, W), lambda b: (0, 0)),
            pl.BlockSpec((S, ML_HEADS, dv, dqk), lambda b: (b, 0, 0, 0)),
            pl.BlockSpec((S, ML_HEADS, dqk), lambda b: (b, 0, 0)),
        ],
        out_specs=[
            pl.BlockSpec((S, 1, W), lambda b: (b, 0, 0)),
            pl.BlockSpec((S, ML_HEADS, dv, dqk), lambda b: (b, 0, 0, 0)),
            pl.BlockSpec((S, ML_HEADS, dqk), lambda b: (b, 0, 0)),
            pl.BlockSpec((S, 1, ML_HEADS), lambda b: (b, 0, 0)),
        ],
        out_shape=[
            jax.ShapeDtypeStruct((N, 1, W), BF16),
            jax.ShapeDtypeStruct((N, ML_HEADS, dv, dqk), F32),
            jax.ShapeDtypeStruct((N, ML_HEADS, dqk), F32),
            jax.ShapeDtypeStruct((N, 1, ML_HEADS), F32),
        ],
        compiler_params=_params("parallel"),
        name="mlstm_step",
    )(hs3, hs3, hs3, hs3, hs3, gs3, m0, out_gain.reshape(1, W), c0, n0)


N_AUG = LANES


def _qk_norm_kernel(q_ref, k_ref, qg_ref, kg_ref, qn_ref, kn_ref, *, dh):
    nh = q_ref.shape[1] // dh
    for h in range(nh):
        sl = slice(h * dh, (h + 1) * dh)
        qn_ref[:, sl] = _rms_rows(q_ref[:, sl], qg_ref[...]).astype(qn_ref.dtype)
        kn_ref[:, sl] = _rms_rows(k_ref[:, sl], kg_ref[...])


def _qk_norm(h3, q_gain, k_gain, cols, *, width, dh, tq):
    B, T, _ = h3.shape
    tq = min(tq, T)
    spec = lambda off: pl.BlockSpec((None, tq, width), lambda b, i, off=off: (b, i, off // width))
    out = pl.BlockSpec((None, tq, width), lambda b, i: (b, i, 0))
    gain = pl.BlockSpec((1, dh), lambda b, i: (0, 0))
    return pl.pallas_call(
        functools.partial(_qk_norm_kernel, dh=dh),
        grid=(B, T // tq),
        in_specs=[spec(cols["fq"]), spec(cols["fk"]), gain, gain],
        out_specs=[out, out],
        out_shape=[jax.ShapeDtypeStruct((B, T, width), BF16), jax.ShapeDtypeStruct((B, T, width), F32)],
        compiler_params=_params("parallel", "parallel"),
        name="qk_norm",
    )(h3, h3, q_gain.reshape(1, dh), k_gain.reshape(1, dh))


def _aug_placement(nh, lane0):
    r = jnp.arange(3 * GATE_LANES)[None, :, None]
    c = jnp.arange(N_AUG)[None, None, :]
    src_lane = (lane0 + jnp.arange(nh))[:, None, None]
    piece = r // GATE_LANES
    hit = (r % GATE_LANES) == src_lane
    pq = jnp.where(hit & (c == piece + 3), 1.0, 0.0)
    pk = jnp.where(hit & (c == piece), -1.0, 0.0)
    return jnp.stack([pq, pk], axis=1).astype(BF16)


def _fox_prep_kernel(q_ref, k_ref, v_ref, g_ref, qg_ref, kg_ref, place_ref,
                     qa_ref, ka_ref, vb_ref, kn_ref, vf_ref, carry_sc, *, dh, qscale):
    tp = q_ref.shape[0]
    nh = q_ref.shape[1] // dh

    @pl.when(pl.program_id(1) == 0)
    def _():
        carry_sc[...] = jnp.zeros_like(carry_sc)

    row = lax.broadcasted_iota(jnp.int32, (tp, tp), 0)
    col = lax.broadcasted_iota(jnp.int32, (tp, tp), 1)
    tri = jnp.where(col <= row, 1.0, 0.0).astype(BF16)
    cum = _dot_sel_left(tri, g_ref[...]) + carry_sc[...]
    carry_sc[...] = cum[tp - 1:tp, :]
    f123 = jnp.concatenate(_split3(cum * LOG2E), axis=1)

    v = v_ref[...]
    vf_ref[...] = v
    vb_ref[...] = v.astype(BF16)
    lane = lax.broadcasted_iota(jnp.int32, (tp, N_AUG), 1)
    q_ones = lane < 3
    k_ones = jnp.abs(lane - 4) <= 1
    for h in range(nh):
        sl = slice(h * dh, (h + 1) * dh)
        kn = _rms_rows(k_ref[:, sl], kg_ref[...])
        kn_ref[:, sl] = kn
        qn = _rms_rows(q_ref[:, sl], qg_ref[...]) * qscale
        eq = jnp.where(q_ones, 1.0, jnp.dot(f123, place_ref[h, 0], preferred_element_type=F32))
        ek = jnp.where(k_ones, 1.0, jnp.dot(f123, place_ref[h, 1], preferred_element_type=F32))
        base = h * (dh + N_AUG)
        qa_ref[:, base:base + dh] = qn.astype(BF16)
        qa_ref[:, base + dh:base + dh + N_AUG] = eq.astype(BF16)
        ka_ref[:, base:base + dh] = kn.astype(BF16)
        ka_ref[:, base + dh:base + dh + N_AUG] = ek.astype(BF16)


def _fox_prep(h3, g3, q_gain, k_gain, cols, *, width, dh, tp):
    B, T, _ = h3.shape
    tp = min(tp, T)
    nh = width // dh
    wa = nh * (dh + N_AUG)
    place = _aug_placement(nh, 2 * ML_HEADS)
    spec = lambda off: pl.BlockSpec((None, tp, width), lambda b, i, off=off: (b, i, off // width))
    out = lambda w: pl.BlockSpec((None, tp, w), lambda b, i: (b, i, 0))
    gain = pl.BlockSpec((1, dh), lambda b, i: (0, 0))
    return pl.pallas_call(
        functools.partial(_fox_prep_kernel, dh=dh, qscale=dh ** -0.5 * LOG2E),
        grid=(B, T // tp),
        in_specs=[spec(cols["fq"]), spec(cols["fk"]), spec(cols["fv"]),
                  pl.BlockSpec((None, tp, GATE_LANES), lambda b, i: (b, i, 0)), gain, gain,
                  pl.BlockSpec(place.shape, lambda b, i: (0, 0, 0, 0))],
        out_specs=[out(wa), out(wa), out(width), out(width), out(width)],
        out_shape=[jax.ShapeDtypeStruct((B, T, wa), BF16), jax.ShapeDtypeStruct((B, T, wa), BF16),
                   jax.ShapeDtypeStruct((B, T, width), BF16),
                   jax.ShapeDtypeStruct((B, T, width), F32), jax.ShapeDtypeStruct((B, T, width), F32)],
        scratch_shapes=[pltpu.VMEM((1, GATE_LANES), F32)],
        compiler_params=_params("parallel", "arbitrary"),
        name="fox_prep",
    )(h3, h3, h3, g3, q_gain.reshape(1, dh), k_gain.reshape(1, dh), place)


def _lane_tiles_reduce(x, op):
    acc = x[:, :LANES]
    for j in range(1, x.shape[1] // LANES):
        acc = op(acc, x[:, j * LANES:(j + 1) * LANES])
    return acc


def _fox_flash_kernel(q_ref, k_ref, v_ref, z_ref, o_ref, m_sc, l_sc, acc_sc, mx_sc, s_sc, p_sc,
                      *, t, sub, dh, HP):
    nq = q_ref.shape[0] // t
    ns = t // sub
    da = dh + N_AUG
    row_l = lax.broadcasted_iota(jnp.int32, (sub, sub), 0)
    col_l = lax.broadcasted_iota(jnp.int32, (sub, sub), 1)

    def block(diag, q0, k0):
        n_chunks = lambda i: i + 1 if diag else ns
        for hh in range(HP):
            ha = slice(hh * da, (hh + 1) * da)
            for i in range(ns):
                rows = slice(i * sub, (i + 1) * sub)
                q_i = q_ref[pl.ds(q0 + i * sub, sub), ha]
                mx = None
                for c in range(n_chunks(i)):
                    cs = slice(c * sub, (c + 1) * sub)
                    s = lax.dot_general(q_i, k_ref[pl.ds(k0 + c * sub, sub), ha], NT_DIMS,
                                        preferred_element_type=F32)
                    if diag and c == i:
                        s = jnp.where(col_l <= row_l, s, -jnp.inf)
                    s_sc[hh, rows, cs] = s
                    cm = _lane_tiles_reduce(s, jnp.maximum)
                    mx = cm if mx is None else jnp.maximum(mx, cm)
                mx_sc[hh, rows, :] = mx
        for hh in range(HP):
            hd = slice(hh * dh, (hh + 1) * dh)
            for i in range(ns):
                rows = slice(i * sub, (i + 1) * sub)
                m_prev = m_sc[hh, rows, :]
                m_new = jnp.maximum(m_prev, jnp.max(mx_sc[hh, rows, :], axis=1, keepdims=True))
                alpha = jnp.exp2(m_prev - m_new)
                m_wide = jnp.tile(m_new, (1, sub // LANES))
                ls = None
                for c in range(n_chunks(i)):
                    cs = slice(c * sub, (c + 1) * sub)
                    p = jnp.exp2(s_sc[hh, rows, cs] - m_wide)
                    p_sc[hh, rows, cs] = p.astype(BF16)
                    lp = _lane_tiles_reduce(p, jnp.add)
                    ls = lp if ls is None else ls + lp
                w = n_chunks(i) * sub
                l_sc[hh, rows, :] = alpha * l_sc[hh, rows, :] + jnp.sum(ls, axis=1, keepdims=True)
                acc_sc[hh, rows, :] = alpha * acc_sc[hh, rows, :] + jnp.dot(
                    p_sc[hh, rows, :w], v_ref[pl.ds(k0, w), hd], preferred_element_type=F32)
                m_sc[hh, rows, :] = m_new

    def q_tile(qi, carry):
        q0 = pl.multiple_of(qi * t, t)
        m_sc[...] = jnp.full_like(m_sc, -jnp.inf)
        l_sc[...] = jnp.zeros_like(l_sc)
        acc_sc[...] = jnp.zeros_like(acc_sc)

        def k_tile(ki, c):
            block(False, q0, pl.multiple_of(ki * t, t))
            return c

        lax.fori_loop(0, qi, k_tile, 0)

        def diag_tile(ki, c):
            block(True, q0, q0)
            return c

        lax.fori_loop(qi, qi + 1, diag_tile, 0)
        for hh in range(HP):
            hd = slice(hh * dh, (hh + 1) * dh)
            zz = z_ref[pl.ds(q0, t), hd]
            o_ref[pl.ds(q0, t), hd] = (zz * _sigmoid(zz) * (acc_sc[hh] / l_sc[hh])).astype(o_ref.dtype)
        return carry

    lax.fori_loop(0, nq, q_tile, 0)


def _fox_flash(qa, ka, vb, h3, cols, *, dh, t, sub, heads_per_step):
    B, T, W = vb.shape
    assert dh == LANES
    t = min(t, T)
    sub = min(sub, t)
    nh = W // dh
    HP = math.gcd(nh, heads_per_step)
    da = dh + N_AUG
    zo = cols["fz"] // (HP * dh)
    head = lambda w, off=0: pl.BlockSpec((None, T, HP * w), lambda b, h, off=off: (b, 0, off + h))
    return pl.pallas_call(
        functools.partial(_fox_flash_kernel, t=t, sub=sub, dh=dh, HP=HP),
        grid=(B, nh // HP),
        in_specs=[head(da), head(da), head(dh), head(dh, zo)],
        out_specs=head(dh),
        out_shape=jax.ShapeDtypeStruct((B, T, W), BF16),
        scratch_shapes=[pltpu.VMEM((HP, t, LANES), F32), pltpu.VMEM((HP, t, LANES), F32),
                        pltpu.VMEM((HP, t, dh), F32), pltpu.VMEM((HP, t, LANES), F32),
                        pltpu.VMEM((HP, t, t), F32), pltpu.VMEM((HP, t, t), BF16)],
        compiler_params=_params("parallel", "parallel"),
        name="fox_flash",
    )(qa, ka, vb, h3)


def _decay_selector(P, nh):
    t_src = jnp.arange(P)[:, None]
    t_dst = jnp.arange(P * nh)[None, :] // nh
    later = (t_src > t_dst).astype(BF16)
    return jnp.concatenate([later, jnp.ones((P, LANES), BF16)], axis=1)


def _fox_decode_kernel(pt_ref, q_ref, kn_ref, vn_ref, z_ref, lfn_ref, u_ref, k_hbm, v_hbm, *rest,
                       scale, nh, G, n_pages, n_buf):
    lf_refs = rest[:G]
    o_ref, m_sc, l_sc, acc_sc, carry_sc, k_buf, v_buf, k_sem, v_sem = rest[G:]
    b = pl.program_id(0)
    p = pl.program_id(1)
    steps_per_sample = pl.num_programs(1)
    step = b * steps_per_sample + p
    n_steps = pl.num_programs(0) * steps_per_sample

    def page_copies(s, slot):
        sb = s // steps_per_sample
        sp = s % steps_per_sample
        for g in range(G):
            page = pt_ref[sb, n_pages - 1 - (sp * G + g)]
            yield pltpu.make_async_copy(k_hbm.at[page], k_buf.at[slot, g], k_sem.at[slot])
            yield pltpu.make_async_copy(v_hbm.at[page], v_buf.at[slot, g], v_sem.at[slot])

    def start_group(s):
        for cp in page_copies(s, s % n_buf):
            cp.start()

    @pl.when(step == 0)
    def _():
        for s0 in range(n_buf - 1):
            start_group(s0)

    ahead = step + (n_buf - 1)

    @pl.when(ahead < n_steps)
    def _():
        start_group(ahead)

    slot = step % n_buf
    for cp in page_copies(step, slot):
        cp.wait()

    q = q_ref[...]

    @pl.when(p == 0)
    def _():
        s_new = jnp.sum(q.astype(F32) * kn_ref[...], axis=1, keepdims=True) * scale
        m_sc[...] = s_new
        l_sc[...] = jnp.ones_like(l_sc)
        acc_sc[...] = vn_ref[...]
        carry_sc[...] = lfn_ref[...]

    P, dh = k_buf.shape[2], k_buf.shape[4]
    n = P * nh
    sub = lax.broadcasted_iota(jnp.int32, (nh, n), 0)
    lane = lax.broadcasted_iota(jnp.int32, (nh, n), 1)
    diag = (lane & (nh - 1)) == sub
    carry = carry_sc[...]
    scores = []
    for g in range(G):
        k2 = k_buf[slot, g].reshape(n, dh).astype(BF16)
        s = lax.dot_general(q, k2, NT_DIMS, preferred_element_type=F32) * scale
        l1, l2, l3 = _split3(lf_refs[g][...])
        zz = jnp.dot(jnp.concatenate([l1, l2, l3], axis=0), u_ref[...], preferred_element_type=F32)
        zs = zz[0:nh] + zz[nh:2 * nh] + zz[2 * nh:3 * nh]
        s = s + zs[:, :n] + jnp.tile(carry, (1, n // LANES))
        carry = carry + zs[:, n:]
        scores.append(jnp.where(diag, s, -jnp.inf))
    carry_sc[...] = carry

    m_prev = m_sc[...]
    m_new = m_prev
    for s in scores:
        m_new = jnp.maximum(m_new, jnp.max(s, axis=1, keepdims=True))
    alpha = jnp.exp(m_prev - m_new)
    l_new = alpha * l_sc[...]
    acc = alpha * acc_sc[...]
    for g in range(G):
        pr = jnp.exp(scores[g] - m_new)
        v2 = v_buf[slot, g].reshape(n, dh).astype(BF16)
        l_new = l_new + jnp.sum(pr, axis=1, keepdims=True)
        acc = acc + jnp.dot(pr.astype(BF16), v2, preferred_element_type=F32)
    l_sc[...] = l_new
    acc_sc[...] = acc
    m_sc[...] = m_new

    @pl.when(p == steps_per_sample - 1)
    def _():
        zz = z_ref[...]
        o_ref[...] = zz * _sigmoid(zz) * (acc / l_new)


def _fox_decode(page_table, q3, kn3, vn3, z3, lfn3, k_pool, v_pool, lf_pool_t, *, dh, G, n_buf):
    N, nh, _ = q3.shape
    n_pages = page_table.shape[1]
    P = k_pool.shape[1]
    G = math.gcd(G, n_pages)
    assert nh & (nh - 1) == 0 and N * (n_pages // G) >= n_buf
    u = _decay_selector(P, nh)
    per = lambda: pl.BlockSpec((None, nh, dh), lambda b, p, pt: (b, 0, 0))
    page3 = lambda g: (lambda b, p, pt, g=g: (pt[b, n_pages - 1 - (p * G + g)], 0, 0))
    grid_spec = pltpu.PrefetchScalarGridSpec(
        num_scalar_prefetch=1,
        grid=(N, n_pages // G),
        in_specs=[per(), per(), per(), per(),
                  pl.BlockSpec((None, nh, LANES), lambda b, p, pt: (b, 0, 0)),
                  pl.BlockSpec(u.shape, lambda b, p, pt: (0, 0)),
                  pl.BlockSpec(memory_space=pl.ANY), pl.BlockSpec(memory_space=pl.ANY)]
                 + [pl.BlockSpec((None, nh, P), page3(g)) for g in range(G)],
        out_specs=pl.BlockSpec((None, nh, dh), lambda b, p, pt: (b, 0, 0)),
        scratch_shapes=[pltpu.VMEM((nh, 1), F32), pltpu.VMEM((nh, 1), F32), pltpu.VMEM((nh, dh), F32),
                        pltpu.VMEM((nh, LANES), F32),
                        pltpu.VMEM((n_buf, G, P, nh, dh), F32), pltpu.VMEM((n_buf, G, P, nh, dh), F32),
                        pltpu.SemaphoreType.DMA((n_buf,)), pltpu.SemaphoreType.DMA((n_buf,))],
    )
    return pl.pallas_call(
        functools.partial(_fox_decode_kernel, scale=dh ** -0.5, nh=nh, G=G, n_pages=n_pages, n_buf=n_buf),
        grid_spec=grid_spec,
        out_shape=jax.ShapeDtypeStruct((N, nh, dh), F32),
        compiler_params=_params("arbitrary", "arbitrary"),
        name="fox_decode",
    )(page_table, q3, kn3, vn3, z3, lfn3, u, k_pool, v_pool, *([lf_pool_t] * G))


def _mem_attn_kernel(q_ref, z_ref, k_ref, v_ref, qg_ref, a_ref, *, dh, scale):
    nh = q_ref.shape[1] // dh
    for h in range(nh):
        sl = slice(h * dh, (h + 1) * dh)
        q = _rms_rows(q_ref[:, sl], qg_ref[...]).astype(BF16)
        s = lax.dot_general(q, k_ref[:, sl].astype(BF16), NT_DIMS, preferred_element_type=F32) * scale
        e = jnp.exp(s - jnp.max(s, axis=1, keepdims=True))
        o = jnp.dot(e.astype(BF16), v_ref[:, sl].astype(BF16), preferred_element_type=F32)
        o = o / jnp.sum(e, axis=1, keepdims=True)
        zz = z_ref[:, sl]
        a_ref[:, sl] = (zz * _sigmoid(zz) * o).astype(a_ref.dtype)


def _mem_attn_prompt(h3, mk, mv, q_gain, cols, *, width, dh, tq):
    B, T, _ = h3.shape
    n_mem = mk.shape[1]
    tq = min(tq, T)
    spec = lambda off: pl.BlockSpec((None, tq, width), lambda b, i, off=off: (b, i, off // width))
    mem = pl.BlockSpec((None, n_mem, width), lambda b, i: (b, 0, 0))
    return pl.pallas_call(
        functools.partial(_mem_attn_kernel, dh=dh, scale=dh ** -0.5),
        grid=(B, T // tq),
        in_specs=[spec(cols["cq"]), spec(cols["cz"]), mem, mem, pl.BlockSpec((1, dh), lambda b, i: (0, 0))],
        out_specs=pl.BlockSpec((None, tq, width), lambda b, i: (b, i, 0)),
        out_shape=jax.ShapeDtypeStruct((B, T, width), BF16),
        compiler_params=_params("parallel", "parallel"),
        name="mem_attn",
    )(h3, h3, mk, mv, q_gain.reshape(1, dh))


def _mem_decode_kernel(q_ref, z_ref, k_ref, v_ref, qg_ref, o_ref, *, scale, nh):
    for s_i in range(q_ref.shape[0]):
        q = _rms_rows(q_ref[s_i], qg_ref[...]).astype(BF16)
        kp = k_ref[s_i]
        vp = v_ref[s_i]
        rows = kp.shape[0] * kp.shape[1]
        k2 = kp.reshape(rows, kp.shape[2]).astype(BF16)
        v2 = vp.reshape(rows, vp.shape[2]).astype(BF16)
        s = lax.dot_general(q, k2, NT_DIMS, preferred_element_type=F32) * scale
        sub = lax.broadcasted_iota(jnp.int32, s.shape, 0)
        lane = lax.broadcasted_iota(jnp.int32, s.shape, 1)
        s = jnp.where((lane & (nh - 1)) == sub, s, -jnp.inf)
        m = jnp.maximum(jnp.max(s, axis=1, keepdims=True), -1e30)
        e = jnp.exp(s - m)
        den = jnp.maximum(jnp.sum(e, axis=1, keepdims=True), 1e-30)
        o = jnp.dot(e.astype(BF16), v2, preferred_element_type=F32) / den
        zz = z_ref[s_i]
        o_ref[s_i] = zz * _sigmoid(zz) * o


def _mem_decode(q3, z3, k_cache, v_cache, q_gain, *, dh, samples_per_step):
    N, n_mem, nh, _ = k_cache.shape
    S = math.gcd(N, samples_per_step)
    assert nh & (nh - 1) == 0
    per = pl.BlockSpec((S, 8, dh), lambda b: (b, 0, 0))
    cache = pl.BlockSpec((S, n_mem, nh, dh), lambda b: (b, 0, 0, 0))
    return pl.pallas_call(
        functools.partial(_mem_decode_kernel, scale=dh ** -0.5, nh=nh),
        grid=(N // S,),
        in_specs=[per, per, cache, cache, pl.BlockSpec((1, dh), lambda b: (0, 0))],
        out_specs=per,
        out_shape=jax.ShapeDtypeStruct((N, 8, dh), F32),
        compiler_params=_params("parallel"),
        name="mem_decode",
    )(q3, z3, k_cache, v_cache, q_gain.reshape(1, dh))


def _merge_kernel(am_ref, af_ref, ac_ref, wm_ref, wf_ref, wc_ref, gm_ref, gf_ref, gc_ref, u_ref):
    d = lambda a, w: jnp.dot(a[...], w[...], preferred_element_type=F32)
    u = _sigmoid(gm_ref[...]) * d(am_ref, wm_ref)
    u = u + _sigmoid(gf_ref[...]) * d(af_ref, wf_ref)
    u = u + _sigmoid(gc_ref[...]) * d(ac_ref, wc_ref)
    u_ref[...] = u.astype(u_ref.dtype)


def _merge(am, af, ac, wm, wf, wc, h2, g_off, *, tm, tn):
    m, k = am.shape
    n = wm.shape[1]
    tm = min(tm, m)
    a_spec = pl.BlockSpec((tm, k), lambda i, j: (i, 0))
    w_spec = pl.BlockSpec((k, tn), lambda i, j: (0, j))
    g_spec = lambda b: pl.BlockSpec((tm, tn), lambda i, j, b=b: (i, (g_off + b * n) // tn + j))
    return pl.pallas_call(
        _merge_kernel,
        grid=(m // tm, n // tn),
        in_specs=[a_spec, a_spec, a_spec, w_spec, w_spec, w_spec, g_spec(0), g_spec(1), g_spec(2)],
        out_specs=pl.BlockSpec((tm, tn), lambda i, j: (i, j)),
        out_shape=jax.ShapeDtypeStruct((m, n), BF16),
        compiler_params=_params("parallel", "arbitrary"),
        name="merge",
    )(am, af, ac, wm, wf, wc, h2, h2, h2)


def _out_proj_kernel(u_ref, w_ref, x_ref, y_ref):
    y_ref[...] = x_ref[...] + jnp.dot(u_ref[...], w_ref[...], preferred_element_type=F32)


def _out_proj(u, w, x, *, tm, tn):
    m, k = u.shape
    n = w.shape[1]
    tm = min(tm, m)
    return pl.pallas_call(
        _out_proj_kernel,
        grid=(m // tm, n // tn),
        in_specs=[
            pl.BlockSpec((tm, k), lambda i, j: (i, 0)),
            pl.BlockSpec((k, tn), lambda i, j: (0, j)),
            pl.BlockSpec((tm, tn), lambda i, j: (i, j)),
        ],
        out_specs=pl.BlockSpec((tm, tn), lambda i, j: (i, j)),
        out_shape=jax.ShapeDtypeStruct((m, n), F32),
        compiler_params=_params("parallel", "arbitrary"),
        name="out_proj",
    )(u, w, x)


def _column_layout(d_model):
    ml_dv = d_model // ML_HEADS
    ml_qkw = ML_HEADS * (ml_dv // 2)
    names = ("mq", "mk", "mv", "mi", "mf", "mo", "mz", "fq", "fk", "fv", "ff", "fz", "cq", "cz", "g")
    widths = (ml_qkw, ml_qkw, d_model, ML_HEADS, ML_HEADS, d_model, d_model,
              d_model, d_model, d_model, FOX_HEADS, d_model, d_model, d_model, N_BRANCH * d_model)
    src, off = {}, 0
    for n, w in zip(names, widths):
        src[n] = (off, w)
        off += w
    big, boff = {}, 0
    for n in names:
        if n in ("mi", "mf", "ff"):
            continue
        big[n] = boff
        boff += src[n][1]
    return src, big, boff


def _layer(x_p, x_s, mem_p, fox_k_pool, fox_v_pool, fox_lf_pool, page_table, mem_k_s, mem_v_s,
           c0_s, n0_s, m0_s, norm_in, norm_mem, w_in, w_mem_kv, b_i, b_f, b_ff, ml_out_norm,
           fq_norm, fk_norm, cq_norm, ck_norm, w_br_m, w_br_f, w_br_c, w_out):
    B, T, D = x_p.shape
    N = x_s.shape[0]
    dv = D // ML_HEADS
    dqk = dv // 2
    fox_dh = D // FOX_HEADS
    mem_dh = D // MEM_HEADS
    src, cols, nbig = _column_layout(D)
    w_in_t = w_in.T
    gap1, gap2 = cols["mo"], cols["fz"]
    skip1, skip2 = 2 * ML_HEADS, 2 * ML_HEADS + FOX_HEADS

    def big_rows(tn):
        assert gap1 % tn == 0 and gap2 % tn == 0
        return lambda j: j * tn + jnp.where(j * tn >= gap2, skip2, jnp.where(j * tn >= gap1, skip1, 0))

    rows = lambda n: w_in_t[src[n][0]:src[n][0] + src[n][1]]
    n_gate = 2 * ML_HEADS + FOX_HEADS
    w_gate_t = jnp.concatenate([rows("mi"), rows("mf"), rows("ff"),
                                jnp.zeros((GATE_LANES - n_gate, D), F32)], axis=0)
    b_gate = jnp.concatenate([b_i, b_f, b_ff, jnp.zeros((GATE_LANES - n_gate,), F32)]).reshape(1, GATE_LANES)
    wm, wf, wc, wo = (w.astype(BF16) for w in (w_br_m, w_br_f, w_br_c, w_out))

    xp2 = x_p.reshape(B * T, D)
    xs2 = x_s.reshape(N, D)

    h2, hs2 = _norm_proj(xp2, norm_in, w_in_t, tm=2048, tn=512, n_out=nbig, w_index=big_rows(512),
                         transposed=True, x_rider=xs2)
    g2 = _gates(xp2, norm_in, w_gate_t, b_gate, tm=1024)
    h3 = h2.reshape(B, T, nbig)
    g3 = g2.reshape(B, T, GATE_LANES)

    n_mem = mem_p.shape[1]
    mem2 = mem_p.reshape(B * n_mem, D)
    heads_k = D // mem_dh
    mk_p, _ = _norm_proj(mem2, norm_mem, w_mem_kv, tm=512, tn=mem_dh, n_out=D, w_index=lambda j: j,
                         transposed=False, head_gain=ck_norm)
    mv_p, _ = _norm_proj(mem2, norm_mem, w_mem_kv, tm=512, tn=mem_dh, n_out=D, w_index=lambda j: heads_k + j,
                         transposed=False)

    a_m, c_p, n_p, m_p = _mlstm_prompt(h3, g3, ml_out_norm, cols, dqk=dqk, dv=dv,
                                       chunks_per_step=4, heads_per_step=4)

    qa, ka, vb, kn, fv_p = _fox_prep(h3, g3, fq_norm, fk_norm, cols, width=D, dh=fox_dh, tp=256)
    a_f = _fox_flash(qa, ka, vb, h3, cols, dh=fox_dh, t=1024, sub=256, heads_per_step=2)

    a_c = _mem_attn_prompt(h3, mk_p.reshape(B, n_mem, D), mv_p.reshape(B, n_mem, D), cq_norm, cols,
                           width=D, dh=mem_dh, tq=512)

    u_p = _merge(a_m.reshape(B * T, D), a_f.reshape(B * T, D), a_c.reshape(B * T, D),
                 wm, wf, wc, h2, cols["g"], tm=1024, tn=512)
    y_p = _out_proj(u_p, wo, xp2, tm=1024, tn=1024).reshape(B, T, D)

    gs2 = _gates(xs2, norm_in, w_gate_t, b_gate, tm=128)
    hs3 = hs2.reshape(N, 1, nbig)
    gs3 = gs2.reshape(N, 1, GATE_LANES)

    a_ms, c_s, n_s, m_s = _mlstm_sample(hs3, gs3, m0_s.reshape(N, 1, ML_HEADS), ml_out_norm, c0_s, n0_s,
                                        cols, dqk=dqk, dv=dv, samples_per_step=4)

    hs1 = hs2.reshape(1, N, nbig)
    qn_s, kn_s = _qk_norm(hs1, fq_norm, fk_norm, cols, width=D, dh=fox_dh, tq=128)
    seg = lambda n: hs2[:, cols[n]:cols[n] + D]
    fv_s = seg("fv")
    flf_s = gs2[:, 2 * ML_HEADS:n_gate]
    lfn = jnp.broadcast_to(flf_s[:, :, None], (N, FOX_HEADS, LANES))
    to_heads = lambda a: a.reshape(N, FOX_HEADS, fox_dh)
    o_f = _fox_decode(page_table, to_heads(qn_s.reshape(N, D)), to_heads(kn_s.reshape(N, D)), to_heads(fv_s),
                      to_heads(seg("fz")), lfn, fox_k_pool, fox_v_pool, jnp.swapaxes(fox_lf_pool, 1, 2),
                      dh=fox_dh, G=8, n_buf=3)
    a_fs = o_f.reshape(N, D).astype(BF16)

    pad8 = lambda a: jnp.pad(a.reshape(N, MEM_HEADS, mem_dh), ((0, 0), (0, 8 - MEM_HEADS), (0, 0)))
    o_c = _mem_decode(pad8(seg("cq")), pad8(seg("cz")), mem_k_s, mem_v_s, cq_norm, dh=mem_dh, samples_per_step=4)
    a_cs = o_c[:, :MEM_HEADS].reshape(N, D).astype(BF16)

    u_s = _merge(a_ms.reshape(N, D), a_fs, a_cs, wm, wf, wc, hs2, cols["g"], tm=128, tn=512)
    y_s = _out_proj(u_s, wo, xs2, tm=128, tn=1024).reshape(N, 1, D)

    outs_p = (kn.reshape(B, T, FOX_HEADS, fox_dh), fv_p.reshape(B, T, FOX_HEADS, fox_dh),
              g3[:, :, 2 * ML_HEADS:n_gate],
              mk_p.reshape(B, n_mem, MEM_HEADS, mem_dh), mv_p.reshape(B, n_mem, MEM_HEADS, mem_dh),
              c_p, n_p.reshape(B, ML_HEADS, dqk), m_p.reshape(B, ML_HEADS))
    outs_s = (kn_s.reshape(N, 1, FOX_HEADS, fox_dh), fv_s.reshape(N, 1, FOX_HEADS, fox_dh),
              flf_s.reshape(N, 1, FOX_HEADS), c_s, n_s, m_s.reshape(N, ML_HEADS))
    return y_p, y_s, outs_p, outs_s


def kernel(x_prompt, x_sample, mem_prompt, cache_fox_k, cache_fox_v, cache_fox_logf, page_table, cache_mem_k, cache_mem_v, state_mlstm_C, state_mlstm_n, state_mlstm_m, norm_in, norm_mem, w_in, w_mem_kv, b_mlstm_i, b_mlstm_f, b_fox_f, mlstm_out_norm, fox_q_norm, fox_k_norm, mem_q_norm, mem_k_norm, w_br_mlstm, w_br_fox, w_br_mem, w_out):
    depth = w_in.shape[0]
    y_p, y_s = x_prompt, x_sample
    per_layer = []
    for l in range(depth):
        y_p, y_s, outs_p, outs_s = _layer(
            y_p, y_s, mem_prompt, cache_fox_k[l], cache_fox_v[l], cache_fox_logf[l], page_table,
            cache_mem_k[l], cache_mem_v[l], state_mlstm_C[l], state_mlstm_n[l], state_mlstm_m[l],
            norm_in[l], norm_mem[l], w_in[l], w_mem_kv[l], b_mlstm_i[l], b_mlstm_f[l], b_fox_f[l],
            mlstm_out_norm[l], fox_q_norm[l], fox_k_norm[l], mem_q_norm[l], mem_k_norm[l],
            w_br_mlstm[l], w_br_fox[l], w_br_mem[l], w_out[l])
        per_layer.append(outs_p + outs_s)
    stacked = tuple(jnp.stack([lay[i] for lay in per_layer]) for i in range(len(per_layer[0])))
    return (y_p, y_s) + stacked
```
